```python
import math
import jax, jax.numpy as jnp
from jax import lax
import numpy as np

D_MODEL = 1024
BATCH = 16
SEQ = 2048
DEPTH = 1
DEC_BATCH = 8
DEC_SEQ = 64
PAST_LEN = 2048

CHUNK = 64
D_MIX = D_MODEL
C_CONV = D_MIX // 2
CONV_WIDTH = 31
N_HEADS = 4
HEAD_DIM = (D_MIX - C_CONV) // (2 * N_HEADS)
V_DIM = 2 * HEAD_DIM
QK_WIDTH = N_HEADS * 2 * HEAD_DIM
ATTN_WIDTH = N_HEADS * V_DIM
D_IN = 2 * C_CONV + 2 * QK_WIDTH + ATTN_WIDTH
N_BUCKETS = 32
MAX_DISTANCE = 128
Q_BLOCK = 128
N_KEYS = 128
N_EXPERTS = N_KEYS * N_KEYS
R_HEADS = 8
TOPK = 16
D_QUERY = 256
D_HALF = D_QUERY // 2
PEER_BLOCK = 128
EPS = 1e-6
NEG = -1e30

kernel_name = "hymba_conformer_diffattn_peer_stream"


def _lambda_init(layer):
    return 0.8 - 0.6 * math.exp(-0.3 * layer)


def _rmsnorm(x, g):
    xf = x.astype(jnp.float32)
    y = xf * lax.rsqrt(jnp.mean(xf * xf, axis=-1, keepdims=True) + EPS)
    return (y * g.astype(jnp.float32)).astype(x.dtype)


def _rel_bucket(rel):
    nb = N_BUCKETS // 2
    max_exact = nb // 2
    ret = jnp.where(rel > 0, nb, 0)
    n = jnp.abs(rel)
    nf = jnp.maximum(n, 1).astype(jnp.float32)
    large = max_exact + (jnp.log(nf / max_exact) / math.log(MAX_DISTANCE / max_exact)
                         * (nb - max_exact)).astype(jnp.int32)
    large = jnp.minimum(large, nb - 1)
    return ret + jnp.where(n < max_exact, n, large)


def _project(x, g_mix, w_in):
    B, S, _ = x.shape
    z = _rmsnorm(x, g_mix) @ w_in
    glu_in, q, k, v = jnp.split(z, [2 * C_CONV, 2 * C_CONV + QK_WIDTH, 2 * C_CONV + 2 * QK_WIDTH], axis=-1)
    a = glu_in[..., :C_CONV] * jax.nn.sigmoid(glu_in[..., C_CONV:])
    q = q.reshape(B, S, N_HEADS, 2, HEAD_DIM)
    k = k.reshape(B, S, N_HEADS, 2, HEAD_DIM)
    v = v.reshape(B, S, N_HEADS, V_DIM)
    return a, q, k, v


def _conv_branch(a, left, conv_w, conv_b, ln_g, ln_b):
    padded = jnp.concatenate([left, a], axis=1)
    y = lax.conv_general_dilated(padded, conv_w[:, None, :], window_strides=(1,), padding='VALID',
                                 dimension_numbers=('NWC', 'WIO', 'NWC'), feature_group_count=C_CONV)
    yf = (y + conv_b).astype(jnp.float32)
    mu = jnp.mean(yf, axis=-1, keepdims=True)
    var = jnp.mean((yf - mu) ** 2, axis=-1, keepdims=True)
    yn = (yf - mu) * lax.rsqrt(var + EPS) * ln_g.astype(jnp.float32) + ln_b.astype(jnp.float32)
    return jax.nn.silu(yn).astype(a.dtype), padded[:, -(CONV_WIDTH - 1):]


def _diff_attn(q, k, v, q_pos, k_pos, rel_bias, lam):
    logits = jnp.einsum('bqhmd,bkhmd->bhmqk', q, k).astype(jnp.float32) * (HEAD_DIM ** -0.5)
    bias = jnp.moveaxis(rel_bias[_rel_bucket(k_pos[None, :] - q_pos[:, None])].astype(jnp.float32), -1, 0)
    logits = logits + bias[None, :, None]
    mask = (k_pos[None, :] // CHUNK) <= (q_pos[:, None] // CHUNK)
    p = jax.nn.softmax(jnp.where(mask, logits, NEG), axis=-1)
    attn = p[:, :, 0] - lam * p[:, :, 1]
    return jnp.einsum('bhqk,bkhe->bqhe', attn.astype(v.dtype), v)


def _subln(o, g, lambda_init):
    of = o.astype(jnp.float32)
    y = of * lax.rsqrt(jnp.mean(of * of, axis=-1, keepdims=True) + EPS) * g.astype(jnp.float32)
    y = y * (1.0 - lambda_init)
    B, S = o.shape[:2]
    return y.reshape(B, S, ATTN_WIDTH).astype(o.dtype)


def _peer(h, w_query, sub_keys, peer_u, peer_v):
    B, S, D = h.shape
    T = B * S
    n_blk = -(-T // PEER_BLOCK)
    flat = jnp.pad(h.reshape(T, D), ((0, n_blk * PEER_BLOCK - T), (0, 0))).reshape(n_blk, PEER_BLOCK, D)

    def one_block(xb):
        q = (xb @ w_query).reshape(PEER_BLOCK, R_HEADS, 2, D_HALF)
        s = jnp.einsum('trpd,rpnd->trpn', q, sub_keys).astype(jnp.float32)
        s_top, i_top = lax.top_k(s, TOPK)
        cand = s_top[:, :, 0, :, None] + s_top[:, :, 1, None, :]
        cand_idx = i_top[:, :, 0, :, None] * N_KEYS + i_top[:, :, 1, None, :]
        best, pos = lax.top_k(cand.reshape(PEER_BLOCK, R_HEADS, TOPK * TOPK), TOPK)
        idx = jnp.take_along_axis(cand_idx.reshape(PEER_BLOCK, R_HEADS, TOPK * TOPK), pos, axis=-1)
        gate = jax.nn.softmax(best, axis=-1)
        act = jax.nn.gelu(jnp.einsum('trkd,td->trk', peer_u[idx], xb).astype(jnp.float32), approximate=False)
        coef = (gate * act).astype(xb.dtype)
        return jnp.einsum('trk,trkd->td', coef, peer_v[idx])

    out = lax.map(one_block, flat)
    return out.reshape(n_blk * PEER_BLOCK, D)[:T].reshape(B, S, D)


def setup_inputs(seed: int = 0) -> dict:
    key = jax.random.key(seed)
    ks = jax.random.split(key, 24)
    nrm = lambda k, shape, s: jax.random.normal(k, shape, jnp.float32) * s
    return {
        "x_prompt": nrm(ks[0], (BATCH, SEQ, D_MODEL), 1.0),
        "x_sample": nrm(ks[1], (DEC_BATCH, DEC_SEQ, D_MODEL), 1.0),
        "cache_k": nrm(ks[2], (DEPTH, DEC_BATCH, PAST_LEN, N_HEADS, 2, HEAD_DIM), 1.0),
        "cache_v": nrm(ks[3], (DEPTH, DEC_BATCH, PAST_LEN, N_HEADS, V_DIM), 1.0),
        "state_conv": nrm(ks[4], (DEPTH, DEC_BATCH, CONV_WIDTH - 1, C_CONV), 0.5),
        "g_mix": 1.0 + nrm(ks[5], (DEPTH, D_MODEL), 0.02),
        "w_in": nrm(ks[6], (DEPTH, D_MODEL, D_IN), D_MODEL ** -0.5),
        "conv_w": nrm(ks[7], (DEPTH, CONV_WIDTH, C_CONV), CONV_WIDTH ** -0.5),
        "conv_b": nrm(ks[8], (DEPTH, C_CONV), 0.02),
        "conv_ln_g": 1.0 + nrm(ks[9], (DEPTH, C_CONV), 0.02),
        "conv_ln_b": nrm(ks[10], (DEPTH, C_CONV), 0.02),
        "lambda_q1": nrm(ks[11], (DEPTH, HEAD_DIM), 0.1),
        "lambda_k1": nrm(ks[12], (DEPTH, HEAD_DIM), 0.1),
        "lambda_q2": nrm(ks[13], (DEPTH, HEAD_DIM), 0.1),
        "lambda_k2": nrm(ks[14], (DEPTH, HEAD_DIM), 0.1),
        "subln_g": 1.0 + nrm(ks[15], (DEPTH, V_DIM), 0.02),
        "rel_bias": nrm(ks[16], (N_BUCKETS, N_HEADS), 0.5),
        "w_out": nrm(ks[17], (DEPTH, D_MIX, D_MODEL), D_MIX ** -0.5),
        "g_ffn": 1.0 + nrm(ks[18], (DEPTH, D_MODEL), 0.02),
        "w_query": nrm(ks[19], (DEPTH, D_MODEL, R_HEADS * D_QUERY), D_MODEL ** -0.5),
        "sub_keys": nrm(ks[20], (DEPTH, R_HEADS, 2, N_KEYS, D_HALF), D_HALF ** -0.5),
        "peer_u": nrm(ks[21], (DEPTH, N_EXPERTS, D_MODEL), D_MODEL ** -0.5),
        "peer_v": nrm(ks[22], (DEPTH, N_EXPERTS, D_MODEL), 0.25),
        "g_final": 1.0 + nrm(ks[23], (D_MODEL,), 0.02),
    }


def reference(x_prompt, x_sample, cache_k, cache_v, state_conv, g_mix, w_in, conv_w, conv_b,
              conv_ln_g, conv_ln_b, lambda_q1, lambda_k1, lambda_q2, lambda_k2, subln_g, rel_bias,
              w_out, g_ffn, w_query, sub_keys, peer_u, peer_v, g_final):
    B, S, _ = x_prompt.shape
    Bd, Sd, _ = x_sample.shape
    past = cache_k.shape[2]
    pos_p = jnp.arange(S, dtype=jnp.int32)
    pos_s = past + jnp.arange(Sd, dtype=jnp.int32)
    pos_all = jnp.arange(past + Sd, dtype=jnp.int32)

    xp, xs = x_prompt, x_sample
    kp_l, vp_l, cp_l, ks_l, vs_l, cs_l = [], [], [], [], [], []
    for l in range(DEPTH):
        lam_init = _lambda_init(l)
        lam = (jnp.exp(jnp.sum(lambda_q1[l].astype(jnp.float32) * lambda_k1[l].astype(jnp.float32)))
               - jnp.exp(jnp.sum(lambda_q2[l].astype(jnp.float32) * lambda_k2[l].astype(jnp.float32)))
               + lam_init)

        a_p, q_p, k_p, v_p = _project(xp, g_mix[l], w_in[l])
        zero_left = jnp.zeros((B, CONV_WIDTH - 1, C_CONV), a_p.dtype)
        conv_p, tail_p = _conv_branch(a_p, zero_left, conv_w[l], conv_b[l], conv_ln_g[l], conv_ln_b[l])
        outs = []
        for start in range(0, S, Q_BLOCK):
            stop = start + Q_BLOCK
            outs.append(_diff_attn(q_p[:, start:stop], k_p[:, :stop], v_p[:, :stop],
                                   pos_p[start:stop], pos_p[:stop], rel_bias, lam))
        att_p = _subln(jnp.concatenate(outs, axis=1), subln_g[l], lam_init)
        xp = xp + jnp.concatenate([conv_p, att_p], axis=-1) @ w_out[l]
        xp = xp + _peer(_rmsnorm(xp, g_ffn[l]), w_query[l], sub_keys[l], peer_u[l], peer_v[l])

        a_s, q_s, k_s, v_s = _project(xs, g_mix[l], w_in[l])
        conv_s, tail_s = _conv_branch(a_s, state_conv[l], conv_w[l], conv_b[l], conv_ln_g[l], conv_ln_b[l])
        keys = jnp.concatenate([cache_k[l], k_s], axis=1)
        vals = jnp.concatenate([cache_v[l], v_s], axis=1)
        att_s = _subln(_diff_attn(q_s, keys, vals, pos_s, pos_all, rel_bias, lam), subln_g[l], lam_init)
        xs = xs + jnp.concatenate([conv_s, att_s], axis=-1) @ w_out[l]
        xs = xs + _peer(_rmsnorm(xs, g_ffn[l]), w_query[l], sub_keys[l], peer_u[l], peer_v[l])

        kp_l.append(k_p); vp_l.append(v_p); cp_l.append(tail_p)
        ks_l.append(k_s); vs_l.append(v_s); cs_l.append(tail_s)

    y_prompt = _rmsnorm(xp, g_final)
    y_sample = _rmsnorm(xs, g_final)
    return (y_prompt, y_sample, jnp.stack(kp_l), jnp.stack(vp_l), jnp.stack(cp_l),
            jnp.stack(ks_l), jnp.stack(vs_l), jnp.stack(cs_l))
```

```python
import functools
import math

import jax
import jax.numpy as jnp
from jax import lax
from jax.experimental import pallas as pl
from jax.experimental.pallas import tpu as pltpu

CHUNK = 64
CONV_WIDTH = 31
CONV_PAD = 32
N_HEADS = 4
HEAD_DIM = 64
V_DIM = 2 * HEAD_DIM
N_BUCKETS = 32
MAX_DISTANCE = 128
N_KEYS = 128
R_HEADS = 8
TOPK = 16
EPS = 1e-6
NEG = -1e30
NEG_BIG = -3.0e38
LANES = 128
SUBLANES = 8
VMEM_LIMIT = 48 * 1024 * 1024

_NT = (((1,), (1,)), ((), ()))


def _lambda_init(layer):
    return 0.8 - 0.6 * math.exp(-0.3 * layer)


def _rms(xf, g):
    return xf * lax.rsqrt(jnp.mean(xf * xf, axis=-1, keepdims=True) + EPS) * g


def _sigmoid(x):
    return 1.0 / (1.0 + jnp.exp(-x))


def _in_proj_kernel(x_ref, g_ref, w_ref, left_ref, cw_ref, cb_ref, lg_ref, lb_ref,
                    q_ref, k_ref, v_ref, kb_ref, vb_ref, conv_ref, tail_ref, abuf, *, ts, c):
    s = pl.program_id(1)
    h = _rms(x_ref[0], g_ref[...]).astype(jnp.bfloat16)

    @pl.when(s == 0)
    def _():
        abuf[0:CONV_PAD, :] = left_ref[0]

    glu_in = jnp.dot(h, w_ref[:, 0:2 * c], preferred_element_type=jnp.float32)
    abuf[CONV_PAD:CONV_PAD + ts, :] = glu_in[:, :c] * _sigmoid(glu_in[:, c:])
    q = jnp.dot(h, w_ref[:, 2 * c:3 * c], preferred_element_type=jnp.float32)
    q_ref[0] = (q * (HEAD_DIM ** -0.5)).astype(jnp.bfloat16)
    k = jnp.dot(h, w_ref[:, 3 * c:4 * c], preferred_element_type=jnp.float32)
    k_ref[0] = k
    kb_ref[0] = k.astype(jnp.bfloat16)
    v = jnp.dot(h, w_ref[:, 4 * c:5 * c], preferred_element_type=jnp.float32)
    v_ref[0] = v
    vb_ref[0] = v.astype(jnp.bfloat16)

    rc = min(ts, 64)
    for r0 in range(0, ts, rc):
        acc = jnp.zeros((rc, c), jnp.float32)
        for w in range(CONV_WIDTH):
            off = r0 + CONV_PAD - (CONV_WIDTH - 1) + w
            acc = acc + abuf[off:off + rc, :] * cw_ref[w:w + 1, :]
        y = acc + cb_ref[...]
        mu = jnp.mean(y, axis=-1, keepdims=True)
        d = y - mu
        var = jnp.mean(d * d, axis=-1, keepdims=True)
        yn = d * lax.rsqrt(var + EPS) * lg_ref[...] + lb_ref[...]
        conv_ref[0, r0:r0 + rc, :] = (yn * _sigmoid(yn)).astype(jnp.bfloat16)

    tail = abuf[ts:ts + CONV_PAD, :]
    tail_ref[0] = tail
    abuf[0:CONV_PAD, :] = tail


def _in_proj(x, g_mix, w_in_bf, left, conv_w, conv_b, ln_g, ln_b):
    b, s, d = x.shape
    c = conv_w.shape[1]
    ts = min(s, 512)
    assert s % ts == 0 and ts >= CONV_PAD and ts % SUBLANES == 0
    cw = jnp.pad(conv_w, ((0, CONV_PAD - CONV_WIDTH), (0, 0)))
    row = lambda a: a.reshape(1, -1)
    tok = lambda bi, si: (bi, si, 0)
    const2 = lambda bi, si: (0, 0)
    f32, bf16 = jnp.float32, jnp.bfloat16
    outs = pl.pallas_call(
        functools.partial(_in_proj_kernel, ts=ts, c=c),
        grid=(b, s // ts),
        in_specs=[
            pl.BlockSpec((1, ts, d), tok),
            pl.BlockSpec((1, d), const2),
            pl.BlockSpec(w_in_bf.shape, const2),
            pl.BlockSpec((1, CONV_PAD, c), lambda bi, si: (bi, 0, 0)),
            pl.BlockSpec((CONV_PAD, c), const2),
            pl.BlockSpec((1, c), const2),
            pl.BlockSpec((1, c), const2),
            pl.BlockSpec((1, c), const2),
        ],
        out_specs=[pl.BlockSpec((1, ts, c), tok)] * 6
        + [pl.BlockSpec((1, CONV_PAD, c), lambda bi, si: (bi, 0, 0))],
        out_shape=[
            jax.ShapeDtypeStruct((b, s, c), bf16),
            jax.ShapeDtypeStruct((b, s, c), f32),
            jax.ShapeDtypeStruct((b, s, c), f32),
            jax.ShapeDtypeStruct((b, s, c), bf16),
            jax.ShapeDtypeStruct((b, s, c), bf16),
            jax.ShapeDtypeStruct((b, s, c), bf16),
            jax.ShapeDtypeStruct((b, CONV_PAD, c), f32),
        ],
        scratch_shapes=[pltpu.VMEM((ts + CONV_PAD, c), f32)],
        compiler_params=pltpu.CompilerParams(
            dimension_semantics=("arbitrary", "arbitrary"), vmem_limit_bytes=VMEM_LIMIT),
        name="in_proj",
    )(x, row(g_mix), w_in_bf, left, cw, row(conv_b), row(ln_g), row(ln_b))
    return outs


def _rel_bucket(rel):
    nb = N_BUCKETS // 2
    max_exact = nb // 2
    ret = jnp.where(rel > 0, nb, 0)
    n = jnp.abs(rel)
    nf = jnp.maximum(n, 1).astype(jnp.float32)
    large = max_exact + (jnp.log(nf / max_exact) / math.log(MAX_DISTANCE / max_exact)
                         * (nb - max_exact)).astype(jnp.int32)
    large = jnp.minimum(large, nb - 1)
    return ret + jnp.where(n < max_exact, n, large)


def _bias_table(rel_bias, q_pos, k_pos, masked):
    bias = jnp.moveaxis(rel_bias[_rel_bucket(k_pos[None, :] - q_pos[:, None])].astype(jnp.float32), -1, 0)
    if masked:
        mask = (k_pos[None, :] // CHUNK) <= (q_pos[:, None] // CHUNK)
        bias = jnp.where(mask[None], bias, NEG)
    return bias


def _split_maps(q):
    lane = lax.broadcasted_iota(jnp.int32, q.shape, 1)
    zero = jnp.zeros_like(q)
    return jnp.where(lane < HEAD_DIM, q, zero), jnp.where(lane >= HEAD_DIM, q, zero)


def _softmax_rows(s):
    p = jnp.exp(s - jnp.max(s, axis=-1, keepdims=True))
    return p / jnp.sum(p, axis=-1, keepdims=True)


def _attn_finish(s1, s2, vv, lam, g, out_scale):
    attn = (_softmax_rows(s1) - lam * _softmax_rows(s2)).astype(jnp.bfloat16)
    o = jnp.dot(attn, vv, preferred_element_type=jnp.float32)
    return (_rms(o, g) * out_scale).astype(jnp.bfloat16)


def _attn_prompt_kernel(lam_ref, q_ref, k_ref, v_ref, slab_ref, g_ref, o_ref, s1_ref, s2_ref,
                        *, tq, n_kb, out_scale):
    qi = pl.program_id(2)
    q1, q2 = _split_maps(q_ref[0])
    for j in range(n_kb):
        kj = k_ref[0, j * tq:(j + 1) * tq, :]
        kind = jnp.clip(j - qi, -2, 1) + 2
        bias = slab_ref[0, kind]
        s1_ref[:, j * tq:(j + 1) * tq] = lax.dot_general(q1, kj, _NT, preferred_element_type=jnp.float32) + bias
        s2_ref[:, j * tq:(j + 1) * tq] = lax.dot_general(q2, kj, _NT, preferred_element_type=jnp.float32) + bias
    o_ref[0] = _attn_finish(s1_ref[...], s2_ref[...], v_ref[0], lam_ref[0], g_ref[...], out_scale)


def _attn_prompt(q, kb, vb, rel_bias, lam, subln_g, out_scale):
    b, s, c = q.shape
    tq = 128
    assert s % tq == 0 and tq % CHUNK == 0 and c == N_HEADS * V_DIM
    n_kb = s // tq
    pos = jnp.arange(tq, dtype=jnp.int32)
    slabs = jnp.stack([
        _bias_table(rel_bias, pos + 2 * tq, pos, False),
        _bias_table(rel_bias, pos + tq, pos, False),
        _bias_table(rel_bias, pos, pos, True),
        jnp.full((N_HEADS, tq, tq), NEG, jnp.float32),
    ], axis=1)
    return pl.pallas_call(
        functools.partial(_attn_prompt_kernel, tq=tq, n_kb=n_kb, out_scale=out_scale),
        grid=(b, N_HEADS, n_kb),
        in_specs=[
            pl.BlockSpec(memory_space=pltpu.SMEM),
            pl.BlockSpec((1, tq, V_DIM), lambda bi, hi, qi: (bi, qi, hi)),
            pl.BlockSpec((1, s, V_DIM), lambda bi, hi, qi: (bi, 0, hi)),
            pl.BlockSpec((1, s, V_DIM), lambda bi, hi, qi: (bi, 0, hi)),
            pl.BlockSpec((1, 4, tq, tq), lambda bi, hi, qi: (hi, 0, 0, 0)),
            pl.BlockSpec((1, V_DIM), lambda bi, hi, qi: (0, 0)),
        ],
        out_specs=pl.BlockSpec((1, tq, V_DIM), lambda bi, hi, qi: (bi, qi, hi)),
        out_shape=jax.ShapeDtypeStruct((b, s, c), jnp.bfloat16),
        scratch_shapes=[pltpu.VMEM((tq, s), jnp.float32), pltpu.VMEM((tq, s), jnp.float32)],
        compiler_params=pltpu.CompilerParams(
            dimension_semantics=("arbitrary",) * 3, vmem_limit_bytes=VMEM_LIMIT),
        name="attn_prompt",
    )(lam, q, kb, vb, slabs, subln_g.reshape(1, -1))


def _attn_sample_kernel(lam_ref, q_ref, k_ref, v_ref, bias_ref, g_ref, o_ref, *, out_scale):
    q1, q2 = _split_maps(q_ref[0])
    kk = k_ref[0]
    bias = bias_ref[0]
    s1 = lax.dot_general(q1, kk, _NT, preferred_element_type=jnp.float32) + bias
    s2 = lax.dot_general(q2, kk, _NT, preferred_element_type=jnp.float32) + bias
    o_ref[0] = _attn_finish(s1, s2, v_ref[0], lam_ref[0], g_ref[...], out_scale)


def _attn_sample(q, keys, vals, bias, lam, subln_g, out_scale):
    b, sq, c = q.shape
    sk = keys.shape[1]
    return pl.pallas_call(
        functools.partial(_attn_sample_kernel, out_scale=out_scale),
        grid=(b, N_HEADS),
        in_specs=[
            pl.BlockSpec(memory_space=pltpu.SMEM),
            pl.BlockSpec((1, sq, V_DIM), lambda bi, hi: (bi, 0, hi)),
            pl.BlockSpec((1, sk, V_DIM), lambda bi, hi: (bi, 0, hi)),
            pl.BlockSpec((1, sk, V_DIM), lambda bi, hi: (bi, 0, hi)),
            pl.BlockSpec((1, sq, sk), lambda bi, hi: (hi, 0, 0)),
            pl.BlockSpec((1, V_DIM), lambda bi, hi: (0, 0)),
        ],
        out_specs=pl.BlockSpec((1, sq, V_DIM), lambda bi, hi: (bi, 0, hi)),
        out_shape=jax.ShapeDtypeStruct((b, sq, c), jnp.bfloat16),
        compiler_params=pltpu.CompilerParams(
            dimension_semantics=("arbitrary",) * 2, vmem_limit_bytes=VMEM_LIMIT),
        name="attn_sample",
    )(lam, q, keys, vals, bias, subln_g.reshape(1, -1))


def _mid_kernel(conv_ref, att_ref, x_ref, wc_ref, wa_ref, g_ref, wq_ref, sk_ref,
                x1_ref, h2_ref, st_ref):
    x1 = (x_ref[...]
          + jnp.dot(conv_ref[...], wc_ref[...], preferred_element_type=jnp.float32)
          + jnp.dot(att_ref[...], wa_ref[...], preferred_element_type=jnp.float32))
    x1_ref[...] = x1
    h2 = _rms(x1, g_ref[...]).astype(jnp.bfloat16)
    h2_ref[...] = h2
    qq = jnp.dot(h2, wq_ref[...], preferred_element_type=jnp.float32).astype(jnp.bfloat16)
    for rp in range(2 * R_HEADS):
        st_ref[rp] = lax.dot_general(sk_ref[rp], qq[:, rp * N_KEYS:(rp + 1) * N_KEYS], _NT,
                                     preferred_element_type=jnp.float32)


def _mid(conv, att, x2d, w_out_bf, g_ffn, w_query_bf, sub_keys_bf):
    t, d = x2d.shape
    c = conv.shape[1]
    tb = min(t, 512)
    assert t % tb == 0
    dq = w_query_bf.shape[1]
    nrp = sub_keys_bf.shape[0]
    tok = lambda i: (i, 0)
    const = lambda i: (0, 0)
    return pl.pallas_call(
        _mid_kernel,
        grid=(t // tb,),
        in_specs=[
            pl.BlockSpec((tb, c), tok),
            pl.BlockSpec((tb, c), tok),
            pl.BlockSpec((tb, d), tok),
            pl.BlockSpec((c, d), const),
            pl.BlockSpec((c, d), lambda i: (1, 0)),
            pl.BlockSpec((1, d), const),
            pl.BlockSpec((d, dq), const),
            pl.BlockSpec(sub_keys_bf.shape, lambda i: (0, 0, 0)),
        ],
        out_specs=[
            pl.BlockSpec((tb, d), tok),
            pl.BlockSpec((tb, d), tok),
            pl.BlockSpec((nrp, N_KEYS, tb), lambda i: (0, 0, i)),
        ],
        out_shape=[
            jax.ShapeDtypeStruct((t, d), jnp.float32),
            jax.ShapeDtypeStruct((t, d), jnp.bfloat16),
            jax.ShapeDtypeStruct((nrp, N_KEYS, t), jnp.float32),
        ],
        compiler_params=pltpu.CompilerParams(
            dimension_semantics=("arbitrary",), vmem_limit_bytes=VMEM_LIMIT),
        name="mid",
    )(conv, att, x2d, w_out_bf, w_out_bf, g_ffn.reshape(1, -1), w_query_bf, sub_keys_bf)


def _ce(a, b):
    if a is None:
        return b, None
    if b is None:
        return a, None
    return jnp.maximum(a, b), jnp.minimum(a, b)


def _sort_desc(xs):
    xs = list(xs)
    n = len(xs)
    p = 1
    while p < n:
        k = p
        while k >= 1:
            for j in range(k % p, n - k, 2 * k):
                for i in range(min(k, n - j - k)):
                    if (i + j) // (2 * p) == (i + j + k) // (2 * p):
                        xs[i + j], xs[i + j + k] = _ce(xs[i + j], xs[i + j + k])
            k //= 2
        p *= 2
    return xs


def _bitonic_top(a, b):
    n = len(a)
    return [_ce(a[i], b[n - 1 - i])[0] for i in range(n)]


def _bitonic_sort_desc(xs):
    xs = list(xs)
    n = len(xs)
    d = n // 2
    while d >= 1:
        for i in range(n):
            if i & d == 0:
                xs[i], xs[i + d] = _ce(xs[i], xs[i + d])
        d //= 2
    return xs


def _fill(xs):
    return [jnp.full((SUBLANES, LANES), NEG_BIG, jnp.float32) if x is None else x for x in xs]


def _sublane_merge_sorted(xs):
    for shift in (4, 2, 1):
        other = [pltpu.roll(x, shift, 0) for x in xs]
        xs = _bitonic_sort_desc(_bitonic_top(xs, other))
    return xs


def _sublane_merge_kth(xs):
    for shift in (4, 2):
        other = [pltpu.roll(x, shift, 0) for x in xs]
        xs = _bitonic_sort_desc(_bitonic_top(xs, other))
    other = [pltpu.roll(x, 1, 0) for x in xs]
    top = _bitonic_top(xs, other)
    return functools.reduce(jnp.minimum, top)


def _top16_rows(s):
    tiles = [s[i * SUBLANES:(i + 1) * SUBLANES, :] for i in range(N_KEYS // SUBLANES)]
    return _sublane_merge_sorted(_sort_desc(tiles))


def _route_kernel(st_ref, thr_ref, ea_ref, eb_ref, *, tb):
    sub = lax.broadcasted_iota(jnp.int32, (SUBLANES, LANES), 0)
    for g in range(tb // LANES):
        cols = slice(g * LANES, (g + 1) * LANES)
        for r in range(R_HEADS):
            s1 = st_ref[2 * r, :, cols]
            s2 = st_ref[2 * r + 1, :, cols]
            v1 = _top16_rows(s1)
            v2 = _top16_rows(s2)
            pack = lambda v, base: functools.reduce(
                lambda acc, j: jnp.where(sub == j, v[base + j], acc), range(1, SUBLANES), v[base])
            w1a, w1b, w2a, w2b = pack(v1, 0), pack(v1, 8), pack(v2, 0), pack(v2, 8)
            neg = jnp.full((SUBLANES, LANES), NEG_BIG, jnp.float32)
            cands = [
                v1[0] + w2a,
                v1[0] + w2b,
                jnp.where(sub >= 1, v2[0] + w1a, neg),
                v2[0] + w1b,
                jnp.where(sub >= 1, v1[1] + w2a, neg),
                jnp.where(sub >= 2, v2[1] + w1a, neg),
                jnp.where((sub >= 2) & (sub <= 4), v1[2] + w2a, neg),
                jnp.where((sub >= 2) & (sub <= 3), v1[3] + w2a, neg),
                jnp.where(sub == 2, v1[4] + w2a, neg),
            ]
            srt = _fill(_sort_desc(cands + [None] * (TOPK - len(cands))))
            thr = _sublane_merge_kth(srt)
            m1, m2 = v1[0], v2[0]
            top = m1 + m2
            z = functools.reduce(
                lambda acc, cnd: acc + jnp.where(cnd >= thr, jnp.exp(cnd - top), 0.0), cands,
                jnp.zeros((SUBLANES, LANES), jnp.float32))
            for shift in (4, 2, 1):
                z = z + pltpu.roll(z, shift, 0)
            thr_ref[r:r + 1, cols] = thr[0:1, :]
            ea_ref[r, :, cols] = jnp.exp(s1 - m1[0:1, :]) / z[0:1, :]
            eb_ref[r, :, cols] = jnp.exp(s2 - m2[0:1, :])


def _route(st):
    nrp, nk, t = st.shape
    tb = min(t, 256)
    assert t % tb == 0 and tb % LANES == 0 and nk == N_KEYS and nrp == 2 * R_HEADS
    return pl.pallas_call(
        functools.partial(_route_kernel, tb=tb),
        grid=(t // tb,),
        in_specs=[pl.BlockSpec((nrp, nk, tb), lambda i: (0, 0, i))],
        out_specs=[
            pl.BlockSpec((R_HEADS, tb), lambda i: (0, i)),
            pl.BlockSpec((R_HEADS, nk, tb), lambda i: (0, 0, i)),
            pl.BlockSpec((R_HEADS, nk, tb), lambda i: (0, 0, i)),
        ],
        out_shape=[
            jax.ShapeDtypeStruct((R_HEADS, t), jnp.float32),
            jax.ShapeDtypeStruct((R_HEADS, nk, t), jnp.float32),
            jax.ShapeDtypeStruct((R_HEADS, nk, t), jnp.float32),
        ],
        compiler_params=pltpu.CompilerParams(
            dimension_semantics=("arbitrary",), vmem_limit_bytes=VMEM_LIMIT),
        name="route",
    )(st)


def _gelu(x):
    return 0.5 * x * (1.0 + lax.erf(x * (2.0 ** -0.5)))


def _peer_kernel(h_ref, u_ref, vt_ref, st_ref, thr_ref, ea_ref, eb_ref, x1_ref, g_ref,
                 y_ref, acc_ref, *, eb_rows, tb):
    j = pl.program_id(1)

    @pl.when(j == 0)
    def _():
        acc_ref[...] = jnp.zeros_like(acc_ref)

    act = lax.dot_general(u_ref[...], h_ref[...], _NT, preferred_element_type=jnp.float32)
    n_i1 = eb_rows // N_KEYS
    coefs = []
    for ci in range(n_i1):
        i1 = j * n_i1 + ci
        gate = jnp.zeros((N_KEYS, tb), jnp.float32)
        for r in range(R_HEADS):
            pair = st_ref[2 * r, pl.ds(i1, 1), :] + st_ref[2 * r + 1]
            w = ea_ref[r, pl.ds(i1, 1), :] * eb_ref[r]
            gate = gate + jnp.where(pair >= thr_ref[r:r + 1, :], w, 0.0)
        coefs.append((gate * _gelu(act[ci * N_KEYS:(ci + 1) * N_KEYS, :])).astype(jnp.bfloat16))
    coef = jnp.concatenate(coefs, axis=0)
    acc_ref[...] += jnp.dot(vt_ref[...], coef, preferred_element_type=jnp.float32)

    @pl.when(j == pl.num_programs(1) - 1)
    def _():
        y_ref[...] = _rms(x1_ref[...] + acc_ref[...].T, g_ref[...])


def _peer(h2, u_bf, vt_bf, st, thr, ea, eb, x1, g_final):
    t, d = h2.shape
    n_exp = u_bf.shape[0]
    tb = min(t, 512)
    eb_rows = 512
    assert t % tb == 0 and n_exp % eb_rows == 0 and n_exp == N_KEYS * N_KEYS
    tok = lambda i, j: (i, 0)
    tok3 = lambda i, j: (0, 0, i)
    return pl.pallas_call(
        functools.partial(_peer_kernel, eb_rows=eb_rows, tb=tb),
        grid=(t // tb, n_exp // eb_rows),
        in_specs=[
            pl.BlockSpec((tb, d), tok),
            pl.BlockSpec((eb_rows, d), lambda i, j: (j, 0)),
            pl.BlockSpec((d, eb_rows), lambda i, j: (0, j)),
            pl.BlockSpec((2 * R_HEADS, N_KEYS, tb), tok3),
            pl.BlockSpec((R_HEADS, tb), lambda i, j: (0, i)),
            pl.BlockSpec((R_HEADS, N_KEYS, tb), tok3),
            pl.BlockSpec((R_HEADS, N_KEYS, tb), tok3),
            pl.BlockSpec((tb, d), tok),
            pl.BlockSpec((1, d), lambda i, j: (0, 0)),
        ],
        out_specs=pl.BlockSpec((tb, d), tok),
        out_shape=jax.ShapeDtypeStruct((t, d), jnp.float32),
        scratch_shapes=[pltpu.VMEM((d, tb), jnp.float32)],
        compiler_params=pltpu.CompilerParams(
            dimension_semantics=("arbitrary", "arbitrary"), vmem_limit_bytes=VMEM_LIMIT),
        name="peer",
    )(h2, u_bf, vt_bf, st, thr, ea, eb, x1, g_final.reshape(1, -1))


def _stream(x, left, attend, p):
    b, s, d = x.shape
    q, k, v, kb, vb, conv, tail = _in_proj(x, p["g_mix"], p["w_in"], left, p["conv_w"], p["conv_b"],
                                           p["ln_g"], p["ln_b"])
    att = attend(q, kb, vb)
    c = conv.shape[-1]
    x1, h2, st = _mid(conv.reshape(b * s, c), att.reshape(b * s, c), x.reshape(b * s, d),
                      p["w_out"], p["g_ffn"], p["w_query"], p["sub_keys"])
    thr, ea, eb = _route(st)
    y = _peer(h2, p["peer_u"], p["peer_vt"], st, thr, ea, eb, x1, p["g_final"])
    k = k.reshape(b, s, N_HEADS, 2, HEAD_DIM)
    v = v.reshape(b, s, N_HEADS, V_DIM)
    return y.reshape(b, s, d), k, v, tail[:, CONV_PAD - (CONV_WIDTH - 1):]


def kernel(x_prompt, x_sample, cache_k, cache_v, state_conv, g_mix, w_in, conv_w, conv_b, conv_ln_g, conv_ln_b, lambda_q1, lambda_k1, lambda_q2, lambda_k2, subln_g, rel_bias, w_out, g_ffn, w_query, sub_keys, peer_u, peer_v, g_final):
    depth = w_in.shape[0]
    assert depth == 1, "single-layer step"
    l = 0
    bf16 = jnp.bfloat16
    b, s, d = x_prompt.shape
    bd, sd, _ = x_sample.shape
    past = cache_k.shape[2]
    c = conv_w.shape[-1]

    lam_init = _lambda_init(l)
    lam = (jnp.exp(jnp.sum(lambda_q1[l].astype(jnp.float32) * lambda_k1[l].astype(jnp.float32)))
           - jnp.exp(jnp.sum(lambda_q2[l].astype(jnp.float32) * lambda_k2[l].astype(jnp.float32)))
           + lam_init).reshape(1)
    out_scale = 1.0 - lam_init

    p = {
        "g_mix": g_mix[l], "w_in": w_in[l].astype(bf16), "conv_w": conv_w[l], "conv_b": conv_b[l],
        "ln_g": conv_ln_g[l], "ln_b": conv_ln_b[l], "w_out": w_out[l].astype(bf16), "g_ffn": g_ffn[l],
        "w_query": w_query[l].astype(bf16),
        "sub_keys": sub_keys[l].reshape(2 * R_HEADS, N_KEYS, -1).astype(bf16),
        "peer_u": peer_u[l].astype(bf16), "peer_vt": peer_v[l].astype(bf16).T, "g_final": g_final,
    }

    attend_p = lambda q, kb, vb: _attn_prompt(q, kb, vb, rel_bias, lam, subln_g[l], out_scale)
    y_p, k_p, v_p, tail_p = _stream(x_prompt, jnp.zeros((b, CONV_PAD, c), jnp.float32), attend_p, p)

    sk = past + sd
    sk_pad = -(-sk // LANES) * LANES
    pos_s = past + jnp.arange(sd, dtype=jnp.int32)
    bias_s = _bias_table(rel_bias, pos_s, jnp.arange(sk, dtype=jnp.int32), True)
    bias_s = jnp.pad(bias_s, ((0, 0), (0, 0), (0, sk_pad - sk)), constant_values=NEG)
    ck = cache_k[l].reshape(bd, past, c).astype(bf16)
    cv = cache_v[l].reshape(bd, past, c).astype(bf16)
    pad = jnp.zeros((bd, sk_pad - sk, c), bf16)

    def attend_s(q, kb, vb):
        keys = jnp.concatenate([ck, kb, pad], axis=1)
        vals = jnp.concatenate([cv, vb, pad], axis=1)
        return _attn_sample(q, keys, vals, bias_s, lam, subln_g[l], out_scale)

    left_s = jnp.pad(state_conv[l], ((0, 0), (CONV_PAD - (CONV_WIDTH - 1), 0), (0, 0)))
    y_s, k_s, v_s, tail_s = _stream(x_sample, left_s, attend_s, p)

    return (y_p, y_s, k_p[None], v_p[None], tail_p[None], k_s[None], v_s[None], tail_s[None])
```

```python
import functools
import math

import jax
import jax.numpy as jnp
from jax import lax
from jax.experimental import pallas as pl
from jax.experimental.pallas import tpu as pltpu

CHUNK = 64
CONV_WIDTH = 31
CONV_PAD = 32
N_HEADS = 4
HEAD_DIM = 64
V_DIM = 2 * HEAD_DIM
N_BUCKETS = 32
MAX_DISTANCE = 128
N_KEYS = 128
R_HEADS = 8
TOPK = 16
EPS = 1e-6
NEG = -1e30
NEG_BIG = -3.0e38
LANES = 128
SUBLANES = 8
PACKED_ROWS = 2 * SUBLANES
MXU_COLS = 256
VMEM_LIMIT = 48 * 1024 * 1024

_NT = (((1,), (1,)), ((), ()))


def _lambda_init(layer):
    return 0.8 - 0.6 * math.exp(-0.3 * layer)


def _rms(xf, g):
    return xf * lax.rsqrt(jnp.mean(xf * xf, axis=-1, keepdims=True) + EPS) * g


def _sigmoid(x):
    return 1.0 / (1.0 + jnp.exp(-x))


def _in_proj_kernel(x_ref, g_ref, w_ref, left_ref, cw_ref, cb_ref, lg_ref, lb_ref,
                    q_ref, k_ref, v_ref, kb_ref, vb_ref, conv_ref, tail_ref, abuf, *, ts, c):
    s = pl.program_id(1)
    h = _rms(x_ref[0], g_ref[...]).astype(jnp.bfloat16)

    @pl.when(s == 0)
    def _():
        abuf[0:CONV_PAD, :] = left_ref[0]

    glu_in = jnp.dot(h, w_ref[:, 0:2 * c], preferred_element_type=jnp.float32)
    abuf[CONV_PAD:CONV_PAD + ts, :] = glu_in[:, :c] * _sigmoid(glu_in[:, c:])
    q = jnp.dot(h, w_ref[:, 2 * c:3 * c], preferred_element_type=jnp.float32)
    q_ref[0] = (q * (HEAD_DIM ** -0.5)).astype(jnp.bfloat16)
    k = jnp.dot(h, w_ref[:, 3 * c:4 * c], preferred_element_type=jnp.float32)
    k_ref[0] = k
    kb_ref[0] = k.astype(jnp.bfloat16)
    v = jnp.dot(h, w_ref[:, 4 * c:5 * c], preferred_element_type=jnp.float32)
    v_ref[0] = v
    vb_ref[0] = v.astype(jnp.bfloat16)

    rc = min(ts, 64)
    for r0 in range(0, ts, rc):
        acc = jnp.zeros((rc, c), jnp.float32)
        for w in range(CONV_WIDTH):
            off = r0 + CONV_PAD - (CONV_WIDTH - 1) + w
            acc = acc + abuf[off:off + rc, :] * cw_ref[w:w + 1, :]
        y = acc + cb_ref[...]
        mu = jnp.mean(y, axis=-1, keepdims=True)
        d = y - mu
        var = jnp.mean(d * d, axis=-1, keepdims=True)
        yn = d * lax.rsqrt(var + EPS) * lg_ref[...] + lb_ref[...]
        conv_ref[0, r0:r0 + rc, :] = (yn * _sigmoid(yn)).astype(jnp.bfloat16)

    tail = abuf[ts:ts + CONV_PAD, :]
    tail_ref[0] = tail
    abuf[0:CONV_PAD, :] = tail


def _in_proj(x, g_mix, w_in_bf, left, conv_w, conv_b, ln_g, ln_b):
    b, s, d = x.shape
    c = conv_w.shape[1]
    ts = min(s, 512)
    assert s % ts == 0 and ts >= CONV_PAD and ts % SUBLANES == 0
    cw = jnp.pad(conv_w, ((0, CONV_PAD - CONV_WIDTH), (0, 0)))
    row = lambda a: a.reshape(1, -1)
    tok = lambda bi, si: (bi, si, 0)
    const2 = lambda bi, si: (0, 0)
    f32, bf16 = jnp.float32, jnp.bfloat16
    outs = pl.pallas_call(
        functools.partial(_in_proj_kernel, ts=ts, c=c),
        grid=(b, s // ts),
        in_specs=[
            pl.BlockSpec((1, ts, d), tok),
            pl.BlockSpec((1, d), const2),
            pl.BlockSpec(w_in_bf.shape, const2),
            pl.BlockSpec((1, CONV_PAD, c), lambda bi, si: (bi, 0, 0)),
            pl.BlockSpec((CONV_PAD, c), const2),
            pl.BlockSpec((1, c), const2),
            pl.BlockSpec((1, c), const2),
            pl.BlockSpec((1, c), const2),
        ],
        out_specs=[pl.BlockSpec((1, ts, c), tok)] * 6
        + [pl.BlockSpec((1, CONV_PAD, c), lambda bi, si: (bi, 0, 0))],
        out_shape=[
            jax.ShapeDtypeStruct((b, s, c), bf16),
            jax.ShapeDtypeStruct((b, s, c), f32),
            jax.ShapeDtypeStruct((b, s, c), f32),
            jax.ShapeDtypeStruct((b, s, c), bf16),
            jax.ShapeDtypeStruct((b, s, c), bf16),
            jax.ShapeDtypeStruct((b, s, c), bf16),
            jax.ShapeDtypeStruct((b, CONV_PAD, c), f32),
        ],
        scratch_shapes=[pltpu.VMEM((ts + CONV_PAD, c), f32)],
        compiler_params=pltpu.CompilerParams(
            dimension_semantics=("arbitrary", "arbitrary"), vmem_limit_bytes=VMEM_LIMIT),
        name="in_proj",
    )(x, row(g_mix), w_in_bf, left, cw, row(conv_b), row(ln_g), row(ln_b))
    return outs


def _rel_bucket(rel):
    nb = N_BUCKETS // 2
    max_exact = nb // 2
    ret = jnp.where(rel > 0, nb, 0)
    n = jnp.abs(rel)
    nf = jnp.maximum(n, 1).astype(jnp.float32)
    large = max_exact + (jnp.log(nf / max_exact) / math.log(MAX_DISTANCE / max_exact)
                         * (nb - max_exact)).astype(jnp.int32)
    large = jnp.minimum(large, nb - 1)
    return ret + jnp.where(n < max_exact, n, large)


def _bias_table(rel_bias, q_pos, k_pos, masked):
    bucket = _rel_bucket(k_pos[None, :] - q_pos[:, None])
    table = rel_bias.astype(jnp.float32).T[:, None, None, :]
    hit = bucket[None, :, :, None] == jnp.arange(N_BUCKETS, dtype=jnp.int32)
    bias = jnp.sum(jnp.where(hit, table, 0.0), axis=-1)
    if masked:
        mask = (k_pos[None, :] // CHUNK) <= (q_pos[:, None] // CHUNK)
        bias = jnp.where(mask[None], bias, NEG)
    return bias


def _split_maps(q):
    lane = lax.broadcasted_iota(jnp.int32, q.shape, 1)
    zero = jnp.zeros_like(q)
    return jnp.where(lane < HEAD_DIM, q, zero), jnp.where(lane >= HEAD_DIM, q, zero)


def _softmax_rows(s):
    p = jnp.exp(s - jnp.max(s, axis=-1, keepdims=True))
    return p / jnp.sum(p, axis=-1, keepdims=True)


def _attn_finish(s1, s2, vv, lam, g, out_scale):
    attn = (_softmax_rows(s1) - lam * _softmax_rows(s2)).astype(jnp.bfloat16)
    o = jnp.dot(attn, vv, preferred_element_type=jnp.float32)
    return (_rms(o, g) * out_scale).astype(jnp.bfloat16)


def _attn_prompt_kernel(lam_ref, q_ref, k_ref, v_ref, slab_ref, g_ref, o_ref, s1_ref, s2_ref,
                        *, tq, n_kb, out_scale):
    qi = pl.program_id(2)
    q1, q2 = _split_maps(q_ref[0])
    for j in range(n_kb):
        kj = k_ref[0, j * tq:(j + 1) * tq, :]
        kind = jnp.clip(j - qi, -2, 1) + 2
        bias = slab_ref[0, kind]
        s1_ref[:, j * tq:(j + 1) * tq] = lax.dot_general(q1, kj, _NT, preferred_element_type=jnp.float32) + bias
        s2_ref[:, j * tq:(j + 1) * tq] = lax.dot_general(q2, kj, _NT, preferred_element_type=jnp.float32) + bias
    o_ref[0] = _attn_finish(s1_ref[...], s2_ref[...], v_ref[0], lam_ref[0], g_ref[...], out_scale)


def _attn_prompt(q, kb, vb, rel_bias, lam, subln_g, out_scale):
    b, s, c = q.shape
    tq = 128
    assert s % tq == 0 and tq % CHUNK == 0 and c == N_HEADS * V_DIM
    n_kb = s // tq
    pos = jnp.arange(tq, dtype=jnp.int32)
    slabs = jnp.stack([
        _bias_table(rel_bias, pos + 2 * tq, pos, False),
        _bias_table(rel_bias, pos + tq, pos, False),
        _bias_table(rel_bias, pos, pos, True),
        jnp.full((N_HEADS, tq, tq), NEG, jnp.float32),
    ], axis=1)
    return pl.pallas_call(
        functools.partial(_attn_prompt_kernel, tq=tq, n_kb=n_kb, out_scale=out_scale),
        grid=(b, N_HEADS, n_kb),
        in_specs=[
            pl.BlockSpec(memory_space=pltpu.SMEM),
            pl.BlockSpec((1, tq, V_DIM), lambda bi, hi, qi: (bi, qi, hi)),
            pl.BlockSpec((1, s, V_DIM), lambda bi, hi, qi: (bi, 0, hi)),
            pl.BlockSpec((1, s, V_DIM), lambda bi, hi, qi: (bi, 0, hi)),
            pl.BlockSpec((1, 4, tq, tq), lambda bi, hi, qi: (hi, 0, 0, 0)),
            pl.BlockSpec((1, V_DIM), lambda bi, hi, qi: (0, 0)),
        ],
        out_specs=pl.BlockSpec((1, tq, V_DIM), lambda bi, hi, qi: (bi, qi, hi)),
        out_shape=jax.ShapeDtypeStruct((b, s, c), jnp.bfloat16),
        scratch_shapes=[pltpu.VMEM((tq, s), jnp.float32), pltpu.VMEM((tq, s), jnp.float32)],
        compiler_params=pltpu.CompilerParams(
            dimension_semantics=("arbitrary",) * 3, vmem_limit_bytes=VMEM_LIMIT),
        name="attn_prompt",
    )(lam, q, kb, vb, slabs, subln_g.reshape(1, -1))


def _attn_sample_kernel(lam_ref, q_ref, k_ref, v_ref, bias_ref, g_ref, o_ref, *, out_scale):
    q1, q2 = _split_maps(q_ref[0])
    kk = k_ref[0]
    bias = bias_ref[0]
    s1 = lax.dot_general(q1, kk, _NT, preferred_element_type=jnp.float32) + bias
    s2 = lax.dot_general(q2, kk, _NT, preferred_element_type=jnp.float32) + bias
    o_ref[0] = _attn_finish(s1, s2, v_ref[0], lam_ref[0], g_ref[...], out_scale)


def _attn_sample(q, keys, vals, bias, lam, subln_g, out_scale):
    b, sq, c = q.shape
    sk = keys.shape[1]
    return pl.pallas_call(
        functools.partial(_attn_sample_kernel, out_scale=out_scale),
        grid=(b, N_HEADS),
        in_specs=[
            pl.BlockSpec(memory_space=pltpu.SMEM),
            pl.BlockSpec((1, sq, V_DIM), lambda bi, hi: (bi, 0, hi)),
            pl.BlockSpec((1, sk, V_DIM), lambda bi, hi: (bi, 0, hi)),
            pl.BlockSpec((1, sk, V_DIM), lambda bi, hi: (bi, 0, hi)),
            pl.BlockSpec((1, sq, sk), lambda bi, hi: (hi, 0, 0)),
            pl.BlockSpec((1, V_DIM), lambda bi, hi: (0, 0)),
        ],
        out_specs=pl.BlockSpec((1, sq, V_DIM), lambda bi, hi: (bi, 0, hi)),
        out_shape=jax.ShapeDtypeStruct((b, sq, c), jnp.bfloat16),
        compiler_params=pltpu.CompilerParams(
            dimension_semantics=("arbitrary",) * 2, vmem_limit_bytes=VMEM_LIMIT),
        name="attn_sample",
    )(lam, q, keys, vals, bias, subln_g.reshape(1, -1))


def _mid_kernel(conv_ref, att_ref, x_ref, wc_ref, wa_ref, g_ref, wq_ref, sk_ref,
                x1_ref, h2_ref, st_ref):
    x1 = (x_ref[...]
          + jnp.dot(conv_ref[...], wc_ref[...], preferred_element_type=jnp.float32)
          + jnp.dot(att_ref[...], wa_ref[...], preferred_element_type=jnp.float32))
    x1_ref[...] = x1
    h2 = _rms(x1, g_ref[...]).astype(jnp.bfloat16)
    h2_ref[...] = h2
    qq = jnp.dot(h2, wq_ref[...], preferred_element_type=jnp.float32).astype(jnp.bfloat16)
    for rp in range(2 * R_HEADS):
        st_ref[rp] = lax.dot_general(sk_ref[rp], qq[:, rp * N_KEYS:(rp + 1) * N_KEYS], _NT,
                                     preferred_element_type=jnp.float32)


def _mid(conv, att, x2d, w_out_bf, g_ffn, w_query_bf, sub_keys_bf):
    t, d = x2d.shape
    c = conv.shape[1]
    tb = min(t, 512)
    assert t % tb == 0
    dq = w_query_bf.shape[1]
    nrp = sub_keys_bf.shape[0]
    tok = lambda i: (i, 0)
    const = lambda i: (0, 0)
    return pl.pallas_call(
        _mid_kernel,
        grid=(t // tb,),
        in_specs=[
            pl.BlockSpec((tb, c), tok),
            pl.BlockSpec((tb, c), tok),
            pl.BlockSpec((tb, d), tok),
            pl.BlockSpec((c, d), const),
            pl.BlockSpec((c, d), lambda i: (1, 0)),
            pl.BlockSpec((1, d), const),
            pl.BlockSpec((d, dq), const),
            pl.BlockSpec(sub_keys_bf.shape, lambda i: (0, 0, 0)),
        ],
        out_specs=[
            pl.BlockSpec((tb, d), tok),
            pl.BlockSpec((tb, d), tok),
            pl.BlockSpec((nrp, N_KEYS, tb), lambda i: (0, 0, i)),
        ],
        out_shape=[
            jax.ShapeDtypeStruct((t, d), jnp.float32),
            jax.ShapeDtypeStruct((t, d), jnp.bfloat16),
            jax.ShapeDtypeStruct((nrp, N_KEYS, t), jnp.float32),
        ],
        compiler_params=pltpu.CompilerParams(
            dimension_semantics=("arbitrary",), vmem_limit_bytes=VMEM_LIMIT),
        name="mid",
    )(conv, att, x2d, w_out_bf, w_out_bf, g_ffn.reshape(1, -1), w_query_bf, sub_keys_bf)


def _ce(a, b):
    if a is None:
        return b, None
    if b is None:
        return a, None
    return jnp.maximum(a, b), jnp.minimum(a, b)


def _sort_desc(xs):
    xs = list(xs)
    n = len(xs)
    p = 1
    while p < n:
        k = p
        while k >= 1:
            for j in range(k % p, n - k, 2 * k):
                for i in range(min(k, n - j - k)):
                    if (i + j) // (2 * p) == (i + j + k) // (2 * p):
                        xs[i + j], xs[i + j + k] = _ce(xs[i + j], xs[i + j + k])
            k //= 2
        p *= 2
    return xs


def _bitonic_top(a, b):
    n = len(a)
    return [_ce(a[i], b[n - 1 - i])[0] for i in range(n)]


def _bitonic_sort_desc(xs):
    xs = list(xs)
    n = len(xs)
    d = n // 2
    while d >= 1:
        for i in range(n):
            if i & d == 0:
                xs[i], xs[i + d] = _ce(xs[i], xs[i + d])
        d //= 2
    return xs


def _fill(xs):
    return [jnp.full((SUBLANES, LANES), NEG_BIG, jnp.float32) if x is None else x for x in xs]


def _sublane_merge_sorted(xs):
    for shift in (4, 2, 1):
        other = [pltpu.roll(x, shift, 0) for x in xs]
        xs = _bitonic_sort_desc(_bitonic_top(xs, other))
    return xs


def _sublane_merge_kth(xs):
    for shift in (4, 2):
        other = [pltpu.roll(x, shift, 0) for x in xs]
        xs = _bitonic_sort_desc(_bitonic_top(xs, other))
    other = [pltpu.roll(x, 1, 0) for x in xs]
    top = _bitonic_top(xs, other)
    return functools.reduce(jnp.minimum, top)


def _top16_rows(s):
    tiles = [s[i * SUBLANES:(i + 1) * SUBLANES, :] for i in range(N_KEYS // SUBLANES)]
    return _sublane_merge_sorted(_sort_desc(tiles))


def _dup_bf16(x):
    b = pltpu.bitcast(x.astype(jnp.bfloat16).astype(jnp.float32), jnp.uint32)
    return b | (b >> 16)


def _route_kernel(st_ref, rank_ref, n_ref, ea_ref, eb_ref, *, tb):
    sub = lax.broadcasted_iota(jnp.int32, (SUBLANES, LANES), 0)
    for g in range(tb // LANES):
        cols = slice(g * LANES, (g + 1) * LANES)
        for r in range(R_HEADS):
            s1 = st_ref[2 * r, :, cols]
            s2 = st_ref[2 * r + 1, :, cols]
            v1 = _top16_rows(s1)
            v2 = _top16_rows(s2)
            pack = lambda v, base: functools.reduce(
                lambda acc, j: jnp.where(sub == j, v[base + j], acc), range(1, SUBLANES), v[base])
            w1a, w1b, w2a, w2b = pack(v1, 0), pack(v1, 8), pack(v2, 0), pack(v2, 8)
            neg = jnp.full((SUBLANES, LANES), NEG_BIG, jnp.float32)
            cands = [
                v1[0] + w2a,
                v1[0] + w2b,
                jnp.where(sub >= 1, v2[0] + w1a, neg),
                v2[0] + w1b,
                jnp.where(sub >= 1, v1[1] + w2a, neg),
                jnp.where(sub >= 2, v2[1] + w1a, neg),
                jnp.where((sub >= 2) & (sub <= 4), v1[2] + w2a, neg),
                jnp.where((sub >= 2) & (sub <= 3), v1[3] + w2a, neg),
                jnp.where(sub == 2, v1[4] + w2a, neg),
            ]
            srt = _fill(_sort_desc(cands + [None] * (TOPK - len(cands))))
            thr = _sublane_merge_kth(srt)
            m1, m2 = v1[0], v2[0]
            top = m1 + m2
            z = functools.reduce(
                lambda acc, cnd: acc + jnp.where(cnd >= thr, jnp.exp(cnd - top), 0.0), cands,
                jnp.zeros((SUBLANES, LANES), jnp.float32))
            for shift in (4, 2, 1):
                z = z + pltpu.roll(z, shift, 0)
            rank2 = jnp.zeros((N_KEYS, LANES), jnp.float32)
            cnt = jnp.zeros((N_KEYS, LANES), jnp.float32)
            thr_row = thr[0:1, :]
            for j in range(TOPK):
                v2j = v2[j][0:1, :]
                rank2 = rank2 + jnp.where(v2j > s2, 1.0, 0.0)
                cnt = cnt + jnp.where(s1 + v2j >= thr_row, 1.0, 0.0)
            n_ref[r, :, cols] = _dup_bf16(cnt)
            ea_ref[r, :, cols] = _dup_bf16(jnp.exp(s1 - m1[0:1, :]) / z[0:1, :])
            eb_ref[r, :, cols] = jnp.exp(s2 - m2[0:1, :]).astype(jnp.bfloat16)
            rank_ref[r, :, cols] = rank2.astype(jnp.bfloat16)


def _route(st):
    nrp, nk, t = st.shape
    tb = min(t, 256)
    assert t % tb == 0 and tb % LANES == 0 and nk == N_KEYS and nrp == 2 * R_HEADS
    row_spec = pl.BlockSpec((R_HEADS, nk, tb), lambda i: (0, 0, i))
    return pl.pallas_call(
        functools.partial(_route_kernel, tb=tb),
        grid=(t // tb,),
        in_specs=[pl.BlockSpec((nrp, nk, tb), lambda i: (0, 0, i))],
        out_specs=[row_spec] * 4,
        out_shape=[
            jax.ShapeDtypeStruct((R_HEADS, nk, t), jnp.bfloat16),
            jax.ShapeDtypeStruct((R_HEADS, nk, t), jnp.uint32),
            jax.ShapeDtypeStruct((R_HEADS, nk, t), jnp.uint32),
            jax.ShapeDtypeStruct((R_HEADS, nk, t), jnp.bfloat16),
        ],
        compiler_params=pltpu.CompilerParams(
            dimension_semantics=("arbitrary",), vmem_limit_bytes=VMEM_LIMIT),
        name="route",
    )(st)


def _gelu(x):
    hx = 0.5 * x
    return hx + hx * lax.erf(x * (2.0 ** -0.5))


def _packed_row(ref, r, row, cols):
    tile = jnp.broadcast_to(ref[r, row:row + 1, cols], (SUBLANES, LANES))
    return pltpu.bitcast(tile, jnp.bfloat16)


def _peer_kernel(h_ref, u_ref, vt_ref, rank_ref, eb_ref, n_odd_ref, n_even_ref, ea_odd_ref, ea_even_ref,
                 x1_ref, g_ref, y_ref, acc_ref, act_a, act_b, coef_a, coef_b, *, eb_rows, tb):
    s = pl.program_id(1)

    @pl.when(s == 0)
    def _():
        for ref in (acc_ref, act_a, act_b, coef_a, coef_b):
            ref[...] = jnp.zeros_like(ref)

    n_tg = tb // LANES
    tn = min(tb, MXU_COLS)

    def accumulate(half, coef_ref, piece):
        vt = vt_ref[:, half * eb_rows:(half + 1) * eb_rows]
        cols = slice(piece * tn, (piece + 1) * tn)
        acc_ref[:, cols] += jnp.dot(vt, coef_ref[:, cols], preferred_element_type=jnp.float32)

    def gate(cnt_ref, gain_ref, act_ref, coef_ref, tg):
        zero = jnp.zeros((PACKED_ROWS, LANES), jnp.bfloat16)
        cols = slice(tg * LANES, (tg + 1) * LANES)
        for ci in range(eb_rows // N_KEYS):
            cnt = [_packed_row(cnt_ref, r, ci, cols) for r in range(R_HEADS)]
            ea = [_packed_row(gain_ref, r, ci, cols) for r in range(R_HEADS)]
            for ch in range(N_KEYS // PACKED_ROWS):
                keys = slice(ch * PACKED_ROWS, (ch + 1) * PACKED_ROWS)
                g = zero
                for r in range(R_HEADS):
                    sel = jnp.minimum(jnp.maximum(cnt[r] - rank_ref[r, keys, cols], 0), 1)
                    g = g + (ea[r] * sel) * eb_ref[r, keys, cols]
                rows = slice(ci * N_KEYS + ch * PACKED_ROWS, ci * N_KEYS + (ch + 1) * PACKED_ROWS)
                coef_ref[rows, cols] = g * _gelu(act_ref[rows, cols]).astype(jnp.bfloat16)

    def activate(half, act_ref, piece):
        u = u_ref[half * eb_rows:(half + 1) * eb_rows, :]
        cols = slice(piece * tn, (piece + 1) * tn)
        act_ref[:, cols] = lax.dot_general(u, h_ref[cols, :], _NT, preferred_element_type=jnp.float32)

    def half_step(half, coef_in, cnt_ref, gain_ref, act_in, coef_out, act_out):
        n_piece = tb // tn
        per = n_tg // n_piece
        for piece in range(n_piece):
            accumulate(half, coef_in, piece)
            for tg in range(piece * per, piece * per + per // 2):
                gate(cnt_ref, gain_ref, act_in, coef_out, tg)
            activate(half, act_out, piece)
            for tg in range(piece * per + per // 2, (piece + 1) * per):
                gate(cnt_ref, gain_ref, act_in, coef_out, tg)

    half_step(0, coef_a, n_odd_ref, ea_odd_ref, act_b, coef_b, act_a)
    half_step(1, coef_b, n_even_ref, ea_even_ref, act_a, coef_a, act_b)

    @pl.when(s == pl.num_programs(1) - 1)
    def _():
        y_ref[...] = _rms(x1_ref[...] + acc_ref[...].T, g_ref[...])


def _peer(h2, u_bf, vt_bf, rank2, cnt, ea, eb, x1, g_final):
    t, d = h2.shape
    n_exp = u_bf.shape[0]
    tb = min(t, 512)
    eb_rows = SUBLANES * N_KEYS
    assert t % tb == 0 and n_exp % (2 * eb_rows) == 0 and n_exp == N_KEYS * N_KEYS
    n_blocks = n_exp // eb_rows
    n_steps = n_blocks // 2 + 1
    clamp = lambda b, hi: jnp.clip(b, 0, hi)
    tok = lambda i, s: (i, 0)
    tile_spec = pl.BlockSpec((R_HEADS, N_KEYS, tb), lambda i, s: (0, 0, i))
    odd_spec = pl.BlockSpec((R_HEADS, SUBLANES, tb), lambda i, s: (0, clamp(2 * s - 1, n_blocks - 1), i))
    even_spec = pl.BlockSpec((R_HEADS, SUBLANES, tb), lambda i, s: (0, clamp(2 * s, n_blocks - 1), i))
    return pl.pallas_call(
        functools.partial(_peer_kernel, eb_rows=eb_rows, tb=tb),
        grid=(t // tb, n_steps),
        in_specs=[
            pl.BlockSpec((tb, d), tok),
            pl.BlockSpec((2 * eb_rows, d), lambda i, s: (clamp(s, n_steps - 2), 0)),
            pl.BlockSpec((d, 2 * eb_rows), lambda i, s: (0, clamp(s - 1, n_steps - 2))),
            tile_spec, tile_spec, odd_spec, even_spec, odd_spec, even_spec,
            pl.BlockSpec((tb, d), tok),
            pl.BlockSpec((1, d), lambda i, s: (0, 0)),
        ],
        out_specs=pl.BlockSpec((tb, d), tok),
        out_shape=jax.ShapeDtypeStruct((t, d), jnp.float32),
        scratch_shapes=[
            pltpu.VMEM((d, tb), jnp.float32),
            pltpu.VMEM((eb_rows, tb), jnp.float32),
            pltpu.VMEM((eb_rows, tb), jnp.float32),
            pltpu.VMEM((eb_rows, tb), jnp.bfloat16),
            pltpu.VMEM((eb_rows, tb), jnp.bfloat16),
        ],
        compiler_params=pltpu.CompilerParams(
            dimension_semantics=("arbitrary", "arbitrary"), vmem_limit_bytes=VMEM_LIMIT),
        name="peer",
    )(h2, u_bf, vt_bf, rank2, eb, cnt, cnt, ea, ea, x1, g_final.reshape(1, -1))


def _stream(x, left, attend, p):
    b, s, d = x.shape
    q, k, v, kb, vb, conv, tail = _in_proj(x, p["g_mix"], p["w_in"], left, p["conv_w"], p["conv_b"],
                                           p["ln_g"], p["ln_b"])
    att = attend(q, kb, vb)
    c = conv.shape[-1]
    x1, h2, st = _mid(conv.reshape(b * s, c), att.reshape(b * s, c), x.reshape(b * s, d),
                      p["w_out"], p["g_ffn"], p["w_query"], p["sub_keys"])
    rank2, cnt, ea, eb = _route(st)
    y = _peer(h2, p["peer_u"], p["peer_vt"], rank2, cnt, ea, eb, x1, p["g_final"])
    k = k.reshape(b, s, N_HEADS, 2, HEAD_DIM)
    v = v.reshape(b, s, N_HEADS, V_DIM)
    return y.reshape(b, s, d), k, v, tail[:, CONV_PAD - (CONV_WIDTH - 1):]


def kernel(x_prompt, x_sample, cache_k, cache_v, state_conv, g_mix, w_in, conv_w, conv_b, conv_ln_g, conv_ln_b, lambda_q1, lambda_k1, lambda_q2, lambda_k2, subln_g, rel_bias, w_out, g_ffn, w_query, sub_keys, peer_u, peer_v, g_final):
    depth = w_in.shape[0]
    assert depth == 1, "single-layer step"
    l = 0
    bf16 = jnp.bfloat16
    b, s, d = x_prompt.shape
    bd, sd, _ = x_sample.shape
    past = cache_k.shape[2]
    c = conv_w.shape[-1]

    lam_init = _lambda_init(l)
    lam = (jnp.exp(jnp.sum(lambda_q1[l].astype(jnp.float32) * lambda_k1[l].astype(jnp.float32)))
           - jnp.exp(jnp.sum(lambda_q2[l].astype(jnp.float32) * lambda_k2[l].astype(jnp.float32)))
           + lam_init).reshape(1)
    out_scale = 1.0 - lam_init

    p = {
        "g_mix": g_mix[l], "w_in": w_in[l].astype(bf16), "conv_w": conv_w[l], "conv_b": conv_b[l],
        "ln_g": conv_ln_g[l], "ln_b": conv_ln_b[l], "w_out": w_out[l].astype(bf16), "g_ffn": g_ffn[l],
        "w_query": w_query[l].astype(bf16),
        "sub_keys": sub_keys[l].reshape(2 * R_HEADS, N_KEYS, -1).astype(bf16),
        "peer_u": peer_u[l].astype(bf16), "peer_vt": peer_v[l].astype(bf16).T, "g_final": g_final,
    }

    attend_p = lambda q, kb, vb: _attn_prompt(q, kb, vb, rel_bias, lam, subln_g[l], out_scale)
    y_p, k_p, v_p, tail_p = _stream(x_prompt, jnp.zeros((b, CONV_PAD, c), jnp.float32), attend_p, p)

    sk = past + sd
    sk_pad = -(-sk // LANES) * LANES
    pos_s = past + jnp.arange(sd, dtype=jnp.int32)
    bias_s = _bias_table(rel_bias, pos_s, jnp.arange(sk, dtype=jnp.int32), True)
    bias_s = jnp.pad(bias_s, ((0, 0), (0, 0), (0, sk_pad - sk)), constant_values=NEG)
    ck = cache_k[l].reshape(bd, past, c).astype(bf16)
    cv = cache_v[l].reshape(bd, past, c).astype(bf16)
    pad = jnp.zeros((bd, sk_pad - sk, c), bf16)

    def attend_s(q, kb, vb):
        keys = jnp.concatenate([ck, kb, pad], axis=1)
        vals = jnp.concatenate([cv, vb, pad], axis=1)
        return _attn_sample(q, keys, vals, bias_s, lam, subln_g[l], out_scale)

    left_s = jnp.pad(state_conv[l], ((0, 0), (CONV_PAD - (CONV_WIDTH - 1), 0), (0, 0)))
    y_s, k_s, v_s, tail_s = _stream(x_sample, left_s, attend_s, p)

    return (y_p, y_s, k_p[None], v_p[None], tail_p[None], k_s[None], v_s[None], tail_s[None])
```

```python
import functools
import math

import jax
import jax.numpy as jnp
from jax import lax
from jax.experimental import pallas as pl
from jax.experimental.pallas import tpu as pltpu

CHUNK = 64
CONV_WIDTH = 31
CONV_PAD = 32
N_HEADS = 4
HEAD_DIM = 64
V_DIM = 2 * HEAD_DIM
N_BUCKETS = 32
MAX_DISTANCE = 128
N_KEYS = 128
R_HEADS = 8
TOPK = 16
EPS = 1e-6
NEG = -1e30
NEG_BIG = -3.0e38
LANES = 128
SUBLANES = 8
PACKED_ROWS = 2 * SUBLANES
MXU_COLS = 256
VMEM_LIMIT = 48 * 1024 * 1024

_NT = (((1,), (1,)), ((), ()))


def _lambda_init(layer):
    return 0.8 - 0.6 * math.exp(-0.3 * layer)


def _rms(xf, g):
    return xf * lax.rsqrt(jnp.mean(xf * xf, axis=-1, keepdims=True) + EPS) * g


def _sigmoid(x):
    return 1.0 / (1.0 + jnp.exp(-x))


def _in_proj_kernel(x_ref, g_ref, w_ref, left_ref, cw_ref, cb_ref, lg_ref, lb_ref,
                    q_ref, k_ref, v_ref, kb_ref, vb_ref, conv_ref, tail_ref, abuf, *, ts, c):
    s = pl.program_id(1)
    h = _rms(x_ref[0], g_ref[...]).astype(jnp.bfloat16)

    @pl.when(s == 0)
    def _():
        abuf[0, 0:CONV_PAD, :] = left_ref[0]

    glu_in = jnp.dot(h, w_ref[:, 0:2 * c], preferred_element_type=jnp.float32)
    abuf[0, CONV_PAD:CONV_PAD + ts, :] = glu_in[:, :c] * _sigmoid(glu_in[:, c:])
    q = jnp.dot(h, w_ref[:, 2 * c:3 * c], preferred_element_type=jnp.float32)
    q_ref[0] = (q * (HEAD_DIM ** -0.5)).astype(jnp.bfloat16)
    k = jnp.dot(h, w_ref[:, 3 * c:4 * c], preferred_element_type=jnp.float32)
    k_ref[0] = k
    kb_ref[0] = k.astype(jnp.bfloat16)
    v = jnp.dot(h, w_ref[:, 4 * c:5 * c], preferred_element_type=jnp.float32)
    v_ref[0] = v
    vb_ref[0] = v.astype(jnp.bfloat16)

    n_sh = ts + CONV_PAD - SUBLANES
    for sh in range(1, SUBLANES):
        abuf[sh, 0:n_sh, :] = abuf[0, sh:sh + n_sh, :]
    rc = min(ts, 64)
    for r0 in range(0, ts, rc):
        acc = jnp.zeros((rc, c), jnp.float32)
        for w in range(CONV_WIDTH):
            off = r0 + CONV_PAD - (CONV_WIDTH - 1) + w
            sh = off % SUBLANES
            acc = acc + abuf[sh, off - sh:off - sh + rc, :] * cw_ref[w:w + 1, :]
        y = acc + cb_ref[...]
        mu = jnp.mean(y, axis=-1, keepdims=True)
        d = y - mu
        var = jnp.mean(d * d, axis=-1, keepdims=True)
        yn = d * lax.rsqrt(var + EPS) * lg_ref[...] + lb_ref[...]
        conv_ref[0, r0:r0 + rc, :] = (yn * _sigmoid(yn)).astype(jnp.bfloat16)

    tail = abuf[0, ts:ts + CONV_PAD, :]
    tail_ref[0] = tail
    abuf[0, 0:CONV_PAD, :] = tail


def _in_proj(x, g_mix, w_in_bf, left, conv_w, conv_b, ln_g, ln_b):
    b, s, d = x.shape
    c = conv_w.shape[1]
    ts = min(s, 512)
    assert s % ts == 0 and ts >= CONV_PAD and ts % SUBLANES == 0
    cw = jnp.pad(conv_w, ((0, CONV_PAD - CONV_WIDTH), (0, 0)))
    row = lambda a: a.reshape(1, -1)
    tok = lambda bi, si: (bi, si, 0)
    const2 = lambda bi, si: (0, 0)
    f32, bf16 = jnp.float32, jnp.bfloat16
    outs = pl.pallas_call(
        functools.partial(_in_proj_kernel, ts=ts, c=c),
        grid=(b, s // ts),
        in_specs=[
            pl.BlockSpec((1, ts, d), tok),
            pl.BlockSpec((1, d), const2),
            pl.BlockSpec(w_in_bf.shape, const2),
            pl.BlockSpec((1, CONV_PAD, c), lambda bi, si: (bi, 0, 0)),
            pl.BlockSpec((CONV_PAD, c), const2),
            pl.BlockSpec((1, c), const2),
            pl.BlockSpec((1, c), const2),
            pl.BlockSpec((1, c), const2),
        ],
        out_specs=[pl.BlockSpec((1, ts, c), tok)] * 6
        + [pl.BlockSpec((1, CONV_PAD, c), lambda bi, si: (bi, 0, 0))],
        out_shape=[
            jax.ShapeDtypeStruct((b, s, c), bf16),
            jax.ShapeDtypeStruct((b, s, c), f32),
            jax.ShapeDtypeStruct((b, s, c), f32),
            jax.ShapeDtypeStruct((b, s, c), bf16),
            jax.ShapeDtypeStruct((b, s, c), bf16),
            jax.ShapeDtypeStruct((b, s, c), bf16),
            jax.ShapeDtypeStruct((b, CONV_PAD, c), f32),
        ],
        scratch_shapes=[pltpu.VMEM((SUBLANES, ts + CONV_PAD, c), f32)],
        compiler_params=pltpu.CompilerParams(
            dimension_semantics=("arbitrary", "arbitrary"), vmem_limit_bytes=VMEM_LIMIT),
        name="in_proj",
    )(x, row(g_mix), w_in_bf, left, cw, row(conv_b), row(ln_g), row(ln_b))
    return outs


def _rel_bucket(rel):
    nb = N_BUCKETS // 2
    max_exact = nb // 2
    ret = jnp.where(rel > 0, nb, 0)
    n = jnp.abs(rel)
    nf = jnp.maximum(n, 1).astype(jnp.float32)
    large = max_exact + (jnp.log(nf / max_exact) / math.log(MAX_DISTANCE / max_exact)
                         * (nb - max_exact)).astype(jnp.int32)
    large = jnp.minimum(large, nb - 1)
    return ret + jnp.where(n < max_exact, n, large)


def _bias_table(rel_bias, q_pos, k_pos, masked):
    bucket = _rel_bucket(k_pos[None, :] - q_pos[:, None])
    table = rel_bias.astype(jnp.float32).T[:, None, None, :]
    hit = bucket[None, :, :, None] == jnp.arange(N_BUCKETS, dtype=jnp.int32)
    bias = jnp.sum(jnp.where(hit, table, 0.0), axis=-1)
    if masked:
        mask = (k_pos[None, :] // CHUNK) <= (q_pos[:, None] // CHUNK)
        bias = jnp.where(mask[None], bias, NEG)
    return bias


def _split_maps(q):
    lane = lax.broadcasted_iota(jnp.int32, q.shape, 1)
    zero = jnp.zeros_like(q)
    return jnp.where(lane < HEAD_DIM, q, zero), jnp.where(lane >= HEAD_DIM, q, zero)


def _softmax_rows(s):
    p = jnp.exp(s - jnp.max(s, axis=-1, keepdims=True))
    return p / jnp.sum(p, axis=-1, keepdims=True)


def _attn_finish(s1, s2, vv, lam, g, out_scale):
    attn = (_softmax_rows(s1) - lam * _softmax_rows(s2)).astype(jnp.bfloat16)
    o = jnp.dot(attn, vv, preferred_element_type=jnp.float32)
    return (_rms(o, g) * out_scale).astype(jnp.bfloat16)


def _attn_prompt_kernel(lam_ref, q_ref, k_ref, v_ref, slab_ref, g_ref, o_ref,
                        s_ref, m_ref, l_ref, acc_ref, *, tq, out_scale):
    qi = pl.program_id(2)
    q1, q2 = _split_maps(q_ref[0])
    half = tq // 2
    fold = lambda a, op: op(a[:, :half], a[:, half:]) if half % LANES == 0 else a

    m_ref[...] = jnp.full(m_ref.shape, NEG_BIG, jnp.float32)
    l_ref[...] = jnp.zeros_like(l_ref)
    acc_ref[...] = jnp.zeros_like(acc_ref)

    def logits(j, carry):
        kj = k_ref[0, pl.ds(pl.multiple_of(j * tq, tq), tq), :]
        bias = slab_ref[0, jnp.clip(j - qi, -2, 0) + 2]
        for mp, qm in enumerate((q1, q2)):
            a = lax.dot_general(qm, kj, _NT, preferred_element_type=jnp.float32) + bias
            s_ref[mp, j] = a
            m_ref[mp] = jnp.maximum(m_ref[mp], fold(a, jnp.maximum))
        return carry

    lax.fori_loop(0, qi + 1, logits, 0)
    row_max = [jnp.broadcast_to(jnp.max(m_ref[mp], axis=-1, keepdims=True), (tq, tq)) for mp in range(2)]

    def accumulate(j, carry):
        vj = v_ref[0, pl.ds(pl.multiple_of(j * tq, tq), tq), :]
        for mp in range(2):
            p = jnp.exp(s_ref[mp, j] - row_max[mp])
            l_ref[mp] += fold(p, jnp.add)
            acc_ref[mp] += jnp.dot(p.astype(jnp.bfloat16), vj, preferred_element_type=jnp.float32)
        return carry

    lax.fori_loop(0, qi + 1, accumulate, 0)
    inv = [1.0 / jnp.sum(l_ref[mp], axis=-1, keepdims=True) for mp in range(2)]
    o = acc_ref[0] * inv[0] - lam_ref[0] * (acc_ref[1] * inv[1])
    o_ref[0] = (_rms(o, g_ref[...]) * out_scale).astype(jnp.bfloat16)


def _attn_prompt(q, kb, vb, rel_bias, lam, subln_g, out_scale):
    b, s, c = q.shape
    tq = min(s, 512)
    assert s % tq == 0 and tq % CHUNK == 0 and tq >= MAX_DISTANCE and c == N_HEADS * V_DIM
    n_kb = s // tq
    pos = jnp.arange(tq, dtype=jnp.int32)
    slabs = jnp.stack([
        _bias_table(rel_bias, pos + 2 * tq, pos, False),
        _bias_table(rel_bias, pos + tq, pos, False),
        _bias_table(rel_bias, pos, pos, True),
    ], axis=1)
    width = tq // 2 if (tq // 2) % LANES == 0 else tq
    return pl.pallas_call(
        functools.partial(_attn_prompt_kernel, tq=tq, out_scale=out_scale),
        grid=(b, N_HEADS, n_kb),
        in_specs=[
            pl.BlockSpec(memory_space=pltpu.SMEM),
            pl.BlockSpec((1, tq, V_DIM), lambda bi, hi, qi: (bi, qi, hi)),
            pl.BlockSpec((1, s, V_DIM), lambda bi, hi, qi: (bi, 0, hi)),
            pl.BlockSpec((1, s, V_DIM), lambda bi, hi, qi: (bi, 0, hi)),
            pl.BlockSpec((1, 3, tq, tq), lambda bi, hi, qi: (hi, 0, 0, 0)),
            pl.BlockSpec((1, V_DIM), lambda bi, hi, qi: (0, 0)),
        ],
        out_specs=pl.BlockSpec((1, tq, V_DIM), lambda bi, hi, qi: (bi, qi, hi)),
        out_shape=jax.ShapeDtypeStruct((b, s, c), jnp.bfloat16),
        scratch_shapes=[
            pltpu.VMEM((2, n_kb, tq, tq), jnp.float32),
            pltpu.VMEM((2, tq, width), jnp.float32),
            pltpu.VMEM((2, tq, width), jnp.float32),
            pltpu.VMEM((2, tq, V_DIM), jnp.float32),
        ],
        compiler_params=pltpu.CompilerParams(
            dimension_semantics=("arbitrary",) * 3, vmem_limit_bytes=VMEM_LIMIT),
        name="attn_prompt",
    )(lam, q, kb, vb, slabs, subln_g.reshape(1, -1))


def _attn_sample_kernel(lam_ref, q_ref, k_ref, v_ref, bias_ref, g_ref, o_ref, *, out_scale):
    q1, q2 = _split_maps(q_ref[0])
    kk = k_ref[0]
    bias = bias_ref[0]
    s1 = lax.dot_general(q1, kk, _NT, preferred_element_type=jnp.float32) + bias
    s2 = lax.dot_general(q2, kk, _NT, preferred_element_type=jnp.float32) + bias
    o_ref[0] = _attn_finish(s1, s2, v_ref[0], lam_ref[0], g_ref[...], out_scale)


def _attn_sample(q, keys, vals, bias, lam, subln_g, out_scale):
    b, sq, c = q.shape
    sk = keys.shape[1]
    return pl.pallas_call(
        functools.partial(_attn_sample_kernel, out_scale=out_scale),
        grid=(b, N_HEADS),
        in_specs=[
            pl.BlockSpec(memory_space=pltpu.SMEM),
            pl.BlockSpec((1, sq, V_DIM), lambda bi, hi: (bi, 0, hi)),
            pl.BlockSpec((1, sk, V_DIM), lambda bi, hi: (bi, 0, hi)),
            pl.BlockSpec((1, sk, V_DIM), lambda bi, hi: (bi, 0, hi)),
            pl.BlockSpec((1, sq, sk), lambda bi, hi: (hi, 0, 0)),
            pl.BlockSpec((1, V_DIM), lambda bi, hi: (0, 0)),
        ],
        out_specs=pl.BlockSpec((1, sq, V_DIM), lambda bi, hi: (bi, 0, hi)),
        out_shape=jax.ShapeDtypeStruct((b, sq, c), jnp.bfloat16),
        compiler_params=pltpu.CompilerParams(
            dimension_semantics=("arbitrary",) * 2, vmem_limit_bytes=VMEM_LIMIT),
        name="attn_sample",
    )(lam, q, keys, vals, bias, subln_g.reshape(1, -1))


def _mid_kernel(conv_ref, att_ref, x_ref, wc_ref, wa_ref, g_ref, wq_ref, sk_ref,
                x1_ref, h2_ref, st_ref):
    x1 = (x_ref[...]
          + jnp.dot(conv_ref[...], wc_ref[...], preferred_element_type=jnp.float32)
          + jnp.dot(att_ref[...], wa_ref[...], preferred_element_type=jnp.float32))
    x1_ref[...] = x1
    h2 = _rms(x1, g_ref[...]).astype(jnp.bfloat16)
    h2_ref[...] = h2
    qq = jnp.dot(h2, wq_ref[...], preferred_element_type=jnp.float32).astype(jnp.bfloat16)
    for rp in range(2 * R_HEADS):
        st_ref[rp] = lax.dot_general(sk_ref[rp], qq[:, rp * N_KEYS:(rp + 1) * N_KEYS], _NT,
                                     preferred_element_type=jnp.float32)


def _mid(conv, att, x2d, w_out_bf, g_ffn, w_query_bf, sub_keys_bf):
    t, d = x2d.shape
    c = conv.shape[1]
    tb = min(t, 512)
    assert t % tb == 0
    dq = w_query_bf.shape[1]
    nrp = sub_keys_bf.shape[0]
    tok = lambda i: (i, 0)
    const = lambda i: (0, 0)
    return pl.pallas_call(
        _mid_kernel,
        grid=(t // tb,),
        in_specs=[
            pl.BlockSpec((tb, c), tok),
            pl.BlockSpec((tb, c), tok),
            pl.BlockSpec((tb, d), tok),
            pl.BlockSpec((c, d), const),
            pl.BlockSpec((c, d), lambda i: (1, 0)),
            pl.BlockSpec((1, d), const),
            pl.BlockSpec((d, dq), const),
            pl.BlockSpec(sub_keys_bf.shape, lambda i: (0, 0, 0)),
        ],
        out_specs=[
            pl.BlockSpec((tb, d), tok),
            pl.BlockSpec((tb, d), tok),
            pl.BlockSpec((nrp, N_KEYS, tb), lambda i: (0, 0, i)),
        ],
        out_shape=[
            jax.ShapeDtypeStruct((t, d), jnp.float32),
            jax.ShapeDtypeStruct((t, d), jnp.bfloat16),
            jax.ShapeDtypeStruct((nrp, N_KEYS, t), jnp.float32),
        ],
        compiler_params=pltpu.CompilerParams(
            dimension_semantics=("arbitrary",), vmem_limit_bytes=VMEM_LIMIT),
        name="mid",
    )(conv, att, x2d, w_out_bf, w_out_bf, g_ffn.reshape(1, -1), w_query_bf, sub_keys_bf)


def _ce(a, b):
    if a is None:
        return b, None
    if b is None:
        return a, None
    return jnp.maximum(a, b), jnp.minimum(a, b)


def _sort_desc(xs):
    xs = list(xs)
    n = len(xs)
    p = 1
    while p < n:
        k = p
        while k >= 1:
            for j in range(k % p, n - k, 2 * k):
                for i in range(min(k, n - j - k)):
                    if (i + j) // (2 * p) == (i + j + k) // (2 * p):
                        xs[i + j], xs[i + j + k] = _ce(xs[i + j], xs[i + j + k])
            k //= 2
        p *= 2
    return xs


def _bitonic_top(a, b):
    n = len(a)
    return [_ce(a[i], b[n - 1 - i])[0] for i in range(n)]


def _bitonic_sort_desc(xs):
    xs = list(xs)
    n = len(xs)
    d = n // 2
    while d >= 1:
        for i in range(n):
            if i & d == 0:
                xs[i], xs[i + d] = _ce(xs[i], xs[i + d])
        d //= 2
    return xs


def _fill(xs):
    return [jnp.full((SUBLANES, LANES), NEG_BIG, jnp.float32) if x is None else x for x in xs]


def _sublane_merge_sorted(xs):
    for shift in (4, 2, 1):
        other = [pltpu.roll(x, shift, 0) for x in xs]
        xs = _bitonic_sort_desc(_bitonic_top(xs, other))
    return xs


def _sublane_merge_kth(xs):
    for shift in (4, 2):
        other = [pltpu.roll(x, shift, 0) for x in xs]
        xs = _bitonic_sort_desc(_bitonic_top(xs, other))
    other = [pltpu.roll(x, 1, 0) for x in xs]
    top = _bitonic_top(xs, other)
    return functools.reduce(jnp.minimum, top)


def _top16_rows(s):
    tiles = [s[i * SUBLANES:(i + 1) * SUBLANES, :] for i in range(N_KEYS // SUBLANES)]
    return _sublane_merge_sorted(_sort_desc(tiles))


def _dup_bf16(x):
    b = pltpu.bitcast(x.astype(jnp.bfloat16).astype(jnp.float32), jnp.uint32)
    return b | (b >> 16)


def _route_kernel(st_ref, rank_ref, n_ref, ea_ref, eb_ref, *, tb):
    sub = lax.broadcasted_iota(jnp.int32, (SUBLANES, LANES), 0)
    for g in range(tb // LANES):
        cols = slice(g * LANES, (g + 1) * LANES)
        for r in range(R_HEADS):
            s1 = st_ref[2 * r, :, cols]
            s2 = st_ref[2 * r + 1, :, cols]
            v1 = _top16_rows(s1)
            v2 = _top16_rows(s2)
            pack = lambda v, base: functools.reduce(
                lambda acc, j: jnp.where(sub == j, v[base + j], acc), range(1, SUBLANES), v[base])
            w1a, w1b, w2a, w2b = pack(v1, 0), pack(v1, 8), pack(v2, 0), pack(v2, 8)
            neg = jnp.full((SUBLANES, LANES), NEG_BIG, jnp.float32)
            cands = [
                v1[0] + w2a,
                v1[0] + w2b,
                jnp.where(sub >= 1, v2[0] + w1a, neg),
                v2[0] + w1b,
                jnp.where(sub >= 1, v1[1] + w2a, neg),
                jnp.where(sub >= 2, v2[1] + w1a, neg),
                jnp.where((sub >= 2) & (sub <= 4), v1[2] + w2a, neg),
                jnp.where((sub >= 2) & (sub <= 3), v1[3] + w2a, neg),
                jnp.where(sub == 2, v1[4] + w2a, neg),
            ]
            srt = _fill(_sort_desc(cands + [None] * (TOPK - len(cands))))
            thr = _sublane_merge_kth(srt)
            m1, m2 = v1[0], v2[0]
            top = m1 + m2
            z = functools.reduce(
                lambda acc, cnd: acc + jnp.where(cnd >= thr, jnp.exp(cnd - top), 0.0), cands,
                jnp.zeros((SUBLANES, LANES), jnp.float32))
            for shift in (4, 2, 1):
                z = z + pltpu.roll(z, shift, 0)
            rank2 = jnp.zeros((N_KEYS, LANES), jnp.float32)
            cnt = jnp.zeros((N_KEYS, LANES), jnp.float32)
            thr_row = thr[0:1, :]
            for j in range(TOPK):
                v2j = v2[j][0:1, :]
                rank2 = jnp.where(v2j > s2, j + 1.0, rank2)
                cnt = jnp.where(s1 + v2j >= thr_row, j + 1.0, cnt)
            n_ref[r, :, cols] = _dup_bf16(cnt)
            ea_ref[r, :, cols] = _dup_bf16(jnp.exp(s1 - m1[0:1, :]) / z[0:1, :])
            eb_ref[r, :, cols] = jnp.exp(s2 - m2[0:1, :]).astype(jnp.bfloat16)
            rank_ref[r, :, cols] = rank2.astype(jnp.bfloat16)


def _route(st):
    nrp, nk, t = st.shape
    tb = min(t, 256)
    assert t % tb == 0 and tb % LANES == 0 and nk == N_KEYS and nrp == 2 * R_HEADS
    row_spec = pl.BlockSpec((R_HEADS, nk, tb), lambda i: (0, 0, i))
    return pl.pallas_call(
        functools.partial(_route_kernel, tb=tb),
        grid=(t // tb,),
        in_specs=[pl.BlockSpec((nrp, nk, tb), lambda i: (0, 0, i))],
        out_specs=[row_spec] * 4,
        out_shape=[
            jax.ShapeDtypeStruct((R_HEADS, nk, t), jnp.bfloat16),
            jax.ShapeDtypeStruct((R_HEADS, nk, t), jnp.uint32),
            jax.ShapeDtypeStruct((R_HEADS, nk, t), jnp.uint32),
            jax.ShapeDtypeStruct((R_HEADS, nk, t), jnp.bfloat16),
        ],
        compiler_params=pltpu.CompilerParams(
            dimension_semantics=("arbitrary",), vmem_limit_bytes=VMEM_LIMIT),
        name="route",
    )(st)


def _gelu(x):
    hx = 0.5 * x
    return hx + hx * lax.erf(x * (2.0 ** -0.5))


def _packed_row(ref, r, row, cols):
    tile = jnp.broadcast_to(ref[r, row:row + 1, cols], (SUBLANES, LANES))
    return pltpu.bitcast(tile, jnp.bfloat16)


def _peer_kernel(h_ref, u_ref, vt_ref, rank_ref, eb_ref, n_odd_ref, n_even_ref, ea_odd_ref, ea_even_ref,
                 x1_ref, g_ref, y_ref, acc_ref, act_a, act_b, coef_a, coef_b, *, eb_rows, tb):
    s = pl.program_id(1)

    @pl.when(s == 0)
    def _():
        for ref in (acc_ref, act_a, act_b, coef_a, coef_b):
            ref[...] = jnp.zeros_like(ref)

    n_tg = tb // LANES
    tn = min(tb, MXU_COLS)

    def accumulate(half, coef_ref, piece):
        vt = vt_ref[:, half * eb_rows:(half + 1) * eb_rows]
        cols = slice(piece * tn, (piece + 1) * tn)
        acc_ref[:, cols] += jnp.dot(vt, coef_ref[:, cols], preferred_element_type=jnp.float32)

    def gate(cnt_ref, gain_ref, act_ref, coef_ref, tg):
        zero = jnp.zeros((PACKED_ROWS, LANES), jnp.bfloat16)
        cols = slice(tg * LANES, (tg + 1) * LANES)
        for ci in range(eb_rows // N_KEYS):
            cnt = [_packed_row(cnt_ref, r, ci, cols) for r in range(R_HEADS)]
            ea = [_packed_row(gain_ref, r, ci, cols) for r in range(R_HEADS)]
            for ch in range(N_KEYS // PACKED_ROWS):
                keys = slice(ch * PACKED_ROWS, (ch + 1) * PACKED_ROWS)
                g = zero
                for r in range(R_HEADS):
                    sel = jnp.minimum(jnp.maximum(cnt[r] - rank_ref[r, keys, cols], 0), 1)
                    g = g + (ea[r] * sel) * eb_ref[r, keys, cols]
                rows = slice(ci * N_KEYS + ch * PACKED_ROWS, ci * N_KEYS + (ch + 1) * PACKED_ROWS)
                coef_ref[rows, cols] = g * _gelu(act_ref[rows, cols]).astype(jnp.bfloat16)

    def activate(half, act_ref, piece):
        u = u_ref[half * eb_rows:(half + 1) * eb_rows, :]
        cols = slice(piece * tn, (piece + 1) * tn)
        act_ref[:, cols] = lax.dot_general(u, h_ref[cols, :], _NT, preferred_element_type=jnp.float32)

    def half_step(half, coef_in, cnt_ref, gain_ref, act_in, coef_out, act_out):
        n_piece = tb // tn
        per = n_tg // n_piece
        for piece in range(n_piece):
            accumulate(half, coef_in, piece)
            for tg in range(piece * per, piece * per + per // 2):
                gate(cnt_ref, gain_ref, act_in, coef_out, tg)
            activate(half, act_out, piece)
            for tg in range(piece * per + per // 2, (piece + 1) * per):
                gate(cnt_ref, gain_ref, act_in, coef_out, tg)

    half_step(0, coef_a, n_odd_ref, ea_odd_ref, act_b, coef_b, act_a)
    half_step(1, coef_b, n_even_ref, ea_even_ref, act_a, coef_a, act_b)

    @pl.when(s == pl.num_programs(1) - 1)
    def _():
        y_ref[...] = _rms(x1_ref[...] + acc_ref[...].T, g_ref[...])


def _peer(h2, u_bf, vt_bf, rank2, cnt, ea, eb, x1, g_final):
    t, d = h2.shape
    n_exp = u_bf.shape[0]
    tb = min(t, 512)
    eb_rows = SUBLANES * N_KEYS
    assert t % tb == 0 and n_exp % (2 * eb_rows) == 0 and n_exp == N_KEYS * N_KEYS
    n_blocks = n_exp // eb_rows
    n_steps = n_blocks // 2 + 1
    clamp = lambda b, hi: jnp.clip(b, 0, hi)
    tok = lambda i, s: (i, 0)
    tile_spec = pl.BlockSpec((R_HEADS, N_KEYS, tb), lambda i, s: (0, 0, i))
    odd_spec = pl.BlockSpec((R_HEADS, SUBLANES, tb), lambda i, s: (0, clamp(2 * s - 1, n_blocks - 1), i))
    even_spec = pl.BlockSpec((R_HEADS, SUBLANES, tb), lambda i, s: (0, clamp(2 * s, n_blocks - 1), i))
    return pl.pallas_call(
        functools.partial(_peer_kernel, eb_rows=eb_rows, tb=tb),
        grid=(t // tb, n_steps),
        in_specs=[
            pl.BlockSpec((tb, d), tok),
            pl.BlockSpec((2 * eb_rows, d), lambda i, s: (clamp(s, n_steps - 2), 0)),
            pl.BlockSpec((d, 2 * eb_rows), lambda i, s: (0, clamp(s - 1, n_steps - 2))),
            tile_spec, tile_spec, odd_spec, even_spec, odd_spec, even_spec,
            pl.BlockSpec((tb, d), tok),
            pl.BlockSpec((1, d), lambda i, s: (0, 0)),
        ],
        out_specs=pl.BlockSpec((tb, d), tok),
        out_shape=jax.ShapeDtypeStruct((t, d), jnp.float32),
        scratch_shapes=[
            pltpu.VMEM((d, tb), jnp.float32),
            pltpu.VMEM((eb_rows, tb), jnp.float32),
            pltpu.VMEM((eb_rows, tb), jnp.float32),
            pltpu.VMEM((eb_rows, tb), jnp.bfloat16),
            pltpu.VMEM((eb_rows, tb), jnp.bfloat16),
        ],
        compiler_params=pltpu.CompilerParams(
            dimension_semantics=("arbitrary", "arbitrary"), vmem_limit_bytes=VMEM_LIMIT),
        name="peer",
    )(h2, u_bf, vt_bf, rank2, eb, cnt, cnt, ea, ea, x1, g_final.reshape(1, -1))


def _stream(x, left, attend, p):
    b, s, d = x.shape
    q, k, v, kb, vb, conv, tail = _in_proj(x, p["g_mix"], p["w_in"], left, p["conv_w"], p["conv_b"],
                                           p["ln_g"], p["ln_b"])
    att = attend(q, kb, vb)
    c = conv.shape[-1]
    x1, h2, st = _mid(conv.reshape(b * s, c), att.reshape(b * s, c), x.reshape(b * s, d),
                      p["w_out"], p["g_ffn"], p["w_query"], p["sub_keys"])
    rank2, cnt, ea, eb = _route(st)
    y = _peer(h2, p["peer_u"], p["peer_vt"], rank2, cnt, ea, eb, x1, p["g_final"])
    k = k.reshape(b, s, N_HEADS, 2, HEAD_DIM)
    v = v.reshape(b, s, N_HEADS, V_DIM)
    return y.reshape(b, s, d), k, v, tail[:, CONV_PAD - (CONV_WIDTH - 1):]


def kernel(x_prompt, x_sample, cache_k, cache_v, state_conv, g_mix, w_in, conv_w, conv_b, conv_ln_g, conv_ln_b, lambda_q1, lambda_k1, lambda_q2, lambda_k2, subln_g, rel_bias, w_out, g_ffn, w_query, sub_keys, peer_u, peer_v, g_final):
    depth = w_in.shape[0]
    assert depth == 1, "single-layer step"
    l = 0
    bf16 = jnp.bfloat16
    b, s, d = x_prompt.shape
    bd, sd, _ = x_sample.shape
    past = cache_k.shape[2]
    c = conv_w.shape[-1]

    lam_init = _lambda_init(l)
    lam = (jnp.exp(jnp.sum(lambda_q1[l].astype(jnp.float32) * lambda_k1[l].astype(jnp.float32)))
           - jnp.exp(jnp.sum(lambda_q2[l].astype(jnp.float32) * lambda_k2[l].astype(jnp.float32)))
           + lam_init).reshape(1)
    out_scale = 1.0 - lam_init

    p = {
        "g_mix": g_mix[l], "w_in": w_in[l].astype(bf16), "conv_w": conv_w[l], "conv_b": conv_b[l],
        "ln_g": conv_ln_g[l], "ln_b": conv_ln_b[l], "w_out": w_out[l].astype(bf16), "g_ffn": g_ffn[l],
        "w_query": w_query[l].astype(bf16),
        "sub_keys": sub_keys[l].reshape(2 * R_HEADS, N_KEYS, -1).astype(bf16),
        "peer_u": peer_u[l].astype(bf16), "peer_vt": peer_v[l].astype(bf16).T, "g_final": g_final,
    }

    attend_p = lambda q, kb, vb: _attn_prompt(q, kb, vb, rel_bias, lam, subln_g[l], out_scale)
    y_p, k_p, v_p, tail_p = _stream(x_prompt, jnp.zeros((b, CONV_PAD, c), jnp.float32), attend_p, p)

    sk = past + sd
    sk_pad = -(-sk // LANES) * LANES
    pos_s = past + jnp.arange(sd, dtype=jnp.int32)
    bias_s = _bias_table(rel_bias, pos_s, jnp.arange(sk, dtype=jnp.int32), True)
    bias_s = jnp.pad(bias_s, ((0, 0), (0, 0), (0, sk_pad - sk)), constant_values=NEG)
    ck = cache_k[l].reshape(bd, past, c).astype(bf16)
    cv = cache_v[l].reshape(bd, past, c).astype(bf16)
    pad = jnp.zeros((bd, sk_pad - sk, c), bf16)

    def attend_s(q, kb, vb):
        keys = jnp.concatenate([ck, kb, pad], axis=1)
        vals = jnp.concatenate([cv, vb, pad], axis=1)
        return _attn_sample(q, keys, vals, bias_s, lam, subln_g[l], out_scale)

    left_s = jnp.pad(state_conv[l], ((0, 0), (CONV_PAD - (CONV_WIDTH - 1), 0), (0, 0)))
    y_s, k_s, v_s, tail_s = _stream(x_sample, left_s, attend_s, p)

    return (y_p, y_s, k_p[None], v_p[None], tail_p[None], k_s[None], v_s[None], tail_s[None])
```

```python
import functools
import math

import jax
import jax.numpy as jnp
from jax import lax
from jax.experimental import pallas as pl
from jax.experimental.pallas import tpu as pltpu

CHUNK = 64
CONV_WIDTH = 31
CONV_PAD = 32
N_HEADS = 4
HEAD_DIM = 64
V_DIM = 2 * HEAD_DIM
N_BUCKETS = 32
MAX_DISTANCE = 128
N_KEYS = 128
R_HEADS = 8
TOPK = 16
EPS = 1e-6
NEG = -1e30
NEG_BIG = -3.0e38
LANES = 128
SUBLANES = 8
PACKED_ROWS = 2 * SUBLANES
MXU_COLS = 256
VMEM_LIMIT = 48 * 1024 * 1024

_NT = (((1,), (1,)), ((), ()))


def _lambda_init(layer):
    return 0.8 - 0.6 * math.exp(-0.3 * layer)


def _rms(xf, g):
    return xf * lax.rsqrt(jnp.mean(xf * xf, axis=-1, keepdims=True) + EPS) * g


def _sigmoid(x):
    return 1.0 / (1.0 + jnp.exp(-x))


def _in_proj_kernel(x_ref, g_ref, w_ref, left_ref, cw_ref, cb_ref, lg_ref, lb_ref,
                    q_ref, k_ref, v_ref, kb_ref, vb_ref, conv_ref, tail_ref, abuf, *, ts, c):
    s = pl.program_id(1)
    h = _rms(x_ref[0], g_ref[...]).astype(jnp.bfloat16)

    @pl.when(s == 0)
    def _():
        abuf[0, 0:CONV_PAD, :] = left_ref[0]

    glu_in = jnp.dot(h, w_ref[:, 0:2 * c], preferred_element_type=jnp.float32)
    abuf[0, CONV_PAD:CONV_PAD + ts, :] = glu_in[:, :c] * _sigmoid(glu_in[:, c:])
    q = jnp.dot(h, w_ref[:, 2 * c:3 * c], preferred_element_type=jnp.float32)
    q_ref[0] = (q * (HEAD_DIM ** -0.5)).astype(jnp.bfloat16)
    k = jnp.dot(h, w_ref[:, 3 * c:4 * c], preferred_element_type=jnp.float32)
    k_ref[0] = k
    kb_ref[0] = k.astype(jnp.bfloat16)
    v = jnp.dot(h, w_ref[:, 4 * c:5 * c], preferred_element_type=jnp.float32)
    v_ref[0] = v
    vb_ref[0] = v.astype(jnp.bfloat16)

    n_sh = ts + CONV_PAD - SUBLANES
    for sh in range(1, SUBLANES):
        abuf[sh, 0:n_sh, :] = abuf[0, sh:sh + n_sh, :]
    rc = min(ts, 64)
    for r0 in range(0, ts, rc):
        acc = jnp.zeros((rc, c), jnp.float32)
        for w in range(CONV_WIDTH):
            off = r0 + CONV_PAD - (CONV_WIDTH - 1) + w
            sh = off % SUBLANES
            acc = acc + abuf[sh, off - sh:off - sh + rc, :] * cw_ref[w:w + 1, :]
        y = acc + cb_ref[...]
        mu = jnp.mean(y, axis=-1, keepdims=True)
        d = y - mu
        var = jnp.mean(d * d, axis=-1, keepdims=True)
        yn = d * lax.rsqrt(var + EPS) * lg_ref[...] + lb_ref[...]
        conv_ref[0, r0:r0 + rc, :] = (yn * _sigmoid(yn)).astype(jnp.bfloat16)

    tail = abuf[0, ts:ts + CONV_PAD, :]
    tail_ref[0] = tail
    abuf[0, 0:CONV_PAD, :] = tail


def _in_proj(x, g_mix, w_in_bf, left, conv_w, conv_b, ln_g, ln_b):
    b, s, d = x.shape
    c = conv_w.shape[1]
    ts = min(s, 512)
    assert s % ts == 0 and ts >= CONV_PAD and ts % SUBLANES == 0
    cw = jnp.pad(conv_w, ((0, CONV_PAD - CONV_WIDTH), (0, 0)))
    row = lambda a: a.reshape(1, -1)
    tok = lambda bi, si: (bi, si, 0)
    const2 = lambda bi, si: (0, 0)
    f32, bf16 = jnp.float32, jnp.bfloat16
    outs = pl.pallas_call(
        functools.partial(_in_proj_kernel, ts=ts, c=c),
        grid=(b, s // ts),
        in_specs=[
            pl.BlockSpec((1, ts, d), tok),
            pl.BlockSpec((1, d), const2),
            pl.BlockSpec(w_in_bf.shape, const2),
            pl.BlockSpec((1, CONV_PAD, c), lambda bi, si: (bi, 0, 0)),
            pl.BlockSpec((CONV_PAD, c), const2),
            pl.BlockSpec((1, c), const2),
            pl.BlockSpec((1, c), const2),
            pl.BlockSpec((1, c), const2),
        ],
        out_specs=[pl.BlockSpec((1, ts, c), tok)] * 6
        + [pl.BlockSpec((1, CONV_PAD, c), lambda bi, si: (bi, 0, 0))],
        out_shape=[
            jax.ShapeDtypeStruct((b, s, c), bf16),
            jax.ShapeDtypeStruct((b, s, c), f32),
            jax.ShapeDtypeStruct((b, s, c), f32),
            jax.ShapeDtypeStruct((b, s, c), bf16),
            jax.ShapeDtypeStruct((b, s, c), bf16),
            jax.ShapeDtypeStruct((b, s, c), bf16),
            jax.ShapeDtypeStruct((b, CONV_PAD, c), f32),
        ],
        scratch_shapes=[pltpu.VMEM((SUBLANES, ts + CONV_PAD, c), f32)],
        compiler_params=pltpu.CompilerParams(
            dimension_semantics=("arbitrary", "arbitrary"), vmem_limit_bytes=VMEM_LIMIT),
        name="in_proj",
    )(x, row(g_mix), w_in_bf, left, cw, row(conv_b), row(ln_g), row(ln_b))
    return outs


def _rel_bucket(rel):
    nb = N_BUCKETS // 2
    max_exact = nb // 2
    ret = jnp.where(rel > 0, nb, 0)
    n = jnp.abs(rel)
    nf = jnp.maximum(n, 1).astype(jnp.float32)
    large = max_exact + (jnp.log(nf / max_exact) / math.log(MAX_DISTANCE / max_exact)
                         * (nb - max_exact)).astype(jnp.int32)
    large = jnp.minimum(large, nb - 1)
    return ret + jnp.where(n < max_exact, n, large)


def _bias_table(rel_bias, q_pos, k_pos, masked):
    nq, nk = q_pos.shape[0], k_pos.shape[0]
    period = nq + nk
    m = jnp.arange(period, dtype=jnp.int32)
    bucket = _rel_bucket(k_pos[0] - q_pos[0] + jnp.where(m < nk, m, m - period))
    table = rel_bias.astype(jnp.float32).T[:, None, :]
    hit = bucket[None, :, None] == jnp.arange(N_BUCKETS, dtype=jnp.int32)
    line = jnp.sum(jnp.where(hit, table, 0.0), axis=-1)
    bias = jnp.tile(line, (1, nq))[:, :nq * (period - 1)].reshape(-1, nq, period - 1)[:, :, :nk]
    if masked:
        mask = (k_pos[None, :] // CHUNK) <= (q_pos[:, None] // CHUNK)
        bias = jnp.where(mask[None], bias, NEG)
    return bias


def _split_maps(q):
    lane = lax.broadcasted_iota(jnp.int32, q.shape, 1)
    zero = jnp.zeros_like(q)
    return jnp.where(lane < HEAD_DIM, q, zero), jnp.where(lane >= HEAD_DIM, q, zero)


def _softmax_rows(s):
    p = jnp.exp(s - jnp.max(s, axis=-1, keepdims=True))
    return p / jnp.sum(p, axis=-1, keepdims=True)


def _attn_finish(s1, s2, vv, lam, g, out_scale):
    attn = (_softmax_rows(s1) - lam * _softmax_rows(s2)).astype(jnp.bfloat16)
    o = jnp.dot(attn, vv, preferred_element_type=jnp.float32)
    return (_rms(o, g) * out_scale).astype(jnp.bfloat16)


def _attn_prompt_kernel(lam_ref, q_ref, k_ref, v_ref, slab_ref, g_ref, o_ref,
                        s_ref, m_ref, l_ref, acc_ref, *, tq, out_scale):
    qi = pl.program_id(2)
    q1, q2 = _split_maps(q_ref[0])
    half = tq // 2
    fold = lambda a, op: op(a[:, :half], a[:, half:]) if half % LANES == 0 else a

    m_ref[...] = jnp.full(m_ref.shape, NEG_BIG, jnp.float32)
    l_ref[...] = jnp.zeros_like(l_ref)
    acc_ref[...] = jnp.zeros_like(acc_ref)

    def logits(j, carry):
        kj = k_ref[0, pl.ds(pl.multiple_of(j * tq, tq), tq), :]
        bias = slab_ref[0, jnp.clip(j - qi, -2, 0) + 2]
        for mp, qm in enumerate((q1, q2)):
            a = lax.dot_general(qm, kj, _NT, preferred_element_type=jnp.float32) + bias
            s_ref[mp, j] = a
            m_ref[mp] = jnp.maximum(m_ref[mp], fold(a, jnp.maximum))
        return carry

    lax.fori_loop(0, qi + 1, logits, 0)
    row_max = [jnp.broadcast_to(jnp.max(m_ref[mp], axis=-1, keepdims=True), (tq, tq)) for mp in range(2)]

    def accumulate(j, carry):
        vj = v_ref[0, pl.ds(pl.multiple_of(j * tq, tq), tq), :]
        for mp in range(2):
            p = jnp.exp(s_ref[mp, j] - row_max[mp])
            l_ref[mp] += fold(p, jnp.add)
            acc_ref[mp] += jnp.dot(p.astype(jnp.bfloat16), vj, preferred_element_type=jnp.float32)
        return carry

    lax.fori_loop(0, qi + 1, accumulate, 0)
    inv = [1.0 / jnp.sum(l_ref[mp], axis=-1, keepdims=True) for mp in range(2)]
    o = acc_ref[0] * inv[0] - lam_ref[0] * (acc_ref[1] * inv[1])
    o_ref[0] = (_rms(o, g_ref[...]) * out_scale).astype(jnp.bfloat16)


def _attn_prompt(q, kb, vb, rel_bias, lam, subln_g, out_scale):
    b, s, c = q.shape
    tq = min(s, 512)
    assert s % tq == 0 and tq % CHUNK == 0 and tq >= MAX_DISTANCE and c == N_HEADS * V_DIM
    n_kb = s // tq
    pos = jnp.arange(tq, dtype=jnp.int32)
    slabs = jnp.stack([
        _bias_table(rel_bias, pos + 2 * tq, pos, False),
        _bias_table(rel_bias, pos + tq, pos, False),
        _bias_table(rel_bias, pos, pos, True),
    ], axis=1)
    width = tq // 2 if (tq // 2) % LANES == 0 else tq
    return pl.pallas_call(
        functools.partial(_attn_prompt_kernel, tq=tq, out_scale=out_scale),
        grid=(b, N_HEADS, n_kb),
        in_specs=[
            pl.BlockSpec(memory_space=pltpu.SMEM),
            pl.BlockSpec((1, tq, V_DIM), lambda bi, hi, qi: (bi, qi, hi)),
            pl.BlockSpec((1, s, V_DIM), lambda bi, hi, qi: (bi, 0, hi)),
            pl.BlockSpec((1, s, V_DIM), lambda bi, hi, qi: (bi, 0, hi)),
            pl.BlockSpec((1, 3, tq, tq), lambda bi, hi, qi: (hi, 0, 0, 0)),
            pl.BlockSpec((1, V_DIM), lambda bi, hi, qi: (0, 0)),
        ],
        out_specs=pl.BlockSpec((1, tq, V_DIM), lambda bi, hi, qi: (bi, qi, hi)),
        out_shape=jax.ShapeDtypeStruct((b, s, c), jnp.bfloat16),
        scratch_shapes=[
            pltpu.VMEM((2, n_kb, tq, tq), jnp.float32),
            pltpu.VMEM((2, tq, width), jnp.float32),
            pltpu.VMEM((2, tq, width), jnp.float32),
            pltpu.VMEM((2, tq, V_DIM), jnp.float32),
        ],
        compiler_params=pltpu.CompilerParams(
            dimension_semantics=("arbitrary",) * 3, vmem_limit_bytes=VMEM_LIMIT),
        name="attn_prompt",
    )(lam, q, kb, vb, slabs, subln_g.reshape(1, -1))


def _attn_sample_kernel(lam_ref, q_ref, k_ref, v_ref, bias_ref, g_ref, o_ref, *, out_scale):
    q1, q2 = _split_maps(q_ref[0])
    kk = k_ref[0]
    bias = bias_ref[0]
    s1 = lax.dot_general(q1, kk, _NT, preferred_element_type=jnp.float32) + bias
    s2 = lax.dot_general(q2, kk, _NT, preferred_element_type=jnp.float32) + bias
    o_ref[0] = _attn_finish(s1, s2, v_ref[0], lam_ref[0], g_ref[...], out_scale)


def _attn_sample(q, keys, vals, bias, lam, subln_g, out_scale):
    b, sq, c = q.shape
    sk = keys.shape[1]
    return pl.pallas_call(
        functools.partial(_attn_sample_kernel, out_scale=out_scale),
        grid=(b, N_HEADS),
        in_specs=[
            pl.BlockSpec(memory_space=pltpu.SMEM),
            pl.BlockSpec((1, sq, V_DIM), lambda bi, hi: (bi, 0, hi)),
            pl.BlockSpec((1, sk, V_DIM), lambda bi, hi: (bi, 0, hi)),
            pl.BlockSpec((1, sk, V_DIM), lambda bi, hi: (bi, 0, hi)),
            pl.BlockSpec((1, sq, sk), lambda bi, hi: (hi, 0, 0)),
            pl.BlockSpec((1, V_DIM), lambda bi, hi: (0, 0)),
        ],
        out_specs=pl.BlockSpec((1, sq, V_DIM), lambda bi, hi: (bi, 0, hi)),
        out_shape=jax.ShapeDtypeStruct((b, sq, c), jnp.bfloat16),
        compiler_params=pltpu.CompilerParams(
            dimension_semantics=("arbitrary",) * 2, vmem_limit_bytes=VMEM_LIMIT),
        name="attn_sample",
    )(lam, q, keys, vals, bias, subln_g.reshape(1, -1))


def _mid_kernel(conv_ref, att_ref, x_ref, wc_ref, wa_ref, g_ref, wq_ref, sk_ref,
                x1_ref, h2_ref, st_ref):
    x1 = (x_ref[...]
          + jnp.dot(conv_ref[...], wc_ref[...], preferred_element_type=jnp.float32)
          + jnp.dot(att_ref[...], wa_ref[...], preferred_element_type=jnp.float32))
    x1_ref[...] = x1
    h2 = _rms(x1, g_ref[...]).astype(jnp.bfloat16)
    h2_ref[...] = h2
    qq = jnp.dot(h2, wq_ref[...], preferred_element_type=jnp.float32).astype(jnp.bfloat16)
    for rp in range(2 * R_HEADS):
        st_ref[rp] = lax.dot_general(sk_ref[rp], qq[:, rp * N_KEYS:(rp + 1) * N_KEYS], _NT,
                                     preferred_element_type=jnp.float32)


def _mid(conv, att, x2d, w_out_bf, g_ffn, w_query_bf, sub_keys_bf):
    t, d = x2d.shape
    c = conv.shape[1]
    tb = min(t, 512)
    assert t % tb == 0
    dq = w_query_bf.shape[1]
    nrp = sub_keys_bf.shape[0]
    tok = lambda i: (i, 0)
    const = lambda i: (0, 0)
    return pl.pallas_call(
        _mid_kernel,
        grid=(t // tb,),
        in_specs=[
            pl.BlockSpec((tb, c), tok),
            pl.BlockSpec((tb, c), tok),
            pl.BlockSpec((tb, d), tok),
            pl.BlockSpec((c, d), const),
            pl.BlockSpec((c, d), lambda i: (1, 0)),
            pl.BlockSpec((1, d), const),
            pl.BlockSpec((d, dq), const),
            pl.BlockSpec(sub_keys_bf.shape, lambda i: (0, 0, 0)),
        ],
        out_specs=[
            pl.BlockSpec((tb, d), tok),
            pl.BlockSpec((tb, d), tok),
            pl.BlockSpec((nrp, N_KEYS, tb), lambda i: (0, 0, i)),
        ],
        out_shape=[
            jax.ShapeDtypeStruct((t, d), jnp.float32),
            jax.ShapeDtypeStruct((t, d), jnp.bfloat16),
            jax.ShapeDtypeStruct((nrp, N_KEYS, t), jnp.float32),
        ],
        compiler_params=pltpu.CompilerParams(
            dimension_semantics=("arbitrary",), vmem_limit_bytes=VMEM_LIMIT),
        name="mid",
    )(conv, att, x2d, w_out_bf, w_out_bf, g_ffn.reshape(1, -1), w_query_bf, sub_keys_bf)


def _ce(a, b):
    if a is None:
        return b, None
    if b is None:
        return a, None
    return jnp.maximum(a, b), jnp.minimum(a, b)


def _sort_desc(xs):
    xs = list(xs)
    n = len(xs)
    p = 1
    while p < n:
        k = p
        while k >= 1:
            for j in range(k % p, n - k, 2 * k):
                for i in range(min(k, n - j - k)):
                    if (i + j) // (2 * p) == (i + j + k) // (2 * p):
                        xs[i + j], xs[i + j + k] = _ce(xs[i + j], xs[i + j + k])
            k //= 2
        p *= 2
    return xs


def _bitonic_top(a, b):
    n = len(a)
    return [_ce(a[i], b[n - 1 - i])[0] for i in range(n)]


def _bitonic_sort_desc(xs):
    xs = list(xs)
    n = len(xs)
    d = n // 2
    while d >= 1:
        for i in range(n):
            if i & d == 0:
                xs[i], xs[i + d] = _ce(xs[i], xs[i + d])
        d //= 2
    return xs


def _fill(xs):
    return [jnp.full((SUBLANES, LANES), NEG_BIG, jnp.float32) if x is None else x for x in xs]


def _sublane_merge_sorted(xs):
    for shift in (4, 2, 1):
        other = [pltpu.roll(x, shift, 0) for x in xs]
        xs = _bitonic_sort_desc(_bitonic_top(xs, other))
    return xs


def _sublane_merge_kth(xs):
    for shift in (4, 2):
        other = [pltpu.roll(x, shift, 0) for x in xs]
        xs = _bitonic_sort_desc(_bitonic_top(xs, other))
    other = [pltpu.roll(x, 1, 0) for x in xs]
    top = _bitonic_top(xs, other)
    return functools.reduce(jnp.minimum, top)


def _top16_rows(s):
    tiles = [s[i * SUBLANES:(i + 1) * SUBLANES, :] for i in range(N_KEYS // SUBLANES)]
    return _sublane_merge_sorted(_sort_desc(tiles))


def _dup_bf16(x):
    b = pltpu.bitcast(x.astype(jnp.bfloat16).astype(jnp.float32), jnp.uint32)
    return b | (b >> 16)


def _route_kernel(st_ref, rank_ref, n_ref, ea_ref, eb_ref, *, tb):
    sub = lax.broadcasted_iota(jnp.int32, (SUBLANES, LANES), 0)
    for g in range(tb // LANES):
        cols = slice(g * LANES, (g + 1) * LANES)
        for r in range(R_HEADS):
            s1 = st_ref[2 * r, :, cols]
            s2 = st_ref[2 * r + 1, :, cols]
            v1 = _top16_rows(s1)
            v2 = _top16_rows(s2)
            pack = lambda v, base: functools.reduce(
                lambda acc, j: jnp.where(sub == j, v[base + j], acc), range(1, SUBLANES), v[base])
            w1a, w1b, w2a, w2b = pack(v1, 0), pack(v1, 8), pack(v2, 0), pack(v2, 8)
            neg = jnp.full((SUBLANES, LANES), NEG_BIG, jnp.float32)
            cands = [
                v1[0] + w2a,
                v1[0] + w2b,
                jnp.where(sub >= 1, v2[0] + w1a, neg),
                v2[0] + w1b,
                jnp.where(sub >= 1, v1[1] + w2a, neg),
                jnp.where(sub >= 2, v2[1] + w1a, neg),
                jnp.where((sub >= 2) & (sub <= 4), v1[2] + w2a, neg),
                jnp.where((sub >= 2) & (sub <= 3), v1[3] + w2a, neg),
                jnp.where(sub == 2, v1[4] + w2a, neg),
            ]
            srt = _fill(_sort_desc(cands + [None] * (TOPK - len(cands))))
            thr = _sublane_merge_kth(srt)
            m1, m2 = v1[0], v2[0]
            top = m1 + m2
            z = functools.reduce(
                lambda acc, cnd: acc + jnp.where(cnd >= thr, jnp.exp(cnd - top), 0.0), cands,
                jnp.zeros((SUBLANES, LANES), jnp.float32))
            for shift in (4, 2, 1):
                z = z + pltpu.roll(z, shift, 0)
            rank2 = jnp.zeros((N_KEYS, LANES), jnp.float32)
            cnt = jnp.zeros((N_KEYS, LANES), jnp.float32)
            thr_row = thr[0:1, :]
            for j in range(TOPK):
                v2j = v2[j][0:1, :]
                rank2 = jnp.where(v2j > s2, j + 1.0, rank2)
                cnt = jnp.where(s1 + v2j >= thr_row, j + 1.0, cnt)
            n_ref[r, :, cols] = _dup_bf16(cnt)
            ea_ref[r, :, cols] = _dup_bf16(jnp.exp(s1 - m1[0:1, :]) / z[0:1, :])
            eb_ref[r, :, cols] = jnp.exp(s2 - m2[0:1, :]).astype(jnp.bfloat16)
            rank_ref[r, :, cols] = rank2.astype(jnp.bfloat16)


def _route(st):
    nrp, nk, t = st.shape
    tb = min(t, 256)
    assert t % tb == 0 and tb % LANES == 0 and nk == N_KEYS and nrp == 2 * R_HEADS
    row_spec = pl.BlockSpec((R_HEADS, nk, tb), lambda i: (0, 0, i))
    return pl.pallas_call(
        functools.partial(_route_kernel, tb=tb),
        grid=(t // tb,),
        in_specs=[pl.BlockSpec((nrp, nk, tb), lambda i: (0, 0, i))],
        out_specs=[row_spec] * 4,
        out_shape=[
            jax.ShapeDtypeStruct((R_HEADS, nk, t), jnp.bfloat16),
            jax.ShapeDtypeStruct((R_HEADS, nk, t), jnp.uint32),
            jax.ShapeDtypeStruct((R_HEADS, nk, t), jnp.uint32),
            jax.ShapeDtypeStruct((R_HEADS, nk, t), jnp.bfloat16),
        ],
        compiler_params=pltpu.CompilerParams(
            dimension_semantics=("arbitrary",), vmem_limit_bytes=VMEM_LIMIT),
        name="route",
    )(st)


def _gelu(x):
    hx = 0.5 * x
    return hx + hx * lax.erf(x * (2.0 ** -0.5))


def _packed_row(ref, r, row, cols):
    tile = jnp.broadcast_to(ref[r, row:row + 1, cols], (SUBLANES, LANES))
    return pltpu.bitcast(tile, jnp.bfloat16)


def _peer_kernel(h_ref, u_ref, vt_ref, rank_ref, eb_ref, n_odd_ref, n_even_ref, ea_odd_ref, ea_even_ref,
                 x1_ref, g_ref, y_ref, acc_ref, act_a, act_b, coef_a, coef_b, *, eb_rows, tb):
    s = pl.program_id(1)
    last = pl.num_programs(1) - 1

    @pl.when(s == 0)
    def _():
        acc_ref[...] = jnp.zeros_like(acc_ref)

    n_tg = tb // LANES
    tn = min(tb, MXU_COLS)

    def accumulate(half, coef_ref, piece):
        vt = vt_ref[:, half * eb_rows:(half + 1) * eb_rows]
        cols = slice(piece * tn, (piece + 1) * tn)
        acc_ref[:, cols] += jnp.dot(vt, coef_ref[:, cols], preferred_element_type=jnp.float32)

    def gate(cnt_ref, gain_ref, act_ref, coef_ref, tg):
        zero = jnp.zeros((PACKED_ROWS, LANES), jnp.bfloat16)
        cols = slice(tg * LANES, (tg + 1) * LANES)
        for ci in range(eb_rows // N_KEYS):
            cnt = [_packed_row(cnt_ref, r, ci, cols) for r in range(R_HEADS)]
            ea = [_packed_row(gain_ref, r, ci, cols) for r in range(R_HEADS)]
            for ch in range(N_KEYS // PACKED_ROWS):
                keys = slice(ch * PACKED_ROWS, (ch + 1) * PACKED_ROWS)
                g = zero
                for r in range(R_HEADS):
                    sel = jnp.minimum(jnp.maximum(cnt[r] - rank_ref[r, keys, cols], 0), 1)
                    g = g + (ea[r] * sel) * eb_ref[r, keys, cols]
                rows = slice(ci * N_KEYS + ch * PACKED_ROWS, ci * N_KEYS + (ch + 1) * PACKED_ROWS)
                coef_ref[rows, cols] = g * _gelu(act_ref[rows, cols].astype(jnp.bfloat16))

    def activate(half, act_ref, piece):
        u = u_ref[half * eb_rows:(half + 1) * eb_rows, :]
        cols = slice(piece * tn, (piece + 1) * tn)
        act_ref[:, cols] = lax.dot_general(u, h_ref[cols, :], _NT, preferred_element_type=jnp.float32)

    n_piece = tb // tn

    @pl.when(s > 0)
    def _():
        for piece in range(n_piece):
            accumulate(0, coef_a, piece)
        for tg in range(n_tg):
            gate(n_odd_ref, ea_odd_ref, act_b, coef_b, tg)
        for piece in range(n_piece):
            accumulate(1, coef_b, piece)

    @pl.when(s < last)
    def _():
        per = n_tg // n_piece
        for piece in range(n_piece):
            activate(1, act_b, piece)
        for piece in range(n_piece):
            activate(0, act_a, piece)
            for tg in range(piece * per, (piece + 1) * per):
                gate(n_even_ref, ea_even_ref, act_a, coef_a, tg)

    @pl.when(s == last)
    def _():
        y_ref[...] = _rms(x1_ref[...] + acc_ref[...].T, g_ref[...])


def _peer(h2, u_bf, vt_bf, rank2, cnt, ea, eb, x1, g_final):
    t, d = h2.shape
    n_exp = u_bf.shape[0]
    tb = min(t, 512)
    eb_rows = SUBLANES * N_KEYS
    assert t % tb == 0 and n_exp % (2 * eb_rows) == 0 and n_exp == N_KEYS * N_KEYS
    n_blocks = n_exp // eb_rows
    n_steps = n_blocks // 2 + 1
    clamp = lambda b, hi: jnp.clip(b, 0, hi)
    tok = lambda i, s: (i, 0)
    tile_spec = pl.BlockSpec((R_HEADS, N_KEYS, tb), lambda i, s: (0, 0, i))
    odd_spec = pl.BlockSpec((R_HEADS, SUBLANES, tb), lambda i, s: (0, clamp(2 * s - 1, n_blocks - 1), i))
    even_spec = pl.BlockSpec((R_HEADS, SUBLANES, tb), lambda i, s: (0, clamp(2 * s, n_blocks - 1), i))
    return pl.pallas_call(
        functools.partial(_peer_kernel, eb_rows=eb_rows, tb=tb),
        grid=(t // tb, n_steps),
        in_specs=[
            pl.BlockSpec((tb, d), tok),
            pl.BlockSpec((2 * eb_rows, d), lambda i, s: (clamp(s, n_steps - 2), 0)),
            pl.BlockSpec((d, 2 * eb_rows), lambda i, s: (0, clamp(s - 1, n_steps - 2))),
            tile_spec, tile_spec, odd_spec, even_spec, odd_spec, even_spec,
            pl.BlockSpec((tb, d), tok),
            pl.BlockSpec((1, d), lambda i, s: (0, 0)),
        ],
        out_specs=pl.BlockSpec((tb, d), tok),
        out_shape=jax.ShapeDtypeStruct((t, d), jnp.float32),
        scratch_shapes=[
            pltpu.VMEM((d, tb), jnp.float32),
            pltpu.VMEM((eb_rows, tb), jnp.float32),
            pltpu.VMEM((eb_rows, tb), jnp.float32),
            pltpu.VMEM((eb_rows, tb), jnp.bfloat16),
            pltpu.VMEM((eb_rows, tb), jnp.bfloat16),
        ],
        compiler_params=pltpu.CompilerParams(
            dimension_semantics=("arbitrary", "arbitrary"), vmem_limit_bytes=VMEM_LIMIT),
        name="peer",
    )(h2, u_bf, vt_bf, rank2, eb, cnt, cnt, ea, ea, x1, g_final.reshape(1, -1))


def _stream(x, left, attend, p):
    b, s, d = x.shape
    q, k, v, kb, vb, conv, tail = _in_proj(x, p["g_mix"], p["w_in"], left, p["conv_w"], p["conv_b"],
                                           p["ln_g"], p["ln_b"])
    att = attend(q, kb, vb)
    c = conv.shape[-1]
    x1, h2, st = _mid(conv.reshape(b * s, c), att.reshape(b * s, c), x.reshape(b * s, d),
                      p["w_out"], p["g_ffn"], p["w_query"], p["sub_keys"])
    rank2, cnt, ea, eb = _route(st)
    y = _peer(h2, p["peer_u"], p["peer_vt"], rank2, cnt, ea, eb, x1, p["g_final"])
    k = k.reshape(b, s, N_HEADS, 2, HEAD_DIM)
    v = v.reshape(b, s, N_HEADS, V_DIM)
    return y.reshape(b, s, d), k, v, tail[:, CONV_PAD - (CONV_WIDTH - 1):]


def kernel(x_prompt, x_sample, cache_k, cache_v, state_conv, g_mix, w_in, conv_w, conv_b, conv_ln_g, conv_ln_b, lambda_q1, lambda_k1, lambda_q2, lambda_k2, subln_g, rel_bias, w_out, g_ffn, w_query, sub_keys, peer_u, peer_v, g_final):
    depth = w_in.shape[0]
    assert depth == 1, "single-layer step"
    l = 0
    bf16 = jnp.bfloat16
    b, s, d = x_prompt.shape
    bd, sd, _ = x_sample.shape
    past = cache_k.shape[2]
    c = conv_w.shape[-1]

    lam_init = _lambda_init(l)
    lam = (jnp.exp(jnp.sum(lambda_q1[l].astype(jnp.float32) * lambda_k1[l].astype(jnp.float32)))
           - jnp.exp(jnp.sum(lambda_q2[l].astype(jnp.float32) * lambda_k2[l].astype(jnp.float32)))
           + lam_init).reshape(1)
    out_scale = 1.0 - lam_init

    p = {
        "g_mix": g_mix[l], "w_in": w_in[l].astype(bf16), "conv_w": conv_w[l], "conv_b": conv_b[l],
        "ln_g": conv_ln_g[l], "ln_b": conv_ln_b[l], "w_out": w_out[l].astype(bf16), "g_ffn": g_ffn[l],
        "w_query": w_query[l].astype(bf16),
        "sub_keys": sub_keys[l].reshape(2 * R_HEADS, N_KEYS, -1).astype(bf16),
        "peer_u": peer_u[l].astype(bf16), "peer_vt": peer_v[l].astype(bf16).T, "g_final": g_final,
    }

    attend_p = lambda q, kb, vb: _attn_prompt(q, kb, vb, rel_bias, lam, subln_g[l], out_scale)
    y_p, k_p, v_p, tail_p = _stream(x_prompt, jnp.zeros((b, CONV_PAD, c), jnp.float32), attend_p, p)

    sk = past + sd
    sk_pad = -(-sk // LANES) * LANES
    pos_s = past + jnp.arange(sd, dtype=jnp.int32)
    bias_s = _bias_table(rel_bias, pos_s, jnp.arange(sk, dtype=jnp.int32), True)
    bias_s = jnp.pad(bias_s, ((0, 0), (0, 0), (0, sk_pad - sk)), constant_values=NEG)
    ck = cache_k[l].reshape(bd, past, c).astype(bf16)
    cv = cache_v[l].reshape(bd, past, c).astype(bf16)
    pad = jnp.zeros((bd, sk_pad - sk, c), bf16)

    def attend_s(q, kb, vb):
        keys = jnp.concatenate([ck, kb, pad], axis=1)
        vals = jnp.concatenate([cv, vb, pad], axis=1)
        return _attn_sample(q, keys, vals, bias_s, lam, subln_g[l], out_scale)

    left_s = jnp.pad(state_conv[l], ((0, 0), (CONV_PAD - (CONV_WIDTH - 1), 0), (0, 0)))
    y_s, k_s, v_s, tail_s = _stream(x_sample, left_s, attend_s, p)

    return (y_p, y_s, k_p[None], v_p[None], tail_p[None], k_s[None], v_s[None], tail_s[None])
```

```python
import functools
import math

import jax
import jax.numpy as jnp
from jax import lax
from jax.experimental import pallas as pl
from jax.experimental.pallas import tpu as pltpu

CHUNK = 64
CONV_WIDTH = 31
CONV_PAD = 32
N_HEADS = 4
HEAD_DIM = 64
V_DIM = 2 * HEAD_DIM
N_BUCKETS = 32
MAX_DISTANCE = 128
N_KEYS = 128
R_HEADS = 8
TOPK = 16
EPS = 1e-6
NEG = -1e30
NEG_BIG = -3.0e38
LANES = 128
SUBLANES = 8
PACKED_ROWS = 2 * SUBLANES
MXU_COLS = 256
VMEM_LIMIT = 48 * 1024 * 1024

_NT = (((1,), (1,)), ((), ()))


def _lambda_init(layer):
    return 0.8 - 0.6 * math.exp(-0.3 * layer)


def _rms(xf, g):
    return xf * lax.rsqrt(jnp.mean(xf * xf, axis=-1, keepdims=True) + EPS) * g


def _sigmoid(x):
    return 1.0 / (1.0 + jnp.exp(-x))


def _in_proj_kernel(x_ref, g_ref, w_ref, left_ref, cw_ref, cb_ref, lg_ref, lb_ref,
                    q_ref, k_ref, v_ref, kb_ref, vb_ref, conv_ref, tail_ref, abuf, *, ts, c):
    s = pl.program_id(1)
    h = _rms(x_ref[0], g_ref[...]).astype(jnp.bfloat16)

    @pl.when(s == 0)
    def _():
        abuf[0, 0:CONV_PAD, :] = left_ref[0]

    glu_in = jnp.dot(h, w_ref[:, 0:2 * c], preferred_element_type=jnp.float32)
    abuf[0, CONV_PAD:CONV_PAD + ts, :] = glu_in[:, :c] * _sigmoid(glu_in[:, c:])
    q = jnp.dot(h, w_ref[:, 2 * c:3 * c], preferred_element_type=jnp.float32)
    q_ref[0] = (q * (HEAD_DIM ** -0.5)).astype(jnp.bfloat16)
    k = jnp.dot(h, w_ref[:, 3 * c:4 * c], preferred_element_type=jnp.float32)
    k_ref[0] = k
    kb_ref[0] = k.astype(jnp.bfloat16)
    v = jnp.dot(h, w_ref[:, 4 * c:5 * c], preferred_element_type=jnp.float32)
    v_ref[0] = v
    vb_ref[0] = v.astype(jnp.bfloat16)

    n_sh = ts + CONV_PAD - SUBLANES
    for sh in range(1, SUBLANES):
        abuf[sh, 0:n_sh, :] = abuf[0, sh:sh + n_sh, :]
    rc = min(ts, 64)
    for r0 in range(0, ts, rc):
        acc = jnp.zeros((rc, c), jnp.float32)
        for w in range(CONV_WIDTH):
            off = r0 + CONV_PAD - (CONV_WIDTH - 1) + w
            sh = off % SUBLANES
            acc = acc + abuf[sh, off - sh:off - sh + rc, :] * cw_ref[w:w + 1, :]
        y = acc + cb_ref[...]
        mu = jnp.mean(y, axis=-1, keepdims=True)
        d = y - mu
        var = jnp.mean(d * d, axis=-1, keepdims=True)
        yn = d * lax.rsqrt(var + EPS) * lg_ref[...] + lb_ref[...]
        conv_ref[0, r0:r0 + rc, :] = (yn * _sigmoid(yn)).astype(jnp.bfloat16)

    tail = abuf[0, ts:ts + CONV_PAD, :]
    tail_ref[0] = tail
    abuf[0, 0:CONV_PAD, :] = tail


def _in_proj(x, g_mix, w_in_bf, left, conv_w, conv_b, ln_g, ln_b):
    b, s, d = x.shape
    c = conv_w.shape[1]
    ts = min(s, 512)
    assert s % ts == 0 and ts >= CONV_PAD and ts % SUBLANES == 0
    cw = jnp.pad(conv_w, ((0, CONV_PAD - CONV_WIDTH), (0, 0)))
    row = lambda a: a.reshape(1, -1)
    tok = lambda bi, si: (bi, si, 0)
    const2 = lambda bi, si: (0, 0)
    f32, bf16 = jnp.float32, jnp.bfloat16
    outs = pl.pallas_call(
        functools.partial(_in_proj_kernel, ts=ts, c=c),
        grid=(b, s // ts),
        in_specs=[
            pl.BlockSpec((1, ts, d), tok),
            pl.BlockSpec((1, d), const2),
            pl.BlockSpec(w_in_bf.shape, const2),
            pl.BlockSpec((1, CONV_PAD, c), lambda bi, si: (bi, 0, 0)),
            pl.BlockSpec((CONV_PAD, c), const2),
            pl.BlockSpec((1, c), const2),
            pl.BlockSpec((1, c), const2),
            pl.BlockSpec((1, c), const2),
        ],
        out_specs=[pl.BlockSpec((1, ts, c), tok)] * 6
        + [pl.BlockSpec((1, CONV_PAD, c), lambda bi, si: (bi, 0, 0))],
        out_shape=[
            jax.ShapeDtypeStruct((b, s, c), bf16),
            jax.ShapeDtypeStruct((b, s, c), f32),
            jax.ShapeDtypeStruct((b, s, c), f32),
            jax.ShapeDtypeStruct((b, s, c), bf16),
            jax.ShapeDtypeStruct((b, s, c), bf16),
            jax.ShapeDtypeStruct((b, s, c), bf16),
            jax.ShapeDtypeStruct((b, CONV_PAD, c), f32),
        ],
        scratch_shapes=[pltpu.VMEM((SUBLANES, ts + CONV_PAD, c), f32)],
        compiler_params=pltpu.CompilerParams(
            dimension_semantics=("arbitrary", "arbitrary"), vmem_limit_bytes=VMEM_LIMIT),
        name="in_proj",
    )(x, row(g_mix), w_in_bf, left, cw, row(conv_b), row(ln_g), row(ln_b))
    return outs


def _rel_bucket(rel):
    nb = N_BUCKETS // 2
    max_exact = nb // 2
    ret = jnp.where(rel > 0, nb, 0)
    n = jnp.abs(rel)
    nf = jnp.maximum(n, 1).astype(jnp.float32)
    large = max_exact + (jnp.log(nf / max_exact) / math.log(MAX_DISTANCE / max_exact)
                         * (nb - max_exact)).astype(jnp.int32)
    large = jnp.minimum(large, nb - 1)
    return ret + jnp.where(n < max_exact, n, large)


def _bias_table(rel_bias, q_pos, k_pos, masked):
    nq, nk = q_pos.shape[0], k_pos.shape[0]
    period = nq + nk
    m = jnp.arange(period, dtype=jnp.int32)
    bucket = _rel_bucket(k_pos[0] - q_pos[0] + jnp.where(m < nk, m, m - period))
    table = rel_bias.astype(jnp.float32).T[:, None, :]
    hit = bucket[None, :, None] == jnp.arange(N_BUCKETS, dtype=jnp.int32)
    line = jnp.sum(jnp.where(hit, table, 0.0), axis=-1)
    bias = jnp.tile(line, (1, nq))[:, :nq * (period - 1)].reshape(-1, nq, period - 1)[:, :, :nk]
    if masked:
        mask = (k_pos[None, :] // CHUNK) <= (q_pos[:, None] // CHUNK)
        bias = jnp.where(mask[None], bias, NEG)
    return bias


def _split_maps(q):
    lane = lax.broadcasted_iota(jnp.int32, q.shape, 1)
    zero = jnp.zeros_like(q)
    return jnp.where(lane < HEAD_DIM, q, zero), jnp.where(lane >= HEAD_DIM, q, zero)


def _softmax_rows(s):
    p = jnp.exp(s - jnp.max(s, axis=-1, keepdims=True))
    return p / jnp.sum(p, axis=-1, keepdims=True)


def _attn_finish(s1, s2, vv, lam, g, out_scale):
    attn = (_softmax_rows(s1) - lam * _softmax_rows(s2)).astype(jnp.bfloat16)
    o = jnp.dot(attn, vv, preferred_element_type=jnp.float32)
    return (_rms(o, g) * out_scale).astype(jnp.bfloat16)


def _attn_prompt_kernel(lam_ref, q_ref, k_ref, v_ref, slab_ref, g_ref, o_ref,
                        s_ref, m_ref, l_ref, acc_ref, *, tq, out_scale):
    qi = pl.program_id(2)
    q1, q2 = _split_maps(q_ref[0])
    half = tq // 2
    fold = lambda a, op: op(a[:, :half], a[:, half:]) if half % LANES == 0 else a

    m_ref[...] = jnp.full(m_ref.shape, NEG_BIG, jnp.float32)
    l_ref[...] = jnp.zeros_like(l_ref)
    acc_ref[...] = jnp.zeros_like(acc_ref)

    def logits(j, carry):
        kj = k_ref[0, pl.ds(pl.multiple_of(j * tq, tq), tq), :]
        bias = slab_ref[0, jnp.clip(j - qi, -2, 0) + 2]
        for mp, qm in enumerate((q1, q2)):
            a = lax.dot_general(qm, kj, _NT, preferred_element_type=jnp.float32) + bias
            s_ref[mp, j] = a
            m_ref[mp] = jnp.maximum(m_ref[mp], fold(a, jnp.maximum))
        return carry

    lax.fori_loop(0, qi + 1, logits, 0)
    row_max = [jnp.broadcast_to(jnp.max(m_ref[mp], axis=-1, keepdims=True), (tq, tq)) for mp in range(2)]

    def accumulate(j, carry):
        vj = v_ref[0, pl.ds(pl.multiple_of(j * tq, tq), tq), :]
        for mp in range(2):
            p = jnp.exp(s_ref[mp, j] - row_max[mp])
            l_ref[mp] += fold(p, jnp.add)
            acc_ref[mp] += jnp.dot(p.astype(jnp.bfloat16), vj, preferred_element_type=jnp.float32)
        return carry

    lax.fori_loop(0, qi + 1, accumulate, 0)
    inv = [1.0 / jnp.sum(l_ref[mp], axis=-1, keepdims=True) for mp in range(2)]
    o = acc_ref[0] * inv[0] - lam_ref[0] * (acc_ref[1] * inv[1])
    o_ref[0] = (_rms(o, g_ref[...]) * out_scale).astype(jnp.bfloat16)


def _attn_prompt(q, kb, vb, rel_bias, lam, subln_g, out_scale):
    b, s, c = q.shape
    tq = min(s, 512)
    assert s % tq == 0 and tq % CHUNK == 0 and tq >= MAX_DISTANCE and c == N_HEADS * V_DIM
    n_kb = s // tq
    pos = jnp.arange(tq, dtype=jnp.int32)
    slabs = jnp.stack([
        _bias_table(rel_bias, pos + 2 * tq, pos, False),
        _bias_table(rel_bias, pos + tq, pos, False),
        _bias_table(rel_bias, pos, pos, True),
    ], axis=1)
    width = tq // 2 if (tq // 2) % LANES == 0 else tq
    return pl.pallas_call(
        functools.partial(_attn_prompt_kernel, tq=tq, out_scale=out_scale),
        grid=(b, N_HEADS, n_kb),
        in_specs=[
            pl.BlockSpec(memory_space=pltpu.SMEM),
            pl.BlockSpec((1, tq, V_DIM), lambda bi, hi, qi: (bi, qi, hi)),
            pl.BlockSpec((1, s, V_DIM), lambda bi, hi, qi: (bi, 0, hi)),
            pl.BlockSpec((1, s, V_DIM), lambda bi, hi, qi: (bi, 0, hi)),
            pl.BlockSpec((1, 3, tq, tq), lambda bi, hi, qi: (hi, 0, 0, 0)),
            pl.BlockSpec((1, V_DIM), lambda bi, hi, qi: (0, 0)),
        ],
        out_specs=pl.BlockSpec((1, tq, V_DIM), lambda bi, hi, qi: (bi, qi, hi)),
        out_shape=jax.ShapeDtypeStruct((b, s, c), jnp.bfloat16),
        scratch_shapes=[
            pltpu.VMEM((2, n_kb, tq, tq), jnp.float32),
            pltpu.VMEM((2, tq, width), jnp.float32),
            pltpu.VMEM((2, tq, width), jnp.float32),
            pltpu.VMEM((2, tq, V_DIM), jnp.float32),
        ],
        compiler_params=pltpu.CompilerParams(
            dimension_semantics=("arbitrary",) * 3, vmem_limit_bytes=VMEM_LIMIT),
        name="attn_prompt",
    )(lam, q, kb, vb, slabs, subln_g.reshape(1, -1))


def _attn_sample_kernel(lam_ref, q_ref, k_ref, v_ref, bias_ref, g_ref, o_ref, *, out_scale):
    q1, q2 = _split_maps(q_ref[0])
    kk = k_ref[0]
    bias = bias_ref[0]
    s1 = lax.dot_general(q1, kk, _NT, preferred_element_type=jnp.float32) + bias
    s2 = lax.dot_general(q2, kk, _NT, preferred_element_type=jnp.float32) + bias
    o_ref[0] = _attn_finish(s1, s2, v_ref[0], lam_ref[0], g_ref[...], out_scale)


def _attn_sample(q, keys, vals, bias, lam, subln_g, out_scale):
    b, sq, c = q.shape
    sk = keys.shape[1]
    return pl.pallas_call(
        functools.partial(_attn_sample_kernel, out_scale=out_scale),
        grid=(b, N_HEADS),
        in_specs=[
            pl.BlockSpec(memory_space=pltpu.SMEM),
            pl.BlockSpec((1, sq, V_DIM), lambda bi, hi: (bi, 0, hi)),
            pl.BlockSpec((1, sk, V_DIM), lambda bi, hi: (bi, 0, hi)),
            pl.BlockSpec((1, sk, V_DIM), lambda bi, hi: (bi, 0, hi)),
            pl.BlockSpec((1, sq, sk), lambda bi, hi: (hi, 0, 0)),
            pl.BlockSpec((1, V_DIM), lambda bi, hi: (0, 0)),
        ],
        out_specs=pl.BlockSpec((1, sq, V_DIM), lambda bi, hi: (bi, 0, hi)),
        out_shape=jax.ShapeDtypeStruct((b, sq, c), jnp.bfloat16),
        compiler_params=pltpu.CompilerParams(
            dimension_semantics=("arbitrary",) * 2, vmem_limit_bytes=VMEM_LIMIT),
        name="attn_sample",
    )(lam, q, keys, vals, bias, subln_g.reshape(1, -1))


def _mid_kernel(conv_ref, att_ref, x_ref, wc_ref, wa_ref, g_ref, wq_ref, sk_ref,
                x1_ref, h2_ref, st_ref):
    x1 = (x_ref[...]
          + jnp.dot(conv_ref[...], wc_ref[...], preferred_element_type=jnp.float32)
          + jnp.dot(att_ref[...], wa_ref[...], preferred_element_type=jnp.float32))
    x1_ref[...] = x1
    h2 = _rms(x1, g_ref[...]).astype(jnp.bfloat16)
    h2_ref[...] = h2
    qq = jnp.dot(h2, wq_ref[...], preferred_element_type=jnp.float32).astype(jnp.bfloat16)
    for rp in range(2 * R_HEADS):
        st_ref[rp] = lax.dot_general(sk_ref[rp], qq[:, rp * N_KEYS:(rp + 1) * N_KEYS], _NT,
                                     preferred_element_type=jnp.float32)


def _mid(conv, att, x2d, w_out_bf, g_ffn, w_query_bf, sub_keys_bf):
    t, d = x2d.shape
    c = conv.shape[1]
    tb = min(t, 512)
    assert t % tb == 0
    dq = w_query_bf.shape[1]
    nrp = sub_keys_bf.shape[0]
    tok = lambda i: (i, 0)
    const = lambda i: (0, 0)
    return pl.pallas_call(
        _mid_kernel,
        grid=(t // tb,),
        in_specs=[
            pl.BlockSpec((tb, c), tok),
            pl.BlockSpec((tb, c), tok),
            pl.BlockSpec((tb, d), tok),
            pl.BlockSpec((c, d), const),
            pl.BlockSpec((c, d), lambda i: (1, 0)),
            pl.BlockSpec((1, d), const),
            pl.BlockSpec((d, dq), const),
            pl.BlockSpec(sub_keys_bf.shape, lambda i: (0, 0, 0)),
        ],
        out_specs=[
            pl.BlockSpec((tb, d), tok),
            pl.BlockSpec((tb, d), tok),
            pl.BlockSpec((nrp, N_KEYS, tb), lambda i: (0, 0, i)),
        ],
        out_shape=[
            jax.ShapeDtypeStruct((t, d), jnp.float32),
            jax.ShapeDtypeStruct((t, d), jnp.bfloat16),
            jax.ShapeDtypeStruct((nrp, N_KEYS, t), jnp.float32),
        ],
        compiler_params=pltpu.CompilerParams(
            dimension_semantics=("arbitrary",), vmem_limit_bytes=VMEM_LIMIT),
        name="mid",
    )(conv, att, x2d, w_out_bf, w_out_bf, g_ffn.reshape(1, -1), w_query_bf, sub_keys_bf)


def _ce(a, b):
    if a is None:
        return b, None
    if b is None:
        return a, None
    return jnp.maximum(a, b), jnp.minimum(a, b)


def _sort_desc(xs):
    xs = list(xs)
    n = len(xs)
    p = 1
    while p < n:
        k = p
        while k >= 1:
            for j in range(k % p, n - k, 2 * k):
                for i in range(min(k, n - j - k)):
                    if (i + j) // (2 * p) == (i + j + k) // (2 * p):
                        xs[i + j], xs[i + j + k] = _ce(xs[i + j], xs[i + j + k])
            k //= 2
        p *= 2
    return xs


def _bitonic_top(a, b):
    n = len(a)
    return [_ce(a[i], b[n - 1 - i])[0] for i in range(n)]


def _bitonic_sort_desc(xs):
    xs = list(xs)
    n = len(xs)
    d = n // 2
    while d >= 1:
        for i in range(n):
            if i & d == 0:
                xs[i], xs[i + d] = _ce(xs[i], xs[i + d])
        d //= 2
    return xs


def _fill(xs):
    return [jnp.full((SUBLANES, LANES), NEG_BIG, jnp.float32) if x is None else x for x in xs]


def _sublane_merge_sorted(xs):
    for shift in (4, 2, 1):
        other = [pltpu.roll(x, shift, 0) for x in xs]
        xs = _bitonic_sort_desc(_bitonic_top(xs, other))
    return xs


def _sublane_merge_kth(xs):
    for shift in (4, 2):
        other = [pltpu.roll(x, shift, 0) for x in xs]
        xs = _bitonic_sort_desc(_bitonic_top(xs, other))
    other = [pltpu.roll(x, 1, 0) for x in xs]
    top = _bitonic_top(xs, other)
    return functools.reduce(jnp.minimum, top)


def _top16_rows(s):
    tiles = [s[i * SUBLANES:(i + 1) * SUBLANES, :] for i in range(N_KEYS // SUBLANES)]
    return _sublane_merge_sorted(_sort_desc(tiles))


def _dup_bf16(x):
    b = pltpu.bitcast(x.astype(jnp.bfloat16).astype(jnp.float32), jnp.uint32)
    return b | (b >> 16)


def _route_kernel(st_ref, rank_ref, n_ref, ea_ref, eb_ref, *, tb):
    sub = lax.broadcasted_iota(jnp.int32, (SUBLANES, LANES), 0)
    for g in range(tb // LANES):
        cols = slice(g * LANES, (g + 1) * LANES)
        for r in range(R_HEADS):
            s1 = st_ref[2 * r, :, cols]
            s2 = st_ref[2 * r + 1, :, cols]
            v1 = _top16_rows(s1)
            v2 = _top16_rows(s2)
            pack = lambda v, base: functools.reduce(
                lambda acc, j: jnp.where(sub == j, v[base + j], acc), range(1, SUBLANES), v[base])
            w1a, w1b, w2a, w2b = pack(v1, 0), pack(v1, 8), pack(v2, 0), pack(v2, 8)
            neg = jnp.full((SUBLANES, LANES), NEG_BIG, jnp.float32)
            cands = [
                v1[0] + w2a,
                v1[0] + w2b,
                jnp.where(sub >= 1, v2[0] + w1a, neg),
                v2[0] + w1b,
                jnp.where(sub >= 1, v1[1] + w2a, neg),
                jnp.where(sub >= 2, v2[1] + w1a, neg),
                jnp.where((sub >= 2) & (sub <= 4), v1[2] + w2a, neg),
                jnp.where((sub >= 2) & (sub <= 3), v1[3] + w2a, neg),
                jnp.where(sub == 2, v1[4] + w2a, neg),
            ]
            srt = _fill(_sort_desc(cands + [None] * (TOPK - len(cands))))
            thr = _sublane_merge_kth(srt)
            m1, m2 = v1[0], v2[0]
            top = m1 + m2
            z = functools.reduce(
                lambda acc, cnd: acc + jnp.where(cnd >= thr, jnp.exp(cnd - top), 0.0), cands,
                jnp.zeros((SUBLANES, LANES), jnp.float32))
            for shift in (4, 2, 1):
                z = z + pltpu.roll(z, shift, 0)
            rank2 = jnp.zeros((N_KEYS, LANES), jnp.float32)
            cnt = jnp.zeros((N_KEYS, LANES), jnp.float32)
            thr_row = thr[0:1, :]
            for j in range(TOPK):
                v2j = v2[j][0:1, :]
                rank2 = jnp.where(v2j > s2, j + 1.0, rank2)
                cnt = jnp.where(s1 + v2j >= thr_row, j + 1.0, cnt)
            n_ref[r, :, cols] = _dup_bf16(cnt)
            ea_ref[r, :, cols] = _dup_bf16(jnp.exp(s1 - m1[0:1, :]) / z[0:1, :])
            eb_ref[r, :, cols] = jnp.exp(s2 - m2[0:1, :]).astype(jnp.bfloat16)
            rank_ref[r, :, cols] = rank2.astype(jnp.bfloat16)


def _route(st):
    nrp, nk, t = st.shape
    tb = min(t, 256)
    assert t % tb == 0 and tb % LANES == 0 and nk == N_KEYS and nrp == 2 * R_HEADS
    row_spec = pl.BlockSpec((R_HEADS, nk, tb), lambda i: (0, 0, i))
    return pl.pallas_call(
        functools.partial(_route_kernel, tb=tb),
        grid=(t // tb,),
        in_specs=[pl.BlockSpec((nrp, nk, tb), lambda i: (0, 0, i))],
        out_specs=[row_spec] * 4,
        out_shape=[
            jax.ShapeDtypeStruct((R_HEADS, nk, t), jnp.bfloat16),
            jax.ShapeDtypeStruct((R_HEADS, nk, t), jnp.uint32),
            jax.ShapeDtypeStruct((R_HEADS, nk, t), jnp.uint32),
            jax.ShapeDtypeStruct((R_HEADS, nk, t), jnp.bfloat16),
        ],
        compiler_params=pltpu.CompilerParams(
            dimension_semantics=("arbitrary",), vmem_limit_bytes=VMEM_LIMIT),
        name="route",
    )(st)


def _gelu(x):
    hx = 0.5 * x
    return hx + hx * lax.erf(x * (2.0 ** -0.5))


def _packed_row(ref, r, row, cols):
    tile = jnp.broadcast_to(ref[r, row:row + 1, cols], (SUBLANES, LANES))
    return pltpu.bitcast(tile, jnp.bfloat16)


def _peer_kernel(h_ref, u_ref, vt_ref, rank_ref, eb_ref, n_odd_ref, n_even_ref, ea_odd_ref, ea_even_ref,
                 x1_ref, g_ref, y_ref, acc_ref, act_a, act_b, coef_a, coef_b, *, eb_rows, tb):
    s = pl.program_id(1)
    last = pl.num_programs(1) - 1

    @pl.when(s == 0)
    def _():
        acc_ref[...] = jnp.zeros_like(acc_ref)

    n_tg = tb // LANES
    tn = min(tb, MXU_COLS)

    def accumulate(half, coef_ref, piece):
        vt = vt_ref[:, half * eb_rows:(half + 1) * eb_rows]
        cols = slice(piece * tn, (piece + 1) * tn)
        acc_ref[:, cols] += jnp.dot(vt, coef_ref[:, cols], preferred_element_type=jnp.float32)

    def gate(cnt_ref, gain_ref, act_ref, coef_ref, tg):
        zero = jnp.zeros((PACKED_ROWS, LANES), jnp.bfloat16)
        cols = slice(tg * LANES, (tg + 1) * LANES)
        for ci in range(eb_rows // N_KEYS):
            cnt = [_packed_row(cnt_ref, r, ci, cols) for r in range(R_HEADS)]
            ea = [_packed_row(gain_ref, r, ci, cols) for r in range(R_HEADS)]
            for ch in range(N_KEYS // PACKED_ROWS):
                keys = slice(ch * PACKED_ROWS, (ch + 1) * PACKED_ROWS)
                g = zero
                for r in range(R_HEADS):
                    sel = jnp.minimum(jnp.maximum(cnt[r] - rank_ref[r, keys, cols], 0), 1)
                    g = g + (ea[r] * sel) * eb_ref[r, keys, cols]
                rows = slice(ci * N_KEYS + ch * PACKED_ROWS, ci * N_KEYS + (ch + 1) * PACKED_ROWS)
                coef_ref[rows, cols] = g * _gelu(act_ref[rows, cols].astype(jnp.bfloat16))

    def activate(half, act_ref, piece):
        u = u_ref[half * eb_rows:(half + 1) * eb_rows, :]
        cols = slice(piece * tn, (piece + 1) * tn)
        act_ref[:, cols] = lax.dot_general(u, h_ref[cols, :], _NT, preferred_element_type=jnp.float32)

    n_piece = tb // tn

    per = n_tg // n_piece

    halves = ((n_even_ref, ea_even_ref, act_a, coef_a), (n_odd_ref, ea_odd_ref, act_b, coef_b))

    def step(drain, fill):
        for half, (cnt_ref, gain_ref, act_ref, coef_ref) in enumerate(halves):
            for piece in range(n_piece):
                if drain:
                    for tg in range(piece * per, (piece + 1) * per):
                        gate(cnt_ref, gain_ref, act_ref, coef_ref, tg)
                    accumulate(half, coef_ref, piece)
                if fill:
                    activate(half, act_ref, piece)

    pl.when(s == 0)(functools.partial(step, False, True))
    pl.when((s > 0) & (s < last))(functools.partial(step, True, True))
    pl.when(s == last)(functools.partial(step, True, False))

    @pl.when(s == last)
    def _():
        y_ref[...] = _rms(x1_ref[...] + acc_ref[...].T, g_ref[...])


def _peer(h2, u_bf, vt_bf, rank2, cnt, ea, eb, x1, g_final):
    t, d = h2.shape
    n_exp = u_bf.shape[0]
    tb = min(t, 512)
    eb_rows = SUBLANES * N_KEYS
    assert t % tb == 0 and n_exp % (2 * eb_rows) == 0 and n_exp == N_KEYS * N_KEYS
    n_blocks = n_exp // eb_rows
    n_steps = n_blocks // 2 + 1
    clamp = lambda b, hi: jnp.clip(b, 0, hi)
    tok = lambda i, s: (i, 0)
    tile_spec = pl.BlockSpec((R_HEADS, N_KEYS, tb), lambda i, s: (0, 0, i))
    odd_spec = pl.BlockSpec((R_HEADS, SUBLANES, tb), lambda i, s: (0, clamp(2 * s - 1, n_blocks - 1), i))
    even_spec = pl.BlockSpec((R_HEADS, SUBLANES, tb), lambda i, s: (0, clamp(2 * s - 2, n_blocks - 1), i))
    return pl.pallas_call(
        functools.partial(_peer_kernel, eb_rows=eb_rows, tb=tb),
        grid=(t // tb, n_steps),
        in_specs=[
            pl.BlockSpec((tb, d), tok),
            pl.BlockSpec((2 * eb_rows, d), lambda i, s: (clamp(s, n_steps - 2), 0)),
            pl.BlockSpec((d, 2 * eb_rows), lambda i, s: (0, clamp(s - 1, n_steps - 2))),
            tile_spec, tile_spec, odd_spec, even_spec, odd_spec, even_spec,
            pl.BlockSpec((tb, d), tok),
            pl.BlockSpec((1, d), lambda i, s: (0, 0)),
        ],
        out_specs=pl.BlockSpec((tb, d), tok),
        out_shape=jax.ShapeDtypeStruct((t, d), jnp.float32),
        scratch_shapes=[
            pltpu.VMEM((d, tb), jnp.float32),
            pltpu.VMEM((eb_rows, tb), jnp.float32),
            pltpu.VMEM((eb_rows, tb), jnp.float32),
            pltpu.VMEM((eb_rows, tb), jnp.bfloat16),
            pltpu.VMEM((eb_rows, tb), jnp.bfloat16),
        ],
        compiler_params=pltpu.CompilerParams(
            dimension_semantics=("arbitrary", "arbitrary"), vmem_limit_bytes=VMEM_LIMIT),
        name="peer",
    )(h2, u_bf, vt_bf, rank2, eb, cnt, cnt, ea, ea, x1, g_final.reshape(1, -1))


def _stream(x, left, attend, p):
    b, s, d = x.shape
    q, k, v, kb, vb, conv, tail = _in_proj(x, p["g_mix"], p["w_in"], left, p["conv_w"], p["conv_b"],
                                           p["ln_g"], p["ln_b"])
    att = attend(q, kb, vb)
    c = conv.shape[-1]
    x1, h2, st = _mid(conv.reshape(b * s, c), att.reshape(b * s, c), x.reshape(b * s, d),
                      p["w_out"], p["g_ffn"], p["w_query"], p["sub_keys"])
    rank2, cnt, ea, eb = _route(st)
    y = _peer(h2, p["peer_u"], p["peer_vt"], rank2, cnt, ea, eb, x1, p["g_final"])
    k = k.reshape(b, s, N_HEADS, 2, HEAD_DIM)
    v = v.reshape(b, s, N_HEADS, V_DIM)
    return y.reshape(b, s, d), k, v, tail[:, CONV_PAD - (CONV_WIDTH - 1):]


def kernel(x_prompt, x_sample, cache_k, cache_v, state_conv, g_mix, w_in, conv_w, conv_b, conv_ln_g, conv_ln_b, lambda_q1, lambda_k1, lambda_q2, lambda_k2, subln_g, rel_bias, w_out, g_ffn, w_query, sub_keys, peer_u, peer_v, g_final):
    depth = w_in.shape[0]
    assert depth == 1, "single-layer step"
    l = 0
    bf16 = jnp.bfloat16
    b, s, d = x_prompt.shape
    bd, sd, _ = x_sample.shape
    past = cache_k.shape[2]
    c = conv_w.shape[-1]

    lam_init = _lambda_init(l)
    lam = (jnp.exp(jnp.sum(lambda_q1[l].astype(jnp.float32) * lambda_k1[l].astype(jnp.float32)))
           - jnp.exp(jnp.sum(lambda_q2[l].astype(jnp.float32) * lambda_k2[l].astype(jnp.float32)))
           + lam_init).reshape(1)
    out_scale = 1.0 - lam_init

    p = {
        "g_mix": g_mix[l], "w_in": w_in[l].astype(bf16), "conv_w": conv_w[l], "conv_b": conv_b[l],
        "ln_g": conv_ln_g[l], "ln_b": conv_ln_b[l], "w_out": w_out[l].astype(bf16), "g_ffn": g_ffn[l],
        "w_query": w_query[l].astype(bf16),
        "sub_keys": sub_keys[l].reshape(2 * R_HEADS, N_KEYS, -1).astype(bf16),
        "peer_u": peer_u[l].astype(bf16), "peer_vt": peer_v[l].astype(bf16).T, "g_final": g_final,
    }

    attend_p = lambda q, kb, vb: _attn_prompt(q, kb, vb, rel_bias, lam, subln_g[l], out_scale)
    y_p, k_p, v_p, tail_p = _stream(x_prompt, jnp.zeros((b, CONV_PAD, c), jnp.float32), attend_p, p)

    sk = past + sd
    sk_pad = -(-sk // LANES) * LANES
    pos_s = past + jnp.arange(sd, dtype=jnp.int32)
    bias_s = _bias_table(rel_bias, pos_s, jnp.arange(sk, dtype=jnp.int32), True)
    bias_s = jnp.pad(bias_s, ((0, 0), (0, 0), (0, sk_pad - sk)), constant_values=NEG)
    ck = cache_k[l].reshape(bd, past, c).astype(bf16)
    cv = cache_v[l].reshape(bd, past, c).astype(bf16)
    pad = jnp.zeros((bd, sk_pad - sk, c), bf16)

    def attend_s(q, kb, vb):
        keys = jnp.concatenate([ck, kb, pad], axis=1)
        vals = jnp.concatenate([cv, vb, pad], axis=1)
        return _attn_sample(q, keys, vals, bias_s, lam, subln_g[l], out_scale)

    left_s = jnp.pad(state_conv[l], ((0, 0), (CONV_PAD - (CONV_WIDTH - 1), 0), (0, 0)))
    y_s, k_s, v_s, tail_s = _stream(x_sample, left_s, attend_s, p)

    return (y_p, y_s, k_p[None], v_p[None], tail_p[None], k_s[None], v_s[None], tail_s[None])
```

```python
import functools
import math

import jax
import jax.numpy as jnp
from jax import lax
from jax.experimental import pallas as pl
from jax.experimental.pallas import tpu as pltpu

CHUNK = 64
CONV_WIDTH = 31
CONV_PAD = 32
N_HEADS = 4
HEAD_DIM = 64
V_DIM = 2 * HEAD_DIM
N_BUCKETS = 32
MAX_DISTANCE = 128
N_KEYS = 128
R_HEADS = 8
TOPK = 16
EPS = 1e-6
NEG = -1e30
NEG_BIG = -3.0e38
LOG2_E = 1.4426950408889634
LANES = 128
SUBLANES = 8
PACKED_ROWS = 2 * SUBLANES
MXU_COLS = 256
VMEM_LIMIT = 48 * 1024 * 1024

_NT = (((1,), (1,)), ((), ()))


def _lambda_init(layer):
    return 0.8 - 0.6 * math.exp(-0.3 * layer)


def _rms(xf, g):
    return xf * lax.rsqrt(jnp.mean(xf * xf, axis=-1, keepdims=True) + EPS) * g


def _sigmoid(x):
    return 1.0 / (1.0 + jnp.exp(-x))


def _in_proj_kernel(x_ref, g_ref, w_ref, left_ref, cw_ref, cb_ref, lg_ref, lb_ref,
                    q_ref, k_ref, v_ref, kb_ref, vb_ref, conv_ref, tail_ref, abuf, *, ts, c):
    s = pl.program_id(1)
    h = _rms(x_ref[0], g_ref[...]).astype(jnp.bfloat16)

    @pl.when(s == 0)
    def _():
        abuf[0, 0:CONV_PAD, :] = left_ref[0]

    glu_in = jnp.dot(h, w_ref[:, 0:2 * c], preferred_element_type=jnp.float32)
    abuf[0, CONV_PAD:CONV_PAD + ts, :] = glu_in[:, :c] * _sigmoid(glu_in[:, c:])
    q = jnp.dot(h, w_ref[:, 2 * c:3 * c], preferred_element_type=jnp.float32)
    q_ref[0] = (q * (HEAD_DIM ** -0.5)).astype(jnp.bfloat16)
    k = jnp.dot(h, w_ref[:, 3 * c:4 * c], preferred_element_type=jnp.float32)
    k_ref[0] = k
    kb_ref[0] = k.astype(jnp.bfloat16)
    v = jnp.dot(h, w_ref[:, 4 * c:5 * c], preferred_element_type=jnp.float32)
    v_ref[0] = v
    vb_ref[0] = v.astype(jnp.bfloat16)

    n_sh = ts + CONV_PAD - SUBLANES
    for sh in range(1, SUBLANES):
        abuf[sh, 0:n_sh, :] = abuf[0, sh:sh + n_sh, :]
    rc = min(ts, 64)
    for r0 in range(0, ts, rc):
        acc = jnp.zeros((rc, c), jnp.float32)
        for w in range(CONV_WIDTH):
            off = r0 + CONV_PAD - (CONV_WIDTH - 1) + w
            sh = off % SUBLANES
            acc = acc + abuf[sh, off - sh:off - sh + rc, :] * cw_ref[w:w + 1, :]
        y = acc + cb_ref[...]
        mu = jnp.mean(y, axis=-1, keepdims=True)
        d = y - mu
        var = jnp.mean(d * d, axis=-1, keepdims=True)
        yn = d * lax.rsqrt(var + EPS) * lg_ref[...] + lb_ref[...]
        conv_ref[0, r0:r0 + rc, :] = (yn * _sigmoid(yn)).astype(jnp.bfloat16)

    tail = abuf[0, ts:ts + CONV_PAD, :]
    tail_ref[0] = tail
    abuf[0, 0:CONV_PAD, :] = tail


def _in_proj(x, g_mix, w_in_bf, left, conv_w, conv_b, ln_g, ln_b):
    b, s, d = x.shape
    c = conv_w.shape[1]
    ts = min(s, 512)
    assert s % ts == 0 and ts >= CONV_PAD and ts % SUBLANES == 0
    cw = jnp.pad(conv_w, ((0, CONV_PAD - CONV_WIDTH), (0, 0)))
    row = lambda a: a.reshape(1, -1)
    tok = lambda bi, si: (bi, si, 0)
    const2 = lambda bi, si: (0, 0)
    f32, bf16 = jnp.float32, jnp.bfloat16
    outs = pl.pallas_call(
        functools.partial(_in_proj_kernel, ts=ts, c=c),
        grid=(b, s // ts),
        in_specs=[
            pl.BlockSpec((1, ts, d), tok),
            pl.BlockSpec((1, d), const2),
            pl.BlockSpec(w_in_bf.shape, const2),
            pl.BlockSpec((1, CONV_PAD, c), lambda bi, si: (bi, 0, 0)),
            pl.BlockSpec((CONV_PAD, c), const2),
            pl.BlockSpec((1, c), const2),
            pl.BlockSpec((1, c), const2),
            pl.BlockSpec((1, c), const2),
        ],
        out_specs=[pl.BlockSpec((1, ts, c), tok)] * 6
        + [pl.BlockSpec((1, CONV_PAD, c), lambda bi, si: (bi, 0, 0))],
        out_shape=[
            jax.ShapeDtypeStruct((b, s, c), bf16),
            jax.ShapeDtypeStruct((b, s, c), f32),
            jax.ShapeDtypeStruct((b, s, c), f32),
            jax.ShapeDtypeStruct((b, s, c), bf16),
            jax.ShapeDtypeStruct((b, s, c), bf16),
            jax.ShapeDtypeStruct((b, s, c), bf16),
            jax.ShapeDtypeStruct((b, CONV_PAD, c), f32),
        ],
        scratch_shapes=[pltpu.VMEM((SUBLANES, ts + CONV_PAD, c), f32)],
        compiler_params=pltpu.CompilerParams(
            dimension_semantics=("arbitrary", "arbitrary"), vmem_limit_bytes=VMEM_LIMIT),
        name="in_proj",
    )(x, row(g_mix), w_in_bf, left, cw, row(conv_b), row(ln_g), row(ln_b))
    return outs


def _rel_bucket(rel):
    nb = N_BUCKETS // 2
    max_exact = nb // 2
    ret = jnp.where(rel > 0, nb, 0)
    n = jnp.abs(rel)
    nf = jnp.maximum(n, 1).astype(jnp.float32)
    large = max_exact + (jnp.log(nf / max_exact) / math.log(MAX_DISTANCE / max_exact)
                         * (nb - max_exact)).astype(jnp.int32)
    large = jnp.minimum(large, nb - 1)
    return ret + jnp.where(n < max_exact, n, large)


def _bias_table(rel_bias, q_pos, k_pos, masked):
    nq, nk = q_pos.shape[0], k_pos.shape[0]
    period = nq + nk
    m = jnp.arange(period, dtype=jnp.int32)
    bucket = _rel_bucket(k_pos[0] - q_pos[0] + jnp.where(m < nk, m, m - period))
    table = rel_bias.astype(jnp.float32).T[:, None, :]
    hit = bucket[None, :, None] == jnp.arange(N_BUCKETS, dtype=jnp.int32)
    line = jnp.sum(jnp.where(hit, table, 0.0), axis=-1)
    bias = jnp.tile(line, (1, nq))[:, :nq * (period - 1)].reshape(-1, nq, period - 1)[:, :, :nk]
    if masked:
        mask = (k_pos[None, :] // CHUNK) <= (q_pos[:, None] // CHUNK)
        bias = jnp.where(mask[None], bias, NEG)
    return bias


def _split_maps(q):
    lane = lax.broadcasted_iota(jnp.int32, q.shape, 1)
    zero = jnp.zeros_like(q)
    return jnp.where(lane < HEAD_DIM, q, zero), jnp.where(lane >= HEAD_DIM, q, zero)


def _softmax_rows(s):
    p = jnp.exp(s - jnp.max(s, axis=-1, keepdims=True))
    return p / jnp.sum(p, axis=-1, keepdims=True)


def _attn_finish(s1, s2, vv, lam, g, out_scale):
    attn = (_softmax_rows(s1) - lam * _softmax_rows(s2)).astype(jnp.bfloat16)
    o = jnp.dot(attn, vv, preferred_element_type=jnp.float32)
    return (_rms(o, g) * out_scale).astype(jnp.bfloat16)


def _attn_prompt_kernel(lam_ref, q_ref, k_ref, v_ref, slab_ref, g_ref, o_ref,
                        s_ref, m_ref, l_ref, acc_ref, *, tq, out_scale):
    qi = pl.program_id(2)
    q1, q2 = _split_maps(q_ref[0])
    half = tq // 2
    fold = lambda a, op: op(a[:, :half], a[:, half:]) if half % LANES == 0 else a

    m_ref[...] = jnp.full(m_ref.shape, NEG_BIG, jnp.float32)
    l_ref[...] = jnp.zeros_like(l_ref)
    acc_ref[...] = jnp.zeros_like(acc_ref)

    def logits(j, carry):
        kj = k_ref[0, pl.ds(pl.multiple_of(j * tq, tq), tq), :]
        bias = slab_ref[0, jnp.clip(j - qi, -2, 0) + 2]
        for mp, qm in enumerate((q1, q2)):
            a = (lax.dot_general(qm, kj, _NT, preferred_element_type=jnp.float32) + bias) * LOG2_E
            s_ref[mp, j] = a
            m_ref[mp] = jnp.maximum(m_ref[mp], fold(a, jnp.maximum))
        return carry

    lax.fori_loop(0, qi + 1, logits, 0)
    lanes = min(tq, LANES)
    row_max = [jnp.broadcast_to(jnp.max(m_ref[mp], axis=-1, keepdims=True), (tq, lanes)) for mp in range(2)]

    def accumulate(j, carry):
        vj = v_ref[0, pl.ds(pl.multiple_of(j * tq, tq), tq), :]
        for mp in range(2):
            p = jnp.concatenate([jnp.exp2(s_ref[mp, j, :, c * lanes:(c + 1) * lanes] - row_max[mp])
                                 for c in range(tq // lanes)], axis=1)
            l_ref[mp] += fold(p, jnp.add)
            acc_ref[mp] += jnp.dot(p.astype(jnp.bfloat16), vj, preferred_element_type=jnp.float32)
        return carry

    lax.fori_loop(0, qi + 1, accumulate, 0)
    inv = [1.0 / jnp.sum(l_ref[mp], axis=-1, keepdims=True) for mp in range(2)]
    o = acc_ref[0] * inv[0] - lam_ref[0] * (acc_ref[1] * inv[1])
    o_ref[0] = (_rms(o, g_ref[...]) * out_scale).astype(jnp.bfloat16)


def _attn_prompt(q, kb, vb, rel_bias, lam, subln_g, out_scale):
    b, s, c = q.shape
    tq = min(s, 512)
    assert s % tq == 0 and tq % CHUNK == 0 and tq >= MAX_DISTANCE and c == N_HEADS * V_DIM
    n_kb = s // tq
    pos = jnp.arange(tq, dtype=jnp.int32)
    slabs = jnp.stack([
        _bias_table(rel_bias, pos + 2 * tq, pos, False),
        _bias_table(rel_bias, pos + tq, pos, False),
        _bias_table(rel_bias, pos, pos, True),
    ], axis=1)
    width = tq // 2 if (tq // 2) % LANES == 0 else tq
    return pl.pallas_call(
        functools.partial(_attn_prompt_kernel, tq=tq, out_scale=out_scale),
        grid=(b, N_HEADS, n_kb),
        in_specs=[
            pl.BlockSpec(memory_space=pltpu.SMEM),
            pl.BlockSpec((1, tq, V_DIM), lambda bi, hi, qi: (bi, qi, hi)),
            pl.BlockSpec((1, s, V_DIM), lambda bi, hi, qi: (bi, 0, hi)),
            pl.BlockSpec((1, s, V_DIM), lambda bi, hi, qi: (bi, 0, hi)),
            pl.BlockSpec((1, 3, tq, tq), lambda bi, hi, qi: (hi, 0, 0, 0)),
            pl.BlockSpec((1, V_DIM), lambda bi, hi, qi: (0, 0)),
        ],
        out_specs=pl.BlockSpec((1, tq, V_DIM), lambda bi, hi, qi: (bi, qi, hi)),
        out_shape=jax.ShapeDtypeStruct((b, s, c), jnp.bfloat16),
        scratch_shapes=[
            pltpu.VMEM((2, n_kb, tq, tq), jnp.float32),
            pltpu.VMEM((2, tq, width), jnp.float32),
            pltpu.VMEM((2, tq, width), jnp.float32),
            pltpu.VMEM((2, tq, V_DIM), jnp.float32),
        ],
        compiler_params=pltpu.CompilerParams(
            dimension_semantics=("arbitrary",) * 3, vmem_limit_bytes=VMEM_LIMIT),
        name="attn_prompt",
    )(lam, q, kb, vb, slabs, subln_g.reshape(1, -1))


def _attn_sample_kernel(lam_ref, q_ref, k_ref, v_ref, bias_ref, g_ref, o_ref, *, out_scale):
    q1, q2 = _split_maps(q_ref[0])
    kk = k_ref[0]
    bias = bias_ref[0]
    s1 = lax.dot_general(q1, kk, _NT, preferred_element_type=jnp.float32) + bias
    s2 = lax.dot_general(q2, kk, _NT, preferred_element_type=jnp.float32) + bias
    o_ref[0] = _attn_finish(s1, s2, v_ref[0], lam_ref[0], g_ref[...], out_scale)


def _attn_sample(q, keys, vals, bias, lam, subln_g, out_scale):
    b, sq, c = q.shape
    sk = keys.shape[1]
    return pl.pallas_call(
        functools.partial(_attn_sample_kernel, out_scale=out_scale),
        grid=(b, N_HEADS),
        in_specs=[
            pl.BlockSpec(memory_space=pltpu.SMEM),
            pl.BlockSpec((1, sq, V_DIM), lambda bi, hi: (bi, 0, hi)),
            pl.BlockSpec((1, sk, V_DIM), lambda bi, hi: (bi, 0, hi)),
            pl.BlockSpec((1, sk, V_DIM), lambda bi, hi: (bi, 0, hi)),
            pl.BlockSpec((1, sq, sk), lambda bi, hi: (hi, 0, 0)),
            pl.BlockSpec((1, V_DIM), lambda bi, hi: (0, 0)),
        ],
        out_specs=pl.BlockSpec((1, sq, V_DIM), lambda bi, hi: (bi, 0, hi)),
        out_shape=jax.ShapeDtypeStruct((b, sq, c), jnp.bfloat16),
        compiler_params=pltpu.CompilerParams(
            dimension_semantics=("arbitrary",) * 2, vmem_limit_bytes=VMEM_LIMIT),
        name="attn_sample",
    )(lam, q, keys, vals, bias, subln_g.reshape(1, -1))


def _mid_kernel(conv_ref, att_ref, x_ref, wc_ref, wa_ref, g_ref, wq_ref, sk_ref,
                x1_ref, h2_ref, st_ref):
    x1 = (x_ref[...]
          + jnp.dot(conv_ref[...], wc_ref[...], preferred_element_type=jnp.float32)
          + jnp.dot(att_ref[...], wa_ref[...], preferred_element_type=jnp.float32))
    x1_ref[...] = x1
    h2 = _rms(x1, g_ref[...]).astype(jnp.bfloat16)
    h2_ref[...] = h2
    qq = jnp.dot(h2, wq_ref[...], preferred_element_type=jnp.float32).astype(jnp.bfloat16)
    for rp in range(2 * R_HEADS):
        st_ref[rp] = lax.dot_general(sk_ref[rp], qq[:, rp * N_KEYS:(rp + 1) * N_KEYS], _NT,
                                     preferred_element_type=jnp.float32)


def _mid(conv, att, x2d, w_out_bf, g_ffn, w_query_bf, sub_keys_bf):
    t, d = x2d.shape
    c = conv.shape[1]
    tb = min(t, 512)
    assert t % tb == 0
    dq = w_query_bf.shape[1]
    nrp = sub_keys_bf.shape[0]
    tok = lambda i: (i, 0)
    const = lambda i: (0, 0)
    return pl.pallas_call(
        _mid_kernel,
        grid=(t // tb,),
        in_specs=[
            pl.BlockSpec((tb, c), tok),
            pl.BlockSpec((tb, c), tok),
            pl.BlockSpec((tb, d), tok),
            pl.BlockSpec((c, d), const),
            pl.BlockSpec((c, d), lambda i: (1, 0)),
            pl.BlockSpec((1, d), const),
            pl.BlockSpec((d, dq), const),
            pl.BlockSpec(sub_keys_bf.shape, lambda i: (0, 0, 0)),
        ],
        out_specs=[
            pl.BlockSpec((tb, d), tok),
            pl.BlockSpec((tb, d), tok),
            pl.BlockSpec((nrp, N_KEYS, tb), lambda i: (0, 0, i)),
        ],
        out_shape=[
            jax.ShapeDtypeStruct((t, d), jnp.float32),
            jax.ShapeDtypeStruct((t, d), jnp.bfloat16),
            jax.ShapeDtypeStruct((nrp, N_KEYS, t), jnp.float32),
        ],
        compiler_params=pltpu.CompilerParams(
            dimension_semantics=("arbitrary",), vmem_limit_bytes=VMEM_LIMIT),
        name="mid",
    )(conv, att, x2d, w_out_bf, w_out_bf, g_ffn.reshape(1, -1), w_query_bf, sub_keys_bf)


def _ce(a, b):
    if a is None:
        return b, None
    if b is None:
        return a, None
    return jnp.maximum(a, b), jnp.minimum(a, b)


def _sort_desc(xs):
    xs = list(xs)
    n = len(xs)
    p = 1
    while p < n:
        k = p
        while k >= 1:
            for j in range(k % p, n - k, 2 * k):
                for i in range(min(k, n - j - k)):
                    if (i + j) // (2 * p) == (i + j + k) // (2 * p):
                        xs[i + j], xs[i + j + k] = _ce(xs[i + j], xs[i + j + k])
            k //= 2
        p *= 2
    return xs


def _bitonic_top(a, b):
    n = len(a)
    return [_ce(a[i], b[n - 1 - i])[0] for i in range(n)]


def _bitonic_sort_desc(xs):
    xs = list(xs)
    n = len(xs)
    d = n // 2
    while d >= 1:
        for i in range(n):
            if i & d == 0:
                xs[i], xs[i + d] = _ce(xs[i], xs[i + d])
        d //= 2
    return xs


def _fill(xs):
    return [jnp.full((SUBLANES, LANES), NEG_BIG, jnp.float32) if x is None else x for x in xs]


def _sublane_merge_sorted(xs):
    for shift in (4, 2, 1):
        other = [pltpu.roll(x, shift, 0) for x in xs]
        xs = _bitonic_sort_desc(_bitonic_top(xs, other))
    return xs


def _sublane_merge_kth(xs):
    for shift in (4, 2):
        other = [pltpu.roll(x, shift, 0) for x in xs]
        xs = _bitonic_sort_desc(_bitonic_top(xs, other))
    other = [pltpu.roll(x, 1, 0) for x in xs]
    top = _bitonic_top(xs, other)
    return functools.reduce(jnp.minimum, top)


def _top16_rows(s):
    tiles = [s[i * SUBLANES:(i + 1) * SUBLANES, :] for i in range(N_KEYS // SUBLANES)]
    return _sublane_merge_sorted(_sort_desc(tiles))


def _dup_bf16(x):
    b = pltpu.bitcast(x.astype(jnp.bfloat16).astype(jnp.float32), jnp.uint32)
    return b | (b >> 16)


def _route_kernel(st_ref, rank_ref, n_ref, ea_ref, eb_ref, *, tb):
    sub = lax.broadcasted_iota(jnp.int32, (SUBLANES, LANES), 0)
    for g in range(tb // LANES):
        cols = slice(g * LANES, (g + 1) * LANES)
        for r in range(R_HEADS):
            s1 = st_ref[2 * r, :, cols]
            s2 = st_ref[2 * r + 1, :, cols]
            v1 = _top16_rows(s1)
            v2 = _top16_rows(s2)
            pack = lambda v, base: functools.reduce(
                lambda acc, j: jnp.where(sub == j, v[base + j], acc), range(1, SUBLANES), v[base])
            w1a, w1b, w2a, w2b = pack(v1, 0), pack(v1, 8), pack(v2, 0), pack(v2, 8)
            neg = jnp.full((SUBLANES, LANES), NEG_BIG, jnp.float32)
            cands = [
                v1[0] + w2a,
                v1[0] + w2b,
                jnp.where(sub >= 1, v2[0] + w1a, neg),
                v2[0] + w1b,
                jnp.where(sub >= 1, v1[1] + w2a, neg),
                jnp.where(sub >= 2, v2[1] + w1a, neg),
                jnp.where((sub >= 2) & (sub <= 4), v1[2] + w2a, neg),
                jnp.where((sub >= 2) & (sub <= 3), v1[3] + w2a, neg),
                jnp.where(sub == 2, v1[4] + w2a, neg),
            ]
            srt = _fill(_sort_desc(cands + [None] * (TOPK - len(cands))))
            thr = _sublane_merge_kth(srt)
            m1, m2 = v1[0], v2[0]
            top = m1 + m2
            z = functools.reduce(
                lambda acc, cnd: acc + jnp.where(cnd >= thr, jnp.exp(cnd - top), 0.0), cands,
                jnp.zeros((SUBLANES, LANES), jnp.float32))
            for shift in (4, 2, 1):
                z = z + pltpu.roll(z, shift, 0)
            rank2 = jnp.zeros((N_KEYS, LANES), jnp.float32)
            cnt = jnp.zeros((N_KEYS, LANES), jnp.float32)
            thr_row = thr[0:1, :]
            for j in range(TOPK):
                rank2 = jnp.where(v2[j][0:1, :] > s2, j + 1.0, rank2)
            for j in range(TOPK // 2):
                cnt = jnp.where(s1 + v2[j][0:1, :] >= thr_row, j + 1.0, cnt)
            cnt_top = jnp.zeros((SUBLANES, LANES), jnp.float32)
            for j in range(TOPK // 2, TOPK):
                cnt_top = jnp.where(v1[0] + v2[j] >= thr, j + 1.0, cnt_top)
            cnt = jnp.maximum(cnt, jnp.where(s1 == v1[0][0:1, :], cnt_top[0:1, :], 0.0))
            n_ref[r, :, cols] = _dup_bf16(cnt)
            ea_ref[r, :, cols] = _dup_bf16(jnp.exp(s1 - m1[0:1, :]) / z[0:1, :])
            eb_ref[r, :, cols] = jnp.exp(s2 - m2[0:1, :]).astype(jnp.bfloat16)
            rank_ref[r, :, cols] = rank2.astype(jnp.bfloat16)


def _route(st):
    nrp, nk, t = st.shape
    tb = min(t, 256)
    assert t % tb == 0 and tb % LANES == 0 and nk == N_KEYS and nrp == 2 * R_HEADS
    row_spec = pl.BlockSpec((R_HEADS, nk, tb), lambda i: (0, 0, i))
    return pl.pallas_call(
        functools.partial(_route_kernel, tb=tb),
        grid=(t // tb,),
        in_specs=[pl.BlockSpec((nrp, nk, tb), lambda i: (0, 0, i))],
        out_specs=[row_spec] * 4,
        out_shape=[
            jax.ShapeDtypeStruct((R_HEADS, nk, t), jnp.bfloat16),
            jax.ShapeDtypeStruct((R_HEADS, nk, t), jnp.uint32),
            jax.ShapeDtypeStruct((R_HEADS, nk, t), jnp.uint32),
            jax.ShapeDtypeStruct((R_HEADS, nk, t), jnp.bfloat16),
        ],
        compiler_params=pltpu.CompilerParams(
            dimension_semantics=("arbitrary",), vmem_limit_bytes=VMEM_LIMIT),
        name="route",
    )(st)


def _gelu(x):
    hx = 0.5 * x
    return hx + hx * lax.erf(x * (2.0 ** -0.5))


def _packed_row(ref, r, row, cols):
    tile = jnp.broadcast_to(ref[r, row:row + 1, cols], (SUBLANES, LANES))
    return pltpu.bitcast(tile, jnp.bfloat16)


def _peer_kernel(h_ref, u_ref, vt_ref, rank_ref, eb_ref, n_odd_ref, n_even_ref, ea_odd_ref, ea_even_ref,
                 x1_ref, g_ref, y_ref, acc_ref, act_a, act_b, coef_a, coef_b, *, eb_rows, tb):
    s = pl.program_id(1)
    last = pl.num_programs(1) - 1

    @pl.when(s == 0)
    def _():
        acc_ref[...] = jnp.zeros_like(acc_ref)

    n_tg = tb // LANES
    tn = min(tb, MXU_COLS)

    def accumulate(half, coef_ref, piece):
        vt = vt_ref[:, half * eb_rows:(half + 1) * eb_rows]
        cols = slice(piece * tn, (piece + 1) * tn)
        acc_ref[:, cols] += jnp.dot(vt, coef_ref[:, cols], preferred_element_type=jnp.float32)

    def gate(cnt_ref, gain_ref, act_ref, coef_ref, tg):
        zero = jnp.zeros((PACKED_ROWS, LANES), jnp.bfloat16)
        cols = slice(tg * LANES, (tg + 1) * LANES)
        for ci in range(eb_rows // N_KEYS):
            cnt = [_packed_row(cnt_ref, r, ci, cols) for r in range(R_HEADS)]
            ea = [_packed_row(gain_ref, r, ci, cols) for r in range(R_HEADS)]
            for ch in range(N_KEYS // PACKED_ROWS):
                keys = slice(ch * PACKED_ROWS, (ch + 1) * PACKED_ROWS)
                g = zero
                for r in range(R_HEADS):
                    sel = jnp.minimum(jnp.maximum(cnt[r] - rank_ref[r, keys, cols], 0), 1)
                    g = g + (ea[r] * sel) * eb_ref[r, keys, cols]
                rows = slice(ci * N_KEYS + ch * PACKED_ROWS, ci * N_KEYS + (ch + 1) * PACKED_ROWS)
                coef_ref[rows, cols] = g * _gelu(act_ref[rows, cols].astype(jnp.bfloat16))

    def activate(half, act_ref, piece):
        u = u_ref[half * eb_rows:(half + 1) * eb_rows, :]
        cols = slice(piece * tn, (piece + 1) * tn)
        act_ref[:, cols] = lax.dot_general(u, h_ref[cols, :], _NT, preferred_element_type=jnp.float32)

    n_piece = tb // tn

    per = n_tg // n_piece

    halves = ((n_even_ref, ea_even_ref, act_a, coef_a), (n_odd_ref, ea_odd_ref, act_b, coef_b))

    def step(drain, fill):
        for half, (cnt_ref, gain_ref, act_ref, coef_ref) in enumerate(halves):
            for piece in range(n_piece):
                if drain:
                    for tg in range(piece * per, (piece + 1) * per):
                        gate(cnt_ref, gain_ref, act_ref, coef_ref, tg)
                    accumulate(half, coef_ref, piece)
                if fill:
                    activate(half, act_ref, piece)

    pl.when(s == 0)(functools.partial(step, False, True))
    pl.when((s > 0) & (s < last))(functools.partial(step, True, True))
    pl.when(s == last)(functools.partial(step, True, False))

    @pl.when(s == last)
    def _():
        y_ref[...] = _rms(x1_ref[...] + acc_ref[...].T, g_ref[...])


def _peer(h2, u_bf, vt_bf, rank2, cnt, ea, eb, x1, g_final):
    t, d = h2.shape
    n_exp = u_bf.shape[0]
    tb = min(t, 512)
    eb_rows = SUBLANES * N_KEYS
    assert t % tb == 0 and n_exp % (2 * eb_rows) == 0 and n_exp == N_KEYS * N_KEYS
    n_blocks = n_exp // eb_rows
    n_steps = n_blocks // 2 + 1
    clamp = lambda b, hi: jnp.clip(b, 0, hi)
    tok = lambda i, s: (i, 0)
    tile_spec = pl.BlockSpec((R_HEADS, N_KEYS, tb), lambda i, s: (0, 0, i))
    odd_spec = pl.BlockSpec((R_HEADS, SUBLANES, tb), lambda i, s: (0, clamp(2 * s - 1, n_blocks - 1), i))
    even_spec = pl.BlockSpec((R_HEADS, SUBLANES, tb), lambda i, s: (0, clamp(2 * s - 2, n_blocks - 1), i))
    return pl.pallas_call(
        functools.partial(_peer_kernel, eb_rows=eb_rows, tb=tb),
        grid=(t // tb, n_steps),
        in_specs=[
            pl.BlockSpec((tb, d), tok),
            pl.BlockSpec((2 * eb_rows, d), lambda i, s: (clamp(s, n_steps - 2), 0)),
            pl.BlockSpec((d, 2 * eb_rows), lambda i, s: (0, clamp(s - 1, n_steps - 2))),
            tile_spec, tile_spec, odd_spec, even_spec, odd_spec, even_spec,
            pl.BlockSpec((tb, d), tok),
            pl.BlockSpec((1, d), lambda i, s: (0, 0)),
        ],
        out_specs=pl.BlockSpec((tb, d), tok),
        out_shape=jax.ShapeDtypeStruct((t, d), jnp.float32),
        scratch_shapes=[
            pltpu.VMEM((d, tb), jnp.float32),
            pltpu.VMEM((eb_rows, tb), jnp.float32),
            pltpu.VMEM((eb_rows, tb), jnp.float32),
            pltpu.VMEM((eb_rows, tb), jnp.bfloat16),
            pltpu.VMEM((eb_rows, tb), jnp.bfloat16),
        ],
        compiler_params=pltpu.CompilerParams(
            dimension_semantics=("arbitrary", "arbitrary"), vmem_limit_bytes=VMEM_LIMIT),
        name="peer",
    )(h2, u_bf, vt_bf, rank2, eb, cnt, cnt, ea, ea, x1, g_final.reshape(1, -1))


def _stream(x, left, attend, p):
    b, s, d = x.shape
    q, k, v, kb, vb, conv, tail = _in_proj(x, p["g_mix"], p["w_in"], left, p["conv_w"], p["conv_b"],
                                           p["ln_g"], p["ln_b"])
    att = attend(q, kb, vb)
    c = conv.shape[-1]
    x1, h2, st = _mid(conv.reshape(b * s, c), att.reshape(b * s, c), x.reshape(b * s, d),
                      p["w_out"], p["g_ffn"], p["w_query"], p["sub_keys"])
    rank2, cnt, ea, eb = _route(st)
    y = _peer(h2, p["peer_u"], p["peer_vt"], rank2, cnt, ea, eb, x1, p["g_final"])
    k = k.reshape(b, s, N_HEADS, 2, HEAD_DIM)
    v = v.reshape(b, s, N_HEADS, V_DIM)
    return y.reshape(b, s, d), k, v, tail[:, CONV_PAD - (CONV_WIDTH - 1):]


def kernel(x_prompt, x_sample, cache_k, cache_v, state_conv, g_mix, w_in, conv_w, conv_b, conv_ln_g, conv_ln_b, lambda_q1, lambda_k1, lambda_q2, lambda_k2, subln_g, rel_bias, w_out, g_ffn, w_query, sub_keys, peer_u, peer_v, g_final):
    depth = w_in.shape[0]
    assert depth == 1, "single-layer step"
    l = 0
    bf16 = jnp.bfloat16
    b, s, d = x_prompt.shape
    bd, sd, _ = x_sample.shape
    past = cache_k.shape[2]
    c = conv_w.shape[-1]

    lam_init = _lambda_init(l)
    lam = (jnp.exp(jnp.sum(lambda_q1[l].astype(jnp.float32) * lambda_k1[l].astype(jnp.float32)))
           - jnp.exp(jnp.sum(lambda_q2[l].astype(jnp.float32) * lambda_k2[l].astype(jnp.float32)))
           + lam_init).reshape(1)
    out_scale = 1.0 - lam_init

    p = {
        "g_mix": g_mix[l], "w_in": w_in[l].astype(bf16), "conv_w": conv_w[l], "conv_b": conv_b[l],
        "ln_g": conv_ln_g[l], "ln_b": conv_ln_b[l], "w_out": w_out[l].astype(bf16), "g_ffn": g_ffn[l],
        "w_query": w_query[l].astype(bf16),
        "sub_keys": sub_keys[l].reshape(2 * R_HEADS, N_KEYS, -1).astype(bf16),
        "peer_u": peer_u[l].astype(bf16), "peer_vt": peer_v[l].astype(bf16).T, "g_final": g_final,
    }

    attend_p = lambda q, kb, vb: _attn_prompt(q, kb, vb, rel_bias, lam, subln_g[l], out_scale)
    y_p, k_p, v_p, tail_p = _stream(x_prompt, jnp.zeros((b, CONV_PAD, c), jnp.float32), attend_p, p)

    sk = past + sd
    sk_pad = -(-sk // LANES) * LANES
    pos_s = past + jnp.arange(sd, dtype=jnp.int32)
    bias_s = _bias_table(rel_bias, pos_s, jnp.arange(sk, dtype=jnp.int32), True)
    bias_s = jnp.pad(bias_s, ((0, 0), (0, 0), (0, sk_pad - sk)), constant_values=NEG)
    ck = cache_k[l].reshape(bd, past, c).astype(bf16)
    cv = cache_v[l].reshape(bd, past, c).astype(bf16)
    pad = jnp.zeros((bd, sk_pad - sk, c), bf16)

    def attend_s(q, kb, vb):
        keys = jnp.concatenate([ck, kb, pad], axis=1)
        vals = jnp.concatenate([cv, vb, pad], axis=1)
        return _attn_sample(q, keys, vals, bias_s, lam, subln_g[l], out_scale)

    left_s = jnp.pad(state_conv[l], ((0, 0), (CONV_PAD - (CONV_WIDTH - 1), 0), (0, 0)))
    y_s, k_s, v_s, tail_s = _stream(x_sample, left_s, attend_s, p)

    return (y_p, y_s, k_p[None], v_p[None], tail_p[None], k_s[None], v_s[None], tail_s[None])
```

```python
import functools
import math

import jax
import jax.numpy as jnp
from jax import lax
from jax.experimental import pallas as pl
from jax.experimental.pallas import tpu as pltpu

CHUNK = 64
CONV_WIDTH = 31
CONV_PAD = 32
N_HEADS = 4
HEAD_DIM = 64
V_DIM = 2 * HEAD_DIM
N_BUCKETS = 32
MAX_DISTANCE = 128
N_KEYS = 128
R_HEADS = 8
TOPK = 16
EPS = 1e-6
NEG = -1e30
NEG_BIG = -3.0e38
LOG2_E = 1.4426950408889634
LANES = 128
SUBLANES = 8
PACKED_ROWS = 2 * SUBLANES
MXU_COLS = 256
VMEM_LIMIT = 48 * 1024 * 1024

_NT = (((1,), (1,)), ((), ()))


def _lambda_init(layer):
    return 0.8 - 0.6 * math.exp(-0.3 * layer)


def _rms(xf, g):
    return xf * lax.rsqrt(jnp.mean(xf * xf, axis=-1, keepdims=True) + EPS) * g


def _sigmoid(x):
    return 1.0 / (1.0 + jnp.exp(-x))


def _in_proj_kernel(x_ref, g_ref, w_ref, left_ref, cw_ref, cb_ref, lg_ref, lb_ref,
                    q_ref, k_ref, v_ref, kb_ref, vb_ref, conv_ref, tail_ref, abuf, *, ts, c):
    s = pl.program_id(1)
    h = _rms(x_ref[0], g_ref[...]).astype(jnp.bfloat16)

    @pl.when(s == 0)
    def _():
        abuf[0, 0:CONV_PAD, :] = left_ref[0]

    glu_in = jnp.dot(h, w_ref[:, 0:2 * c], preferred_element_type=jnp.float32)
    abuf[0, CONV_PAD:CONV_PAD + ts, :] = glu_in[:, :c] * _sigmoid(glu_in[:, c:])
    q = jnp.dot(h, w_ref[:, 2 * c:3 * c], preferred_element_type=jnp.float32)
    q_ref[0] = (q * (HEAD_DIM ** -0.5)).astype(jnp.bfloat16)
    k = jnp.dot(h, w_ref[:, 3 * c:4 * c], preferred_element_type=jnp.float32)
    k_ref[0] = k
    kb_ref[0] = k.astype(jnp.bfloat16)
    v = jnp.dot(h, w_ref[:, 4 * c:5 * c], preferred_element_type=jnp.float32)
    v_ref[0] = v
    vb_ref[0] = v.astype(jnp.bfloat16)

    n_sh = ts + CONV_PAD - SUBLANES
    for sh in range(1, SUBLANES):
        abuf[sh, 0:n_sh, :] = abuf[0, sh:sh + n_sh, :]
    rc = min(ts, 64)
    for r0 in range(0, ts, rc):
        acc = jnp.zeros((rc, c), jnp.float32)
        for w in range(CONV_WIDTH):
            off = r0 + CONV_PAD - (CONV_WIDTH - 1) + w
            sh = off % SUBLANES
            acc = acc + abuf[sh, off - sh:off - sh + rc, :] * cw_ref[w:w + 1, :]
        y = acc + cb_ref[...]
        mu = jnp.mean(y, axis=-1, keepdims=True)
        d = y - mu
        var = jnp.mean(d * d, axis=-1, keepdims=True)
        yn = d * lax.rsqrt(var + EPS) * lg_ref[...] + lb_ref[...]
        conv_ref[0, r0:r0 + rc, :] = (yn * _sigmoid(yn)).astype(jnp.bfloat16)

    tail = abuf[0, ts:ts + CONV_PAD, :]
    tail_ref[0] = tail
    abuf[0, 0:CONV_PAD, :] = tail


def _in_proj(x, g_mix, w_in_bf, left, conv_w, conv_b, ln_g, ln_b):
    b, s, d = x.shape
    c = conv_w.shape[1]
    ts = min(s, 512)
    assert s % ts == 0 and ts >= CONV_PAD and ts % SUBLANES == 0
    cw = jnp.pad(conv_w, ((0, CONV_PAD - CONV_WIDTH), (0, 0)))
    row = lambda a: a.reshape(1, -1)
    tok = lambda bi, si: (bi, si, 0)
    const2 = lambda bi, si: (0, 0)
    f32, bf16 = jnp.float32, jnp.bfloat16
    outs = pl.pallas_call(
        functools.partial(_in_proj_kernel, ts=ts, c=c),
        grid=(b, s // ts),
        in_specs=[
            pl.BlockSpec((1, ts, d), tok),
            pl.BlockSpec((1, d), const2),
            pl.BlockSpec(w_in_bf.shape, const2),
            pl.BlockSpec((1, CONV_PAD, c), lambda bi, si: (bi, 0, 0)),
            pl.BlockSpec((CONV_PAD, c), const2),
            pl.BlockSpec((1, c), const2),
            pl.BlockSpec((1, c), const2),
            pl.BlockSpec((1, c), const2),
        ],
        out_specs=[pl.BlockSpec((1, ts, c), tok)] * 6
        + [pl.BlockSpec((1, CONV_PAD, c), lambda bi, si: (bi, 0, 0))],
        out_shape=[
            jax.ShapeDtypeStruct((b, s, c), bf16),
            jax.ShapeDtypeStruct((b, s, c), f32),
            jax.ShapeDtypeStruct((b, s, c), f32),
            jax.ShapeDtypeStruct((b, s, c), bf16),
            jax.ShapeDtypeStruct((b, s, c), bf16),
            jax.ShapeDtypeStruct((b, s, c), bf16),
            jax.ShapeDtypeStruct((b, CONV_PAD, c), f32),
        ],
        scratch_shapes=[pltpu.VMEM((SUBLANES, ts + CONV_PAD, c), f32)],
        compiler_params=pltpu.CompilerParams(
            dimension_semantics=("arbitrary", "arbitrary"), vmem_limit_bytes=VMEM_LIMIT),
        name="in_proj",
    )(x, row(g_mix), w_in_bf, left, cw, row(conv_b), row(ln_g), row(ln_b))
    return outs


def _rel_bucket(rel):
    nb = N_BUCKETS // 2
    max_exact = nb // 2
    ret = jnp.where(rel > 0, nb, 0)
    n = jnp.abs(rel)
    nf = jnp.maximum(n, 1).astype(jnp.float32)
    large = max_exact + (jnp.log(nf / max_exact) / math.log(MAX_DISTANCE / max_exact)
                         * (nb - max_exact)).astype(jnp.int32)
    large = jnp.minimum(large, nb - 1)
    return ret + jnp.where(n < max_exact, n, large)


def _bias_table(rel_bias, q_pos, k_pos, masked):
    nq, nk = q_pos.shape[0], k_pos.shape[0]
    period = nq + nk
    m = jnp.arange(period, dtype=jnp.int32)
    bucket = _rel_bucket(k_pos[0] - q_pos[0] + jnp.where(m < nk, m, m - period))
    table = rel_bias.astype(jnp.float32).T[:, None, :]
    hit = bucket[None, :, None] == jnp.arange(N_BUCKETS, dtype=jnp.int32)
    line = jnp.sum(jnp.where(hit, table, 0.0), axis=-1)
    bias = jnp.tile(line, (1, nq))[:, :nq * (period - 1)].reshape(-1, nq, period - 1)[:, :, :nk]
    if masked:
        mask = (k_pos[None, :] // CHUNK) <= (q_pos[:, None] // CHUNK)
        bias = jnp.where(mask[None], bias, NEG)
    return bias


def _split_maps(q):
    lane = lax.broadcasted_iota(jnp.int32, q.shape, 1)
    zero = jnp.zeros_like(q)
    return jnp.where(lane < HEAD_DIM, q, zero), jnp.where(lane >= HEAD_DIM, q, zero)


def _softmax_rows(s):
    p = jnp.exp(s - jnp.max(s, axis=-1, keepdims=True))
    return p / jnp.sum(p, axis=-1, keepdims=True)


def _attn_finish(s1, s2, vv, lam, g, out_scale):
    attn = (_softmax_rows(s1) - lam * _softmax_rows(s2)).astype(jnp.bfloat16)
    o = jnp.dot(attn, vv, preferred_element_type=jnp.float32)
    return (_rms(o, g) * out_scale).astype(jnp.bfloat16)


def _attn_prompt_kernel(lam_ref, q_ref, k_ref, v_ref, slab_ref, g_ref, o_ref,
                        s_ref, m_ref, l_ref, acc_ref, *, tq, out_scale):
    qi = pl.program_id(2)
    q1, q2 = _split_maps(q_ref[0])
    half = tq // 2
    fold = lambda a, op: op(a[:, :half], a[:, half:]) if half % LANES == 0 else a

    m_ref[...] = jnp.full(m_ref.shape, NEG_BIG, jnp.float32)
    l_ref[...] = jnp.zeros_like(l_ref)
    acc_ref[...] = jnp.zeros_like(acc_ref)

    def logits(j, carry):
        kj = k_ref[0, pl.ds(pl.multiple_of(j * tq, tq), tq), :]
        bias = slab_ref[0, jnp.clip(j - qi, -2, 0) + 2]
        for mp, qm in enumerate((q1, q2)):
            a = (lax.dot_general(qm, kj, _NT, preferred_element_type=jnp.float32) + bias) * LOG2_E
            s_ref[mp, j] = a
            m_ref[mp] = jnp.maximum(m_ref[mp], fold(a, jnp.maximum))
        return carry

    lax.fori_loop(0, qi + 1, logits, 0)
    lanes = min(tq, LANES)
    row_max = [jnp.broadcast_to(jnp.max(m_ref[mp], axis=-1, keepdims=True), (tq, lanes)) for mp in range(2)]

    def accumulate(j, carry):
        vj = v_ref[0, pl.ds(pl.multiple_of(j * tq, tq), tq), :]
        for mp in range(2):
            p = jnp.concatenate([jnp.exp2(s_ref[mp, j, :, c * lanes:(c + 1) * lanes] - row_max[mp])
                                 for c in range(tq // lanes)], axis=1)
            l_ref[mp] += fold(p, jnp.add)
            acc_ref[mp] += jnp.dot(p.astype(jnp.bfloat16), vj, preferred_element_type=jnp.float32)
        return carry

    lax.fori_loop(0, qi + 1, accumulate, 0)
    inv = [1.0 / jnp.sum(l_ref[mp], axis=-1, keepdims=True) for mp in range(2)]
    o = acc_ref[0] * inv[0] - lam_ref[0] * (acc_ref[1] * inv[1])
    o_ref[0] = (_rms(o, g_ref[...]) * out_scale).astype(jnp.bfloat16)


def _attn_prompt(q, kb, vb, rel_bias, lam, subln_g, out_scale):
    b, s, c = q.shape
    tq = min(s, 512)
    assert s % tq == 0 and tq % CHUNK == 0 and tq >= MAX_DISTANCE and c == N_HEADS * V_DIM
    n_kb = s // tq
    pos = jnp.arange(tq, dtype=jnp.int32)
    slabs = jnp.stack([
        _bias_table(rel_bias, pos + 2 * tq, pos, False),
        _bias_table(rel_bias, pos + tq, pos, False),
        _bias_table(rel_bias, pos, pos, True),
    ], axis=1)
    width = tq // 2 if (tq // 2) % LANES == 0 else tq
    return pl.pallas_call(
        functools.partial(_attn_prompt_kernel, tq=tq, out_scale=out_scale),
        grid=(b, N_HEADS, n_kb),
        in_specs=[
            pl.BlockSpec(memory_space=pltpu.SMEM),
            pl.BlockSpec((1, tq, V_DIM), lambda bi, hi, qi: (bi, qi, hi)),
            pl.BlockSpec((1, s, V_DIM), lambda bi, hi, qi: (bi, 0, hi)),
            pl.BlockSpec((1, s, V_DIM), lambda bi, hi, qi: (bi, 0, hi)),
            pl.BlockSpec((1, 3, tq, tq), lambda bi, hi, qi: (hi, 0, 0, 0)),
            pl.BlockSpec((1, V_DIM), lambda bi, hi, qi: (0, 0)),
        ],
        out_specs=pl.BlockSpec((1, tq, V_DIM), lambda bi, hi, qi: (bi, qi, hi)),
        out_shape=jax.ShapeDtypeStruct((b, s, c), jnp.bfloat16),
        scratch_shapes=[
            pltpu.VMEM((2, n_kb, tq, tq), jnp.float32),
            pltpu.VMEM((2, tq, width), jnp.float32),
            pltpu.VMEM((2, tq, width), jnp.float32),
            pltpu.VMEM((2, tq, V_DIM), jnp.float32),
        ],
        compiler_params=pltpu.CompilerParams(
            dimension_semantics=("arbitrary",) * 3, vmem_limit_bytes=VMEM_LIMIT),
        name="attn_prompt",
    )(lam, q, kb, vb, slabs, subln_g.reshape(1, -1))


def _attn_sample_kernel(lam_ref, q_ref, ck_ref, cv_ref, kn_ref, vn_ref, bc_ref, bn_ref, g_ref, o_ref,
                        *, out_scale):
    q1, q2 = _split_maps(q_ref[0])
    ck = ck_ref[0].astype(jnp.bfloat16)
    kn = kn_ref[0]
    logits = lambda qm: jnp.concatenate([
        lax.dot_general(qm, ck, _NT, preferred_element_type=jnp.float32) + bc_ref[0],
        lax.dot_general(qm, kn, _NT, preferred_element_type=jnp.float32) + bn_ref[0]], axis=1)
    vv = jnp.concatenate([cv_ref[0].astype(jnp.bfloat16), vn_ref[0]], axis=0)
    o_ref[0] = _attn_finish(logits(q1), logits(q2), vv, lam_ref[0], g_ref[...], out_scale)


def _attn_sample(q, cache_k, cache_v, k_new, v_new, bias_c, bias_n, lam, subln_g, out_scale):
    b, sq, c = q.shape
    past, n_new = cache_k.shape[1], k_new.shape[1]
    per_head = lambda rows: pl.BlockSpec((1, rows, V_DIM), lambda bi, hi: (bi, 0, hi))
    return pl.pallas_call(
        functools.partial(_attn_sample_kernel, out_scale=out_scale),
        grid=(b, N_HEADS),
        in_specs=[
            pl.BlockSpec(memory_space=pltpu.SMEM),
            per_head(sq), per_head(past), per_head(past), per_head(n_new), per_head(n_new),
            pl.BlockSpec((1, sq, past), lambda bi, hi: (hi, 0, 0)),
            pl.BlockSpec((1, sq, n_new), lambda bi, hi: (hi, 0, 0)),
            pl.BlockSpec((1, V_DIM), lambda bi, hi: (0, 0)),
        ],
        out_specs=per_head(sq),
        out_shape=jax.ShapeDtypeStruct((b, sq, c), jnp.bfloat16),
        compiler_params=pltpu.CompilerParams(
            dimension_semantics=("arbitrary",) * 2, vmem_limit_bytes=VMEM_LIMIT),
        name="attn_sample",
    )(lam, q, cache_k, cache_v, k_new, v_new, bias_c, bias_n, subln_g.reshape(1, -1))


def _mid_kernel(conv_ref, att_ref, x_ref, wc_ref, wa_ref, g_ref, wq_ref, sk_ref,
                x1_ref, h2_ref, st_ref):
    x1 = (x_ref[...]
          + jnp.dot(conv_ref[...], wc_ref[...], preferred_element_type=jnp.float32)
          + jnp.dot(att_ref[...], wa_ref[...], preferred_element_type=jnp.float32))
    x1_ref[...] = x1
    h2 = _rms(x1, g_ref[...]).astype(jnp.bfloat16)
    h2_ref[...] = h2
    qq = jnp.dot(h2, wq_ref[...], preferred_element_type=jnp.float32).astype(jnp.bfloat16)
    for rp in range(2 * R_HEADS):
        st_ref[rp] = lax.dot_general(sk_ref[rp], qq[:, rp * N_KEYS:(rp + 1) * N_KEYS], _NT,
                                     preferred_element_type=jnp.float32)


def _mid(conv, att, x2d, w_out_bf, g_ffn, w_query_bf, sub_keys_bf):
    t, d = x2d.shape
    c = conv.shape[1]
    tb = min(t, 512)
    assert t % tb == 0
    dq = w_query_bf.shape[1]
    nrp = sub_keys_bf.shape[0]
    tok = lambda i: (i, 0)
    const = lambda i: (0, 0)
    return pl.pallas_call(
        _mid_kernel,
        grid=(t // tb,),
        in_specs=[
            pl.BlockSpec((tb, c), tok),
            pl.BlockSpec((tb, c), tok),
            pl.BlockSpec((tb, d), tok),
            pl.BlockSpec((c, d), const),
            pl.BlockSpec((c, d), lambda i: (1, 0)),
            pl.BlockSpec((1, d), const),
            pl.BlockSpec((d, dq), const),
            pl.BlockSpec(sub_keys_bf.shape, lambda i: (0, 0, 0)),
        ],
        out_specs=[
            pl.BlockSpec((tb, d), tok),
            pl.BlockSpec((tb, d), tok),
            pl.BlockSpec((nrp, N_KEYS, tb), lambda i: (0, 0, i)),
        ],
        out_shape=[
            jax.ShapeDtypeStruct((t, d), jnp.float32),
            jax.ShapeDtypeStruct((t, d), jnp.bfloat16),
            jax.ShapeDtypeStruct((nrp, N_KEYS, t), jnp.float32),
        ],
        compiler_params=pltpu.CompilerParams(
            dimension_semantics=("arbitrary",), vmem_limit_bytes=VMEM_LIMIT),
        name="mid",
    )(conv, att, x2d, w_out_bf, w_out_bf, g_ffn.reshape(1, -1), w_query_bf, sub_keys_bf)


def _ce(a, b):
    if a is None:
        return b, None
    if b is None:
        return a, None
    return jnp.maximum(a, b), jnp.minimum(a, b)


def _sort_desc(xs):
    xs = list(xs)
    n = len(xs)
    p = 1
    while p < n:
        k = p
        while k >= 1:
            for j in range(k % p, n - k, 2 * k):
                for i in range(min(k, n - j - k)):
                    if (i + j) // (2 * p) == (i + j + k) // (2 * p):
                        xs[i + j], xs[i + j + k] = _ce(xs[i + j], xs[i + j + k])
            k //= 2
        p *= 2
    return xs


def _bitonic_top(a, b):
    n = len(a)
    return [_ce(a[i], b[n - 1 - i])[0] for i in range(n)]


def _bitonic_sort_desc(xs):
    xs = list(xs)
    n = len(xs)
    d = n // 2
    while d >= 1:
        for i in range(n):
            if i & d == 0:
                xs[i], xs[i + d] = _ce(xs[i], xs[i + d])
        d //= 2
    return xs


def _fill(xs):
    return [jnp.full((SUBLANES, LANES), NEG_BIG, jnp.float32) if x is None else x for x in xs]


def _sublane_merge_sorted(xs):
    for shift in (4, 2, 1):
        other = [pltpu.roll(x, shift, 0) for x in xs]
        xs = _bitonic_sort_desc(_bitonic_top(xs, other))
    return xs


def _sublane_merge_kth(xs):
    for shift in (4, 2):
        other = [pltpu.roll(x, shift, 0) for x in xs]
        xs = _bitonic_sort_desc(_bitonic_top(xs, other))
    other = [pltpu.roll(x, 1, 0) for x in xs]
    top = _bitonic_top(xs, other)
    return functools.reduce(jnp.minimum, top)


def _top16_rows(s):
    tiles = [s[i * SUBLANES:(i + 1) * SUBLANES, :] for i in range(N_KEYS // SUBLANES)]
    return _sublane_merge_sorted(_sort_desc(tiles))


def _dup_bf16(x):
    b = pltpu.bitcast(x.astype(jnp.bfloat16).astype(jnp.float32), jnp.uint32)
    return b | (b >> 16)


def _route_kernel(st_ref, rank_ref, n_ref, ea_ref, eb_ref, *, tb):
    sub = lax.broadcasted_iota(jnp.int32, (SUBLANES, LANES), 0)
    for g in range(tb // LANES):
        cols = slice(g * LANES, (g + 1) * LANES)
        for r in range(R_HEADS):
            s1 = st_ref[2 * r, :, cols]
            s2 = st_ref[2 * r + 1, :, cols]
            v1 = _top16_rows(s1)
            v2 = _top16_rows(s2)
            pack = lambda v, base: functools.reduce(
                lambda acc, j: jnp.where(sub == j, v[base + j], acc), range(1, SUBLANES), v[base])
            w1a, w1b, w2a, w2b = pack(v1, 0), pack(v1, 8), pack(v2, 0), pack(v2, 8)
            neg = jnp.full((SUBLANES, LANES), NEG_BIG, jnp.float32)
            cands = [
                v1[0] + w2a,
                v1[0] + w2b,
                jnp.where(sub >= 1, v2[0] + w1a, neg),
                v2[0] + w1b,
                jnp.where(sub >= 1, v1[1] + w2a, neg),
                jnp.where(sub >= 2, v2[1] + w1a, neg),
                jnp.where((sub >= 2) & (sub <= 4), v1[2] + w2a, neg),
                jnp.where((sub >= 2) & (sub <= 3), v1[3] + w2a, neg),
                jnp.where(sub == 2, v1[4] + w2a, neg),
            ]
            srt = _fill(_sort_desc(cands + [None] * (TOPK - len(cands))))
            thr = _sublane_merge_kth(srt)
            m1, m2 = v1[0], v2[0]
            top = m1 + m2
            z = functools.reduce(
                lambda acc, cnd: acc + jnp.where(cnd >= thr, jnp.exp(cnd - top), 0.0), cands,
                jnp.zeros((SUBLANES, LANES), jnp.float32))
            for shift in (4, 2, 1):
                z = z + pltpu.roll(z, shift, 0)
            rank2 = jnp.zeros((N_KEYS, LANES), jnp.float32)
            cnt = jnp.zeros((N_KEYS, LANES), jnp.float32)
            thr_row = thr[0:1, :]
            for j in range(TOPK):
                rank2 = jnp.where(v2[j][0:1, :] > s2, j + 1.0, rank2)
            for j in range(TOPK // 2):
                cnt = jnp.where(s1 + v2[j][0:1, :] >= thr_row, j + 1.0, cnt)
            cnt_top = jnp.zeros((SUBLANES, LANES), jnp.float32)
            for j in range(TOPK // 2, TOPK):
                cnt_top = jnp.where(v1[0] + v2[j] >= thr, j + 1.0, cnt_top)
            cnt = jnp.maximum(cnt, jnp.where(s1 == v1[0][0:1, :], cnt_top[0:1, :], 0.0))
            n_ref[r, :, cols] = _dup_bf16(cnt)
            ea_ref[r, :, cols] = _dup_bf16(jnp.exp(s1 - m1[0:1, :]) / z[0:1, :])
            eb_ref[r, :, cols] = jnp.exp(s2 - m2[0:1, :]).astype(jnp.bfloat16)
            rank_ref[r, :, cols] = rank2.astype(jnp.bfloat16)


def _route(st):
    nrp, nk, t = st.shape
    tb = min(t, 256)
    assert t % tb == 0 and tb % LANES == 0 and nk == N_KEYS and nrp == 2 * R_HEADS
    row_spec = pl.BlockSpec((R_HEADS, nk, tb), lambda i: (0, 0, i))
    return pl.pallas_call(
        functools.partial(_route_kernel, tb=tb),
        grid=(t // tb,),
        in_specs=[pl.BlockSpec((nrp, nk, tb), lambda i: (0, 0, i))],
        out_specs=[row_spec] * 4,
        out_shape=[
            jax.ShapeDtypeStruct((R_HEADS, nk, t), jnp.bfloat16),
            jax.ShapeDtypeStruct((R_HEADS, nk, t), jnp.uint32),
            jax.ShapeDtypeStruct((R_HEADS, nk, t), jnp.uint32),
            jax.ShapeDtypeStruct((R_HEADS, nk, t), jnp.bfloat16),
        ],
        compiler_params=pltpu.CompilerParams(
            dimension_semantics=("arbitrary",), vmem_limit_bytes=VMEM_LIMIT),
        name="route",
    )(st)


def _gelu(x):
    hx = 0.5 * x
    return hx + hx * lax.erf(x * (2.0 ** -0.5))


def _packed_row(ref, r, row, cols):
    tile = jnp.broadcast_to(ref[r, row:row + 1, cols], (SUBLANES, LANES))
    return pltpu.bitcast(tile, jnp.bfloat16)


def _peer_kernel(h_ref, u_ref, vt_ref, rank_ref, eb_ref, n_odd_ref, n_even_ref, ea_odd_ref, ea_even_ref,
                 x1_ref, g_ref, y_ref, acc_ref, act_a, act_b, coef_a, coef_b, *, eb_rows, tb):
    s = pl.program_id(1)
    last = pl.num_programs(1) - 1

    @pl.when(s == 0)
    def _():
        acc_ref[...] = jnp.zeros_like(acc_ref)

    n_tg = tb // LANES
    tn = min(tb, MXU_COLS)

    def accumulate(half, coef_ref, piece):
        vt = vt_ref[:, half * eb_rows:(half + 1) * eb_rows]
        cols = slice(piece * tn, (piece + 1) * tn)
        acc_ref[:, cols] += jnp.dot(vt, coef_ref[:, cols], preferred_element_type=jnp.float32)

    def gate(cnt_ref, gain_ref, act_ref, coef_ref, tg):
        zero = jnp.zeros((PACKED_ROWS, LANES), jnp.bfloat16)
        cols = slice(tg * LANES, (tg + 1) * LANES)
        for ci in range(eb_rows // N_KEYS):
            cnt = [_packed_row(cnt_ref, r, ci, cols) for r in range(R_HEADS)]
            ea = [_packed_row(gain_ref, r, ci, cols) for r in range(R_HEADS)]
            for ch in range(N_KEYS // PACKED_ROWS):
                keys = slice(ch * PACKED_ROWS, (ch + 1) * PACKED_ROWS)
                g = zero
                for r in range(R_HEADS):
                    sel = jnp.minimum(jnp.maximum(cnt[r] - rank_ref[r, keys, cols], 0), 1)
                    g = g + (ea[r] * sel) * eb_ref[r, keys, cols]
                rows = slice(ci * N_KEYS + ch * PACKED_ROWS, ci * N_KEYS + (ch + 1) * PACKED_ROWS)
                coef_ref[rows, cols] = g * _gelu(act_ref[rows, cols].astype(jnp.bfloat16))

    def activate(half, act_ref, piece):
        u = u_ref[half * eb_rows:(half + 1) * eb_rows, :]
        cols = slice(piece * tn, (piece + 1) * tn)
        act_ref[:, cols] = lax.dot_general(u, h_ref[cols, :], _NT, preferred_element_type=jnp.float32)

    n_piece = tb // tn

    per = n_tg // n_piece

    halves = ((n_even_ref, ea_even_ref, act_a, coef_a), (n_odd_ref, ea_odd_ref, act_b, coef_b))

    def step(drain, fill):
        for half, (cnt_ref, gain_ref, act_ref, coef_ref) in enumerate(halves):
            for piece in range(n_piece):
                if drain:
                    for tg in range(piece * per, (piece + 1) * per):
                        gate(cnt_ref, gain_ref, act_ref, coef_ref, tg)
                    accumulate(half, coef_ref, piece)
                if fill:
                    activate(half, act_ref, piece)

    pl.when(s == 0)(functools.partial(step, False, True))
    pl.when((s > 0) & (s < last))(functools.partial(step, True, True))
    pl.when(s == last)(functools.partial(step, True, False))

    @pl.when(s == last)
    def _():
        y_ref[...] = _rms(x1_ref[...] + acc_ref[...].T, g_ref[...])


def _peer(h2, u_bf, vt_bf, rank2, cnt, ea, eb, x1, g_final):
    t, d = h2.shape
    n_exp = u_bf.shape[0]
    tb = min(t, 512)
    eb_rows = SUBLANES * N_KEYS
    assert t % tb == 0 and n_exp % (2 * eb_rows) == 0 and n_exp == N_KEYS * N_KEYS
    n_blocks = n_exp // eb_rows
    n_steps = n_blocks // 2 + 1
    clamp = lambda b, hi: jnp.clip(b, 0, hi)
    tok = lambda i, s: (i, 0)
    tile_spec = pl.BlockSpec((R_HEADS, N_KEYS, tb), lambda i, s: (0, 0, i))
    odd_spec = pl.BlockSpec((R_HEADS, SUBLANES, tb), lambda i, s: (0, clamp(2 * s - 1, n_blocks - 1), i))
    even_spec = pl.BlockSpec((R_HEADS, SUBLANES, tb), lambda i, s: (0, clamp(2 * s - 2, n_blocks - 1), i))
    return pl.pallas_call(
        functools.partial(_peer_kernel, eb_rows=eb_rows, tb=tb),
        grid=(t // tb, n_steps),
        in_specs=[
            pl.BlockSpec((tb, d), tok),
            pl.BlockSpec((2 * eb_rows, d), lambda i, s: (clamp(s, n_steps - 2), 0)),
            pl.BlockSpec((d, 2 * eb_rows), lambda i, s: (0, clamp(s - 1, n_steps - 2))),
            tile_spec, tile_spec, odd_spec, even_spec, odd_spec, even_spec,
            pl.BlockSpec((tb, d), tok),
            pl.BlockSpec((1, d), lambda i, s: (0, 0)),
        ],
        out_specs=pl.BlockSpec((tb, d), tok),
        out_shape=jax.ShapeDtypeStruct((t, d), jnp.float32),
        scratch_shapes=[
            pltpu.VMEM((d, tb), jnp.float32),
            pltpu.VMEM((eb_rows, tb), jnp.float32),
            pltpu.VMEM((eb_rows, tb), jnp.float32),
            pltpu.VMEM((eb_rows, tb), jnp.bfloat16),
            pltpu.VMEM((eb_rows, tb), jnp.bfloat16),
        ],
        compiler_params=pltpu.CompilerParams(
            dimension_semantics=("arbitrary", "arbitrary"), vmem_limit_bytes=VMEM_LIMIT),
        name="peer",
    )(h2, u_bf, vt_bf, rank2, eb, cnt, cnt, ea, ea, x1, g_final.reshape(1, -1))


def _stream(x, left, attend, p):
    b, s, d = x.shape
    q, k, v, kb, vb, conv, tail = _in_proj(x, p["g_mix"], p["w_in"], left, p["conv_w"], p["conv_b"],
                                           p["ln_g"], p["ln_b"])
    att = attend(q, kb, vb)
    c = conv.shape[-1]
    x1, h2, st = _mid(conv.reshape(b * s, c), att.reshape(b * s, c), x.reshape(b * s, d),
                      p["w_out"], p["g_ffn"], p["w_query"], p["sub_keys"])
    rank2, cnt, ea, eb = _route(st)
    y = _peer(h2, p["peer_u"], p["peer_vt"], rank2, cnt, ea, eb, x1, p["g_final"])
    k = k.reshape(b, s, N_HEADS, 2, HEAD_DIM)
    v = v.reshape(b, s, N_HEADS, V_DIM)
    return y.reshape(b, s, d), k, v, tail[:, CONV_PAD - (CONV_WIDTH - 1):]


def kernel(x_prompt, x_sample, cache_k, cache_v, state_conv, g_mix, w_in, conv_w, conv_b, conv_ln_g, conv_ln_b, lambda_q1, lambda_k1, lambda_q2, lambda_k2, subln_g, rel_bias, w_out, g_ffn, w_query, sub_keys, peer_u, peer_v, g_final):
    depth = w_in.shape[0]
    assert depth == 1, "single-layer step"
    l = 0
    bf16 = jnp.bfloat16
    b, s, d = x_prompt.shape
    bd, sd, _ = x_sample.shape
    past = cache_k.shape[2]
    c = conv_w.shape[-1]

    lam_init = _lambda_init(l)
    lam = (jnp.exp(jnp.sum(lambda_q1[l].astype(jnp.float32) * lambda_k1[l].astype(jnp.float32)))
           - jnp.exp(jnp.sum(lambda_q2[l].astype(jnp.float32) * lambda_k2[l].astype(jnp.float32)))
           + lam_init).reshape(1)
    out_scale = 1.0 - lam_init

    p = {
        "g_mix": g_mix[l], "w_in": w_in[l].astype(bf16), "conv_w": conv_w[l], "conv_b": conv_b[l],
        "ln_g": conv_ln_g[l], "ln_b": conv_ln_b[l], "w_out": w_out[l].astype(bf16), "g_ffn": g_ffn[l],
        "w_query": w_query[l].astype(bf16),
        "sub_keys": sub_keys[l].reshape(2 * R_HEADS, N_KEYS, -1).astype(bf16),
        "peer_u": peer_u[l].astype(bf16), "peer_vt": peer_v[l].astype(bf16).T, "g_final": g_final,
    }

    attend_p = lambda q, kb, vb: _attn_prompt(q, kb, vb, rel_bias, lam, subln_g[l], out_scale)
    y_p, k_p, v_p, tail_p = _stream(x_prompt, jnp.zeros((b, CONV_PAD, c), jnp.float32), attend_p, p)

    n_new = -(-sd // LANES) * LANES
    pos_s = past + jnp.arange(sd, dtype=jnp.int32)
    bias_c = _bias_table(rel_bias, pos_s, jnp.arange(past, dtype=jnp.int32), True)
    bias_n = _bias_table(rel_bias, pos_s, past + jnp.arange(n_new, dtype=jnp.int32), True)
    bias_n = jnp.where(jnp.arange(n_new) < sd, bias_n, NEG)
    ck = cache_k[l].reshape(bd, past, c)
    cv = cache_v[l].reshape(bd, past, c)

    def attend_s(q, kb, vb):
        pad = ((0, 0), (0, n_new - sd), (0, 0))
        return _attn_sample(q, ck, cv, jnp.pad(kb, pad), jnp.pad(vb, pad), bias_c, bias_n,
                            lam, subln_g[l], out_scale)

    left_s = jnp.pad(state_conv[l], ((0, 0), (CONV_PAD - (CONV_WIDTH - 1), 0), (0, 0)))
    y_s, k_s, v_s, tail_s = _stream(x_sample, left_s, attend_s, p)

    return (y_p, y_s, k_p[None], v_p[None], tail_p[None], k_s[None], v_s[None], tail_s[None])
```

```python
import functools
import math

import jax
import jax.numpy as jnp
from jax import lax
from jax.experimental import pallas as pl
from jax.experimental.pallas import tpu as pltpu

CHUNK = 64
CONV_WIDTH = 31
CONV_PAD = 32
N_HEADS = 4
HEAD_DIM = 64
V_DIM = 2 * HEAD_DIM
N_BUCKETS = 32
MAX_DISTANCE = 128
N_KEYS = 128
R_HEADS = 8
TOPK = 16
EPS = 1e-6
NEG = -1e30
NEG_BIG = -3.0e38
LOG2_E = 1.4426950408889634
LANES = 128
SUBLANES = 8
PACKED_ROWS = 2 * SUBLANES
MXU_COLS = 256
VMEM_LIMIT = 48 * 1024 * 1024

_NT = (((1,), (1,)), ((), ()))


def _lambda_init(layer):
    return 0.8 - 0.6 * math.exp(-0.3 * layer)


def _rms(xf, g):
    return xf * lax.rsqrt(jnp.mean(xf * xf, axis=-1, keepdims=True) + EPS) * g


def _sigmoid(x):
    return 1.0 / (1.0 + jnp.exp(-x))


def _in_proj_kernel(x_ref, g_ref, w_ref, left_ref, cw_ref, cb_ref, lg_ref, lb_ref,
                    q_ref, k_ref, v_ref, kb_ref, vb_ref, conv_ref, tail_ref, abuf, *, ts, c):
    s = pl.program_id(1)
    h = _rms(x_ref[0], g_ref[...]).astype(jnp.bfloat16)

    @pl.when(s == 0)
    def _():
        abuf[0, 0:CONV_PAD, :] = left_ref[0]

    glu_in = jnp.dot(h, w_ref[:, 0:2 * c], preferred_element_type=jnp.float32)
    abuf[0, CONV_PAD:CONV_PAD + ts, :] = glu_in[:, :c] * _sigmoid(glu_in[:, c:])
    q = jnp.dot(h, w_ref[:, 2 * c:3 * c], preferred_element_type=jnp.float32)
    q_ref[0] = (q * (HEAD_DIM ** -0.5)).astype(jnp.bfloat16)
    k = jnp.dot(h, w_ref[:, 3 * c:4 * c], preferred_element_type=jnp.float32)
    k_ref[0] = k
    kb_ref[0] = k.astype(jnp.bfloat16)
    v = jnp.dot(h, w_ref[:, 4 * c:5 * c], preferred_element_type=jnp.float32)
    v_ref[0] = v
    vb_ref[0] = v.astype(jnp.bfloat16)

    n_sh = ts + CONV_PAD - SUBLANES
    for sh in range(1, SUBLANES):
        abuf[sh, 0:n_sh, :] = abuf[0, sh:sh + n_sh, :]
    rc = min(ts, 64)
    for r0 in range(0, ts, rc):
        acc = jnp.zeros((rc, c), jnp.float32)
        for w in range(CONV_WIDTH):
            off = r0 + CONV_PAD - (CONV_WIDTH - 1) + w
            sh = off % SUBLANES
            acc = acc + abuf[sh, off - sh:off - sh + rc, :] * cw_ref[w:w + 1, :]
        y = acc + cb_ref[...]
        mu = jnp.mean(y, axis=-1, keepdims=True)
        d = y - mu
        var = jnp.mean(d * d, axis=-1, keepdims=True)
        yn = d * lax.rsqrt(var + EPS) * lg_ref[...] + lb_ref[...]
        conv_ref[0, r0:r0 + rc, :] = (yn * _sigmoid(yn)).astype(jnp.bfloat16)

    tail = abuf[0, ts:ts + CONV_PAD, :]
    tail_ref[0] = tail
    abuf[0, 0:CONV_PAD, :] = tail


def _in_proj(x, g_mix, w_in_bf, left, conv_w, conv_b, ln_g, ln_b):
    b, s, d = x.shape
    c = conv_w.shape[1]
    ts = min(s, 512)
    assert s % ts == 0 and ts >= CONV_PAD and ts % SUBLANES == 0
    cw = jnp.pad(conv_w, ((0, CONV_PAD - CONV_WIDTH), (0, 0)))
    row = lambda a: a.reshape(1, -1)
    tok = lambda bi, si: (bi, si, 0)
    const2 = lambda bi, si: (0, 0)
    f32, bf16 = jnp.float32, jnp.bfloat16
    outs = pl.pallas_call(
        functools.partial(_in_proj_kernel, ts=ts, c=c),
        grid=(b, s // ts),
        in_specs=[
            pl.BlockSpec((1, ts, d), tok),
            pl.BlockSpec((1, d), const2),
            pl.BlockSpec(w_in_bf.shape, const2),
            pl.BlockSpec((1, CONV_PAD, c), lambda bi, si: (bi, 0, 0)),
            pl.BlockSpec((CONV_PAD, c), const2),
            pl.BlockSpec((1, c), const2),
            pl.BlockSpec((1, c), const2),
            pl.BlockSpec((1, c), const2),
        ],
        out_specs=[pl.BlockSpec((1, ts, c), tok)] * 6
        + [pl.BlockSpec((1, CONV_PAD, c), lambda bi, si: (bi, 0, 0))],
        out_shape=[
            jax.ShapeDtypeStruct((b, s, c), bf16),
            jax.ShapeDtypeStruct((b, s, c), f32),
            jax.ShapeDtypeStruct((b, s, c), f32),
            jax.ShapeDtypeStruct((b, s, c), bf16),
            jax.ShapeDtypeStruct((b, s, c), bf16),
            jax.ShapeDtypeStruct((b, s, c), bf16),
            jax.ShapeDtypeStruct((b, CONV_PAD, c), f32),
        ],
        scratch_shapes=[pltpu.VMEM((SUBLANES, ts + CONV_PAD, c), f32)],
        compiler_params=pltpu.CompilerParams(
            dimension_semantics=("arbitrary", "arbitrary"), vmem_limit_bytes=VMEM_LIMIT),
        name="in_proj",
    )(x, row(g_mix), w_in_bf, left, cw, row(conv_b), row(ln_g), row(ln_b))
    return outs


def _rel_bucket(rel):
    nb = N_BUCKETS // 2
    max_exact = nb // 2
    ret = jnp.where(rel > 0, nb, 0)
    n = jnp.abs(rel)
    nf = jnp.maximum(n, 1).astype(jnp.float32)
    large = max_exact + (jnp.log(nf / max_exact) / math.log(MAX_DISTANCE / max_exact)
                         * (nb - max_exact)).astype(jnp.int32)
    large = jnp.minimum(large, nb - 1)
    return ret + jnp.where(n < max_exact, n, large)


def _bias_table(rel_bias, q_pos, k_pos, masked):
    nq, nk = q_pos.shape[0], k_pos.shape[0]
    period = nq + nk
    m = jnp.arange(period, dtype=jnp.int32)
    bucket = _rel_bucket(k_pos[0] - q_pos[0] + jnp.where(m < nk, m, m - period))
    table = rel_bias.astype(jnp.float32).T[:, None, :]
    hit = bucket[None, :, None] == jnp.arange(N_BUCKETS, dtype=jnp.int32)
    line = jnp.sum(jnp.where(hit, table, 0.0), axis=-1)
    bias = jnp.tile(line, (1, nq))[:, :nq * (period - 1)].reshape(-1, nq, period - 1)[:, :, :nk]
    if masked:
        mask = (k_pos[None, :] // CHUNK) <= (q_pos[:, None] // CHUNK)
        bias = jnp.where(mask[None], bias, NEG)
    return bias


def _split_maps(q):
    lane = lax.broadcasted_iota(jnp.int32, q.shape, 1)
    zero = jnp.zeros_like(q)
    return jnp.where(lane < HEAD_DIM, q, zero), jnp.where(lane >= HEAD_DIM, q, zero)


def _softmax_rows(s):
    p = jnp.exp(s - jnp.max(s, axis=-1, keepdims=True))
    return p / jnp.sum(p, axis=-1, keepdims=True)


def _attn_finish(s1, s2, vv, lam, g, out_scale):
    attn = (_softmax_rows(s1) - lam * _softmax_rows(s2)).astype(jnp.bfloat16)
    o = jnp.dot(attn, vv, preferred_element_type=jnp.float32)
    return (_rms(o, g) * out_scale).astype(jnp.bfloat16)


def _attn_prompt_kernel(lam_ref, q_ref, k_ref, v_ref, slab_ref, g_ref, o_ref,
                        s_ref, m_ref, l_ref, acc_ref, *, tq, out_scale):
    qi = pl.program_id(2)
    q1, q2 = _split_maps(q_ref[0])
    half = tq // 2
    fold = lambda a, op: op(a[:, :half], a[:, half:]) if half % LANES == 0 else a

    m_ref[...] = jnp.full(m_ref.shape, NEG_BIG, jnp.float32)
    l_ref[...] = jnp.zeros_like(l_ref)
    acc_ref[...] = jnp.zeros_like(acc_ref)

    def logits(j, carry):
        kj = k_ref[0, pl.ds(pl.multiple_of(j * tq, tq), tq), :]
        bias = slab_ref[0, jnp.clip(j - qi, -2, 0) + 2]
        for mp, qm in enumerate((q1, q2)):
            a = (lax.dot_general(qm, kj, _NT, preferred_element_type=jnp.float32) + bias) * LOG2_E
            s_ref[mp, j] = a
            m_ref[mp] = jnp.maximum(m_ref[mp], fold(a, jnp.maximum))
        return carry

    lax.fori_loop(0, qi + 1, logits, 0)
    lanes = min(tq, LANES)
    row_max = [jnp.broadcast_to(jnp.max(m_ref[mp], axis=-1, keepdims=True), (tq, lanes)) for mp in range(2)]

    def accumulate(j, carry):
        vj = v_ref[0, pl.ds(pl.multiple_of(j * tq, tq), tq), :]
        for mp in range(2):
            p = jnp.concatenate([jnp.exp2(s_ref[mp, j, :, c * lanes:(c + 1) * lanes] - row_max[mp])
                                 for c in range(tq // lanes)], axis=1)
            l_ref[mp] += fold(p, jnp.add)
            acc_ref[mp] += jnp.dot(p.astype(jnp.bfloat16), vj, preferred_element_type=jnp.float32)
        return carry

    lax.fori_loop(0, qi + 1, accumulate, 0)
    inv = [1.0 / jnp.sum(l_ref[mp], axis=-1, keepdims=True) for mp in range(2)]
    o = acc_ref[0] * inv[0] - lam_ref[0] * (acc_ref[1] * inv[1])
    o_ref[0] = (_rms(o, g_ref[...]) * out_scale).astype(jnp.bfloat16)


def _attn_prompt(q, kb, vb, rel_bias, lam, subln_g, out_scale):
    b, s, c = q.shape
    tq = min(s, 512)
    assert s % tq == 0 and tq % CHUNK == 0 and tq >= MAX_DISTANCE and c == N_HEADS * V_DIM
    n_kb = s // tq
    pos = jnp.arange(tq, dtype=jnp.int32)
    slabs = jnp.stack([
        _bias_table(rel_bias, pos + 2 * tq, pos, False),
        _bias_table(rel_bias, pos + tq, pos, False),
        _bias_table(rel_bias, pos, pos, True),
    ], axis=1)
    width = tq // 2 if (tq // 2) % LANES == 0 else tq
    return pl.pallas_call(
        functools.partial(_attn_prompt_kernel, tq=tq, out_scale=out_scale),
        grid=(b, N_HEADS, n_kb),
        in_specs=[
            pl.BlockSpec(memory_space=pltpu.SMEM),
            pl.BlockSpec((1, tq, V_DIM), lambda bi, hi, qi: (bi, qi, hi)),
            pl.BlockSpec((1, s, V_DIM), lambda bi, hi, qi: (bi, 0, hi)),
            pl.BlockSpec((1, s, V_DIM), lambda bi, hi, qi: (bi, 0, hi)),
            pl.BlockSpec((1, 3, tq, tq), lambda bi, hi, qi: (hi, 0, 0, 0)),
            pl.BlockSpec((1, V_DIM), lambda bi, hi, qi: (0, 0)),
        ],
        out_specs=pl.BlockSpec((1, tq, V_DIM), lambda bi, hi, qi: (bi, qi, hi)),
        out_shape=jax.ShapeDtypeStruct((b, s, c), jnp.bfloat16),
        scratch_shapes=[
            pltpu.VMEM((2, n_kb, tq, tq), jnp.float32),
            pltpu.VMEM((2, tq, width), jnp.float32),
            pltpu.VMEM((2, tq, width), jnp.float32),
            pltpu.VMEM((2, tq, V_DIM), jnp.float32),
        ],
        compiler_params=pltpu.CompilerParams(
            dimension_semantics=("arbitrary",) * 3, vmem_limit_bytes=VMEM_LIMIT),
        name="attn_prompt",
    )(lam, q, kb, vb, slabs, subln_g.reshape(1, -1))


def _attn_sample_kernel(lam_ref, q_ref, ck_ref, cv_ref, kn_ref, vn_ref, bc_ref, bn_ref, g_ref, o_ref,
                        *, out_scale):
    q1, q2 = _split_maps(q_ref[0])
    ck = ck_ref[0].astype(jnp.bfloat16)
    kn = kn_ref[0]
    logits = lambda qm: jnp.concatenate([
        lax.dot_general(qm, ck, _NT, preferred_element_type=jnp.float32) + bc_ref[0],
        lax.dot_general(qm, kn, _NT, preferred_element_type=jnp.float32) + bn_ref[0]], axis=1)
    vv = jnp.concatenate([cv_ref[0].astype(jnp.bfloat16), vn_ref[0]], axis=0)
    o_ref[0] = _attn_finish(logits(q1), logits(q2), vv, lam_ref[0], g_ref[...], out_scale)


def _attn_sample(q, cache_k, cache_v, k_new, v_new, bias_c, bias_n, lam, subln_g, out_scale):
    b, sq, c = q.shape
    past, n_new = cache_k.shape[1], k_new.shape[1]
    per_head = lambda rows: pl.BlockSpec((1, rows, V_DIM), lambda bi, hi: (bi, 0, hi))
    return pl.pallas_call(
        functools.partial(_attn_sample_kernel, out_scale=out_scale),
        grid=(b, N_HEADS),
        in_specs=[
            pl.BlockSpec(memory_space=pltpu.SMEM),
            per_head(sq), per_head(past), per_head(past), per_head(n_new), per_head(n_new),
            pl.BlockSpec((1, sq, past), lambda bi, hi: (hi, 0, 0)),
            pl.BlockSpec((1, sq, n_new), lambda bi, hi: (hi, 0, 0)),
            pl.BlockSpec((1, V_DIM), lambda bi, hi: (0, 0)),
        ],
        out_specs=per_head(sq),
        out_shape=jax.ShapeDtypeStruct((b, sq, c), jnp.bfloat16),
        compiler_params=pltpu.CompilerParams(
            dimension_semantics=("arbitrary",) * 2, vmem_limit_bytes=VMEM_LIMIT),
        name="attn_sample",
    )(lam, q, cache_k, cache_v, k_new, v_new, bias_c, bias_n, subln_g.reshape(1, -1))


def _mid_kernel(conv_ref, att_ref, x_ref, wc_ref, wa_ref, g_ref, wq_ref, sk_ref,
                x1_ref, h2_ref, st_ref):
    x1 = (x_ref[...]
          + jnp.dot(conv_ref[...], wc_ref[...], preferred_element_type=jnp.float32)
          + jnp.dot(att_ref[...], wa_ref[...], preferred_element_type=jnp.float32))
    x1_ref[...] = x1
    h2 = _rms(x1, g_ref[...]).astype(jnp.bfloat16)
    h2_ref[...] = h2
    qq = jnp.dot(h2, wq_ref[...], preferred_element_type=jnp.float32).astype(jnp.bfloat16)
    for rp in range(2 * R_HEADS):
        st_ref[rp] = lax.dot_general(sk_ref[rp], qq[:, rp * N_KEYS:(rp + 1) * N_KEYS], _NT,
                                     preferred_element_type=jnp.float32)


def _mid(conv, att, x2d, w_out_bf, g_ffn, w_query_bf, sub_keys_bf):
    t, d = x2d.shape
    c = conv.shape[1]
    tb = min(t, 512)
    assert t % tb == 0
    dq = w_query_bf.shape[1]
    nrp = sub_keys_bf.shape[0]
    tok = lambda i: (i, 0)
    const = lambda i: (0, 0)
    return pl.pallas_call(
        _mid_kernel,
        grid=(t // tb,),
        in_specs=[
            pl.BlockSpec((tb, c), tok),
            pl.BlockSpec((tb, c), tok),
            pl.BlockSpec((tb, d), tok),
            pl.BlockSpec((c, d), const),
            pl.BlockSpec((c, d), lambda i: (1, 0)),
            pl.BlockSpec((1, d), const),
            pl.BlockSpec((d, dq), const),
            pl.BlockSpec(sub_keys_bf.shape, lambda i: (0, 0, 0)),
        ],
        out_specs=[
            pl.BlockSpec((tb, d), tok),
            pl.BlockSpec((tb, d), tok),
            pl.BlockSpec((nrp, N_KEYS, tb), lambda i: (0, 0, i)),
        ],
        out_shape=[
            jax.ShapeDtypeStruct((t, d), jnp.float32),
            jax.ShapeDtypeStruct((t, d), jnp.bfloat16),
            jax.ShapeDtypeStruct((nrp, N_KEYS, t), jnp.float32),
        ],
        compiler_params=pltpu.CompilerParams(
            dimension_semantics=("arbitrary",), vmem_limit_bytes=VMEM_LIMIT),
        name="mid",
    )(conv, att, x2d, w_out_bf, w_out_bf, g_ffn.reshape(1, -1), w_query_bf, sub_keys_bf)


def _ce(a, b):
    if a is None:
        return b, None
    if b is None:
        return a, None
    return jnp.maximum(a, b), jnp.minimum(a, b)


def _sort_desc(xs):
    xs = list(xs)
    n = len(xs)
    p = 1
    while p < n:
        k = p
        while k >= 1:
            for j in range(k % p, n - k, 2 * k):
                for i in range(min(k, n - j - k)):
                    if (i + j) // (2 * p) == (i + j + k) // (2 * p):
                        xs[i + j], xs[i + j + k] = _ce(xs[i + j], xs[i + j + k])
            k //= 2
        p *= 2
    return xs


def _bitonic_top(a, b):
    n = len(a)
    return [_ce(a[i], b[n - 1 - i])[0] for i in range(n)]


def _bitonic_sort_desc(xs):
    xs = list(xs)
    n = len(xs)
    d = n // 2
    while d >= 1:
        for i in range(n):
            if i & d == 0:
                xs[i], xs[i + d] = _ce(xs[i], xs[i + d])
        d //= 2
    return xs


def _fill(xs):
    return [jnp.full((SUBLANES, LANES), NEG_BIG, jnp.float32) if x is None else x for x in xs]


def _sublane_merge_sorted(xs):
    for shift in (4, 2, 1):
        other = [pltpu.roll(x, shift, 0) for x in xs]
        xs = _bitonic_sort_desc(_bitonic_top(xs, other))
    return xs


def _sublane_merge_kth(xs):
    for shift in (4, 2):
        other = [pltpu.roll(x, shift, 0) for x in xs]
        xs = _bitonic_sort_desc(_bitonic_top(xs, other))
    other = [pltpu.roll(x, 1, 0) for x in xs]
    top = _bitonic_top(xs, other)
    return functools.reduce(jnp.minimum, top)


def _top16_rows(s):
    tiles = [s[i * SUBLANES:(i + 1) * SUBLANES, :] for i in range(N_KEYS // SUBLANES)]
    return _sublane_merge_sorted(_sort_desc(tiles))


def _dup_bf16(x):
    b = pltpu.bitcast(x.astype(jnp.bfloat16).astype(jnp.float32), jnp.uint32)
    return b | (b >> 16)


def _route_kernel(st_ref, rank_ref, n_ref, ea_ref, eb_ref, *, tb):
    sub = lax.broadcasted_iota(jnp.int32, (SUBLANES, LANES), 0)
    for g in range(tb // LANES):
        cols = slice(g * LANES, (g + 1) * LANES)
        for r in range(R_HEADS):
            s1 = st_ref[2 * r, :, cols]
            s2 = st_ref[2 * r + 1, :, cols]
            v1 = _top16_rows(s1)
            v2 = _top16_rows(s2)
            pack = lambda v, base: functools.reduce(
                lambda acc, j: jnp.where(sub == j, v[base + j], acc), range(1, SUBLANES), v[base])
            w1a, w1b, w2a, w2b = pack(v1, 0), pack(v1, 8), pack(v2, 0), pack(v2, 8)
            neg = jnp.full((SUBLANES, LANES), NEG_BIG, jnp.float32)
            cands = [
                v1[0] + w2a,
                v1[0] + w2b,
                jnp.where(sub >= 1, v2[0] + w1a, neg),
                v2[0] + w1b,
                jnp.where(sub >= 1, v1[1] + w2a, neg),
                jnp.where(sub >= 2, v2[1] + w1a, neg),
                jnp.where((sub >= 2) & (sub <= 4), v1[2] + w2a, neg),
                jnp.where((sub >= 2) & (sub <= 3), v1[3] + w2a, neg),
                jnp.where(sub == 2, v1[4] + w2a, neg),
            ]
            srt = _fill(_sort_desc(cands + [None] * (TOPK - len(cands))))
            thr = _sublane_merge_kth(srt)
            m1, m2 = v1[0], v2[0]
            top = m1 + m2
            z = functools.reduce(
                lambda acc, cnd: acc + jnp.where(cnd >= thr, jnp.exp(cnd - top), 0.0), cands,
                jnp.zeros((SUBLANES, LANES), jnp.float32))
            for shift in (4, 2, 1):
                z = z + pltpu.roll(z, shift, 0)
            rank2 = jnp.zeros((N_KEYS, LANES), jnp.float32)
            cnt = jnp.zeros((N_KEYS, LANES), jnp.float32)
            thr_row = thr[0:1, :]
            for j in range(TOPK):
                rank2 = jnp.where(v2[j][0:1, :] > s2, j + 1.0, rank2)
            for j in range(TOPK // 2):
                cnt = jnp.where(s1 + v2[j][0:1, :] >= thr_row, j + 1.0, cnt)
            cnt_top = jnp.zeros((SUBLANES, LANES), jnp.float32)
            for j in range(TOPK // 2, TOPK):
                cnt_top = jnp.where(v1[0] + v2[j] >= thr, j + 1.0, cnt_top)
            cnt = jnp.maximum(cnt, jnp.where(s1 == v1[0][0:1, :], cnt_top[0:1, :], 0.0))
            n_ref[r, :, cols] = _dup_bf16(cnt)
            ea_ref[r, :, cols] = _dup_bf16(jnp.exp(s1 - m1[0:1, :]) / z[0:1, :])
            eb_ref[r, :, cols] = jnp.exp(s2 - m2[0:1, :]).astype(jnp.bfloat16)
            rank_ref[r, :, cols] = rank2.astype(jnp.bfloat16)


def _route(st):
    nrp, nk, t = st.shape
    tb = min(t, 256)
    assert t % tb == 0 and tb % LANES == 0 and nk == N_KEYS and nrp == 2 * R_HEADS
    row_spec = pl.BlockSpec((R_HEADS, nk, tb), lambda i: (0, 0, i))
    return pl.pallas_call(
        functools.partial(_route_kernel, tb=tb),
        grid=(t // tb,),
        in_specs=[pl.BlockSpec((nrp, nk, tb), lambda i: (0, 0, i))],
        out_specs=[row_spec] * 4,
        out_shape=[
            jax.ShapeDtypeStruct((R_HEADS, nk, t), jnp.bfloat16),
            jax.ShapeDtypeStruct((R_HEADS, nk, t), jnp.uint32),
            jax.ShapeDtypeStruct((R_HEADS, nk, t), jnp.uint32),
            jax.ShapeDtypeStruct((R_HEADS, nk, t), jnp.bfloat16),
        ],
        compiler_params=pltpu.CompilerParams(
            dimension_semantics=("arbitrary",), vmem_limit_bytes=VMEM_LIMIT),
        name="route",
    )(st)


def _gelu(x):
    hx = 0.5 * x
    return hx + hx * lax.erf(x * (2.0 ** -0.5))


def _packed_row(ref, r, row, cols):
    tile = jnp.broadcast_to(ref[r, row:row + 1, cols], (SUBLANES, LANES))
    return pltpu.bitcast(tile, jnp.bfloat16)


def _peer_kernel(h_ref, u_ref, vt_ref, rank_ref, eb_ref, n_odd_ref, n_even_ref, ea_odd_ref, ea_even_ref,
                 x1_ref, g_ref, y_ref, acc_ref, act_a, act_b, coef_a, coef_b, *, eb_rows, tb):
    j, s = pl.program_id(0), pl.program_id(1)
    n_rows, last = pl.num_programs(0), pl.num_programs(1) - 1

    @pl.when((s == 0) & (j > 0))
    def _():
        acc_ref[...] = jnp.zeros_like(acc_ref)

    n_tg = tb // LANES
    tn = min(tb, MXU_COLS)

    def accumulate(half, coef_ref, piece):
        vt = vt_ref[:, half * eb_rows:(half + 1) * eb_rows]
        cols = slice(piece * tn, (piece + 1) * tn)
        acc_ref[:, cols] += jnp.dot(vt, coef_ref[:, cols], preferred_element_type=jnp.float32)

    def gate(cnt_ref, gain_ref, act_ref, coef_ref, tg):
        zero = jnp.zeros((PACKED_ROWS, LANES), jnp.bfloat16)
        cols = slice(tg * LANES, (tg + 1) * LANES)
        for ci in range(eb_rows // N_KEYS):
            cnt = [_packed_row(cnt_ref, r, ci, cols) for r in range(R_HEADS)]
            ea = [_packed_row(gain_ref, r, ci, cols) for r in range(R_HEADS)]
            for ch in range(N_KEYS // PACKED_ROWS):
                keys = slice(ch * PACKED_ROWS, (ch + 1) * PACKED_ROWS)
                g = zero
                for r in range(R_HEADS):
                    sel = jnp.minimum(jnp.maximum(cnt[r] - rank_ref[r, keys, cols], 0), 1)
                    g = g + (ea[r] * sel) * eb_ref[r, keys, cols]
                rows = slice(ci * N_KEYS + ch * PACKED_ROWS, ci * N_KEYS + (ch + 1) * PACKED_ROWS)
                coef_ref[rows, cols] = g * _gelu(act_ref[rows, cols].astype(jnp.bfloat16))

    def activate(half, act_ref, piece):
        u = u_ref[half * eb_rows:(half + 1) * eb_rows, :]
        cols = slice(piece * tn, (piece + 1) * tn)
        act_ref[:, cols] = lax.dot_general(u, h_ref[cols, :], _NT, preferred_element_type=jnp.float32)

    n_piece = tb // tn

    per = n_tg // n_piece

    halves = ((n_even_ref, ea_even_ref, act_a, coef_a), (n_odd_ref, ea_odd_ref, act_b, coef_b))

    def step(drain, fill, finish):
        for half, (cnt_ref, gain_ref, act_ref, coef_ref) in enumerate(halves):
            for piece in range(n_piece):
                if drain:
                    for tg in range(piece * per, (piece + 1) * per):
                        gate(cnt_ref, gain_ref, act_ref, coef_ref, tg)
                    accumulate(half, coef_ref, piece)
                if fill:
                    activate(half, act_ref, piece)
        if finish:
            y_ref[...] = _rms(x1_ref[...] + acc_ref[...].T, g_ref[...])

    pl.when((j > 0) & (s < last))(functools.partial(step, True, True, False))
    pl.when((j > 0) & (j < n_rows - 1) & (s == last))(functools.partial(step, True, True, True))
    pl.when((j == n_rows - 1) & (s == last))(functools.partial(step, True, False, True))
    pl.when((j == 0) & (s == last))(functools.partial(step, False, True, False))


def _peer(h2, u_bf, vt_bf, rank2, cnt, ea, eb, x1, g_final):
    t, d = h2.shape
    n_exp = u_bf.shape[0]
    tb = min(t, 512)
    eb_rows = SUBLANES * N_KEYS
    assert t % tb == 0 and n_exp % (2 * eb_rows) == 0 and n_exp == N_KEYS * N_KEYS
    n_tb = t // tb
    n_steps = n_exp // (2 * eb_rows)
    drained = lambda j: jnp.clip(j - 1, 0, n_tb - 1)
    filled = lambda j, s: jnp.where(s == n_steps - 1, jnp.minimum(j, n_tb - 1), drained(j))
    tok = lambda j, s: (drained(j), 0)
    tile_spec = pl.BlockSpec((R_HEADS, N_KEYS, tb), lambda j, s: (0, 0, drained(j)))
    even_spec = pl.BlockSpec((R_HEADS, SUBLANES, tb), lambda j, s: (0, 2 * s, drained(j)))
    odd_spec = pl.BlockSpec((R_HEADS, SUBLANES, tb), lambda j, s: (0, 2 * s + 1, drained(j)))
    return pl.pallas_call(
        functools.partial(_peer_kernel, eb_rows=eb_rows, tb=tb),
        grid=(n_tb + 1, n_steps),
        in_specs=[
            pl.BlockSpec((tb, d), lambda j, s: (filled(j, s), 0)),
            pl.BlockSpec((2 * eb_rows, d), lambda j, s: ((s + 1) % n_steps, 0)),
            pl.BlockSpec((d, 2 * eb_rows), lambda j, s: (0, s)),
            tile_spec, tile_spec, odd_spec, even_spec, odd_spec, even_spec,
            pl.BlockSpec((tb, d), tok),
            pl.BlockSpec((1, d), lambda j, s: (0, 0)),
        ],
        out_specs=pl.BlockSpec((tb, d), tok),
        out_shape=jax.ShapeDtypeStruct((t, d), jnp.float32),
        scratch_shapes=[
            pltpu.VMEM((d, tb), jnp.float32),
            pltpu.VMEM((eb_rows, tb), jnp.float32),
            pltpu.VMEM((eb_rows, tb), jnp.float32),
            pltpu.VMEM((eb_rows, tb), jnp.bfloat16),
            pltpu.VMEM((eb_rows, tb), jnp.bfloat16),
        ],
        compiler_params=pltpu.CompilerParams(
            dimension_semantics=("arbitrary", "arbitrary"), vmem_limit_bytes=VMEM_LIMIT),
        name="peer",
    )(h2, u_bf, vt_bf, rank2, eb, cnt, cnt, ea, ea, x1, g_final.reshape(1, -1))


def _stream(x, left, attend, p):
    b, s, d = x.shape
    q, k, v, kb, vb, conv, tail = _in_proj(x, p["g_mix"], p["w_in"], left, p["conv_w"], p["conv_b"],
                                           p["ln_g"], p["ln_b"])
    att = attend(q, kb, vb)
    c = conv.shape[-1]
    x1, h2, st = _mid(conv.reshape(b * s, c), att.reshape(b * s, c), x.reshape(b * s, d),
                      p["w_out"], p["g_ffn"], p["w_query"], p["sub_keys"])
    rank2, cnt, ea, eb = _route(st)
    y = _peer(h2, p["peer_u"], p["peer_vt"], rank2, cnt, ea, eb, x1, p["g_final"])
    k = k.reshape(b, s, N_HEADS, 2, HEAD_DIM)
    v = v.reshape(b, s, N_HEADS, V_DIM)
    return y.reshape(b, s, d), k, v, tail[:, CONV_PAD - (CONV_WIDTH - 1):]


def kernel(x_prompt, x_sample, cache_k, cache_v, state_conv, g_mix, w_in, conv_w, conv_b, conv_ln_g, conv_ln_b, lambda_q1, lambda_k1, lambda_q2, lambda_k2, subln_g, rel_bias, w_out, g_ffn, w_query, sub_keys, peer_u, peer_v, g_final):
    depth = w_in.shape[0]
    assert depth == 1, "single-layer step"
    l = 0
    bf16 = jnp.bfloat16
    b, s, d = x_prompt.shape
    bd, sd, _ = x_sample.shape
    past = cache_k.shape[2]
    c = conv_w.shape[-1]

    lam_init = _lambda_init(l)
    lam = (jnp.exp(jnp.sum(lambda_q1[l].astype(jnp.float32) * lambda_k1[l].astype(jnp.float32)))
           - jnp.exp(jnp.sum(lambda_q2[l].astype(jnp.float32) * lambda_k2[l].astype(jnp.float32)))
           + lam_init).reshape(1)
    out_scale = 1.0 - lam_init

    p = {
        "g_mix": g_mix[l], "w_in": w_in[l].astype(bf16), "conv_w": conv_w[l], "conv_b": conv_b[l],
        "ln_g": conv_ln_g[l], "ln_b": conv_ln_b[l], "w_out": w_out[l].astype(bf16), "g_ffn": g_ffn[l],
        "w_query": w_query[l].astype(bf16),
        "sub_keys": sub_keys[l].reshape(2 * R_HEADS, N_KEYS, -1).astype(bf16),
        "peer_u": peer_u[l].astype(bf16), "peer_vt": peer_v[l].astype(bf16).T, "g_final": g_final,
    }

    attend_p = lambda q, kb, vb: _attn_prompt(q, kb, vb, rel_bias, lam, subln_g[l], out_scale)
    y_p, k_p, v_p, tail_p = _stream(x_prompt, jnp.zeros((b, CONV_PAD, c), jnp.float32), attend_p, p)

    n_new = -(-sd // LANES) * LANES
    pos_s = past + jnp.arange(sd, dtype=jnp.int32)
    bias_c = _bias_table(rel_bias, pos_s, jnp.arange(past, dtype=jnp.int32), True)
    bias_n = _bias_table(rel_bias, pos_s, past + jnp.arange(n_new, dtype=jnp.int32), True)
    bias_n = jnp.where(jnp.arange(n_new) < sd, bias_n, NEG)
    ck = cache_k[l].reshape(bd, past, c)
    cv = cache_v[l].reshape(bd, past, c)

    def attend_s(q, kb, vb):
        pad = ((0, 0), (0, n_new - sd), (0, 0))
        return _attn_sample(q, ck, cv, jnp.pad(kb, pad), jnp.pad(vb, pad), bias_c, bias_n,
                            lam, subln_g[l], out_scale)

    left_s = jnp.pad(state_conv[l], ((0, 0), (CONV_PAD - (CONV_WIDTH - 1), 0), (0, 0)))
    y_s, k_s, v_s, tail_s = _stream(x_sample, left_s, attend_s, p)

    return (y_p, y_s, k_p[None], v_p[None], tail_p[None], k_s[None], v_s[None], tail_s[None])
```

```python
import functools
import math

import jax
import jax.numpy as jnp
from jax import lax
from jax.experimental import pallas as pl
from jax.experimental.pallas import tpu as pltpu

CHUNK = 64
CONV_WIDTH = 31
CONV_PAD = 32
N_HEADS = 4
HEAD_DIM = 64
V_DIM = 2 * HEAD_DIM
N_BUCKETS = 32
MAX_DISTANCE = 128
N_KEYS = 128
R_HEADS = 8
TOPK = 16
EPS = 1e-6
NEG = -1e30
NEG_BIG = -3.0e38
LOG2_E = 1.4426950408889634
LANES = 128
SUBLANES = 8
PACKED_ROWS = 2 * SUBLANES
MXU_COLS = 256
VMEM_LIMIT = 48 * 1024 * 1024

_NT = (((1,), (1,)), ((), ()))


def _lambda_init(layer):
    return 0.8 - 0.6 * math.exp(-0.3 * layer)


def _rms(xf, g):
    return xf * lax.rsqrt(jnp.mean(xf * xf, axis=-1, keepdims=True) + EPS) * g


def _sigmoid(x):
    return 1.0 / (1.0 + jnp.exp(-x))


def _in_proj_kernel(x_ref, g_ref, w_ref, left_ref, cw_ref, cb_ref, lg_ref, lb_ref,
                    q_ref, k_ref, v_ref, kb_ref, vb_ref, conv_ref, tail_ref, abuf, *, ts, c):
    s = pl.program_id(1)
    h = _rms(x_ref[0], g_ref[...]).astype(jnp.bfloat16)

    @pl.when(s == 0)
    def _():
        abuf[0, 0:CONV_PAD, :] = left_ref[0]

    glu_in = jnp.dot(h, w_ref[:, 0:2 * c], preferred_element_type=jnp.float32)
    abuf[0, CONV_PAD:CONV_PAD + ts, :] = glu_in[:, :c] * _sigmoid(glu_in[:, c:])
    q = jnp.dot(h, w_ref[:, 2 * c:3 * c], preferred_element_type=jnp.float32)
    q_ref[0] = (q * (HEAD_DIM ** -0.5)).astype(jnp.bfloat16)
    k = jnp.dot(h, w_ref[:, 3 * c:4 * c], preferred_element_type=jnp.float32)
    k_ref[0] = k
    kb_ref[0] = k.astype(jnp.bfloat16)
    v = jnp.dot(h, w_ref[:, 4 * c:5 * c], preferred_element_type=jnp.float32)
    v_ref[0] = v
    vb_ref[0] = v.astype(jnp.bfloat16)

    n_sh = ts + CONV_PAD - SUBLANES
    for sh in range(1, SUBLANES):
        abuf[sh, 0:n_sh, :] = abuf[0, sh:sh + n_sh, :]
    rc = min(ts, 64)
    for r0 in range(0, ts, rc):
        acc = jnp.zeros((rc, c), jnp.float32)
        for w in range(CONV_WIDTH):
            off = r0 + CONV_PAD - (CONV_WIDTH - 1) + w
            sh = off % SUBLANES
            acc = acc + abuf[sh, off - sh:off - sh + rc, :] * cw_ref[w:w + 1, :]
        y = acc + cb_ref[...]
        mu = jnp.mean(y, axis=-1, keepdims=True)
        d = y - mu
        var = jnp.mean(d * d, axis=-1, keepdims=True)
        yn = d * lax.rsqrt(var + EPS) * lg_ref[...] + lb_ref[...]
        conv_ref[0, r0:r0 + rc, :] = (yn * _sigmoid(yn)).astype(jnp.bfloat16)

    tail = abuf[0, ts:ts + CONV_PAD, :]
    tail_ref[0] = tail
    abuf[0, 0:CONV_PAD, :] = tail


def _in_proj(x, g_mix, w_in_bf, left, conv_w, conv_b, ln_g, ln_b):
    b, s, d = x.shape
    c = conv_w.shape[1]
    ts = min(s, 512)
    assert s % ts == 0 and ts >= CONV_PAD and ts % SUBLANES == 0
    cw = jnp.pad(conv_w, ((0, CONV_PAD - CONV_WIDTH), (0, 0)))
    row = lambda a: a.reshape(1, -1)
    tok = lambda bi, si: (bi, si, 0)
    const2 = lambda bi, si: (0, 0)
    f32, bf16 = jnp.float32, jnp.bfloat16
    outs = pl.pallas_call(
        functools.partial(_in_proj_kernel, ts=ts, c=c),
        grid=(b, s // ts),
        in_specs=[
            pl.BlockSpec((1, ts, d), tok),
            pl.BlockSpec((1, d), const2),
            pl.BlockSpec(w_in_bf.shape, const2),
            pl.BlockSpec((1, CONV_PAD, c), lambda bi, si: (bi, 0, 0)),
            pl.BlockSpec((CONV_PAD, c), const2),
            pl.BlockSpec((1, c), const2),
            pl.BlockSpec((1, c), const2),
            pl.BlockSpec((1, c), const2),
        ],
        out_specs=[pl.BlockSpec((1, ts, c), tok)] * 6
        + [pl.BlockSpec((1, CONV_PAD, c), lambda bi, si: (bi, 0, 0))],
        out_shape=[
            jax.ShapeDtypeStruct((b, s, c), bf16),
            jax.ShapeDtypeStruct((b, s, c), f32),
            jax.ShapeDtypeStruct((b, s, c), f32),
            jax.ShapeDtypeStruct((b, s, c), bf16),
            jax.ShapeDtypeStruct((b, s, c), bf16),
            jax.ShapeDtypeStruct((b, s, c), bf16),
            jax.ShapeDtypeStruct((b, CONV_PAD, c), f32),
        ],
        scratch_shapes=[pltpu.VMEM((SUBLANES, ts + CONV_PAD, c), f32)],
        compiler_params=pltpu.CompilerParams(
            dimension_semantics=("arbitrary", "arbitrary"), vmem_limit_bytes=VMEM_LIMIT),
        name="in_proj",
    )(x, row(g_mix), w_in_bf, left, cw, row(conv_b), row(ln_g), row(ln_b))
    return outs


def _rel_bucket(rel):
    nb = N_BUCKETS // 2
    max_exact = nb // 2
    ret = jnp.where(rel > 0, nb, 0)
    n = jnp.abs(rel)
    nf = jnp.maximum(n, 1).astype(jnp.float32)
    large = max_exact + (jnp.log(nf / max_exact) / math.log(MAX_DISTANCE / max_exact)
                         * (nb - max_exact)).astype(jnp.int32)
    large = jnp.minimum(large, nb - 1)
    return ret + jnp.where(n < max_exact, n, large)


def _bias_table(rel_bias, q_pos, k_pos, masked):
    nq, nk = q_pos.shape[0], k_pos.shape[0]
    period = nq + nk
    m = jnp.arange(period, dtype=jnp.int32)
    bucket = _rel_bucket(k_pos[0] - q_pos[0] + jnp.where(m < nk, m, m - period))
    table = rel_bias.astype(jnp.float32).T[:, None, :]
    hit = bucket[None, :, None] == jnp.arange(N_BUCKETS, dtype=jnp.int32)
    line = jnp.sum(jnp.where(hit, table, 0.0), axis=-1)
    bias = jnp.tile(line, (1, nq))[:, :nq * (period - 1)].reshape(-1, nq, period - 1)[:, :, :nk]
    if masked:
        mask = (k_pos[None, :] // CHUNK) <= (q_pos[:, None] // CHUNK)
        bias = jnp.where(mask[None], bias, NEG)
    return bias


def _split_maps(q):
    lane = lax.broadcasted_iota(jnp.int32, q.shape, 1)
    zero = jnp.zeros_like(q)
    return jnp.where(lane < HEAD_DIM, q, zero), jnp.where(lane >= HEAD_DIM, q, zero)


def _softmax_rows(s):
    p = jnp.exp(s - jnp.max(s, axis=-1, keepdims=True))
    return p / jnp.sum(p, axis=-1, keepdims=True)


def _attn_finish(s1, s2, vv, lam, g, out_scale):
    attn = (_softmax_rows(s1) - lam * _softmax_rows(s2)).astype(jnp.bfloat16)
    o = jnp.dot(attn, vv, preferred_element_type=jnp.float32)
    return (_rms(o, g) * out_scale).astype(jnp.bfloat16)


def _attn_prompt_kernel(lam_ref, q_ref, k_ref, v_ref, slab_ref, g_ref, o_ref,
                        s_ref, m_ref, l_ref, acc_ref, *, tq, out_scale):
    qi = pl.program_id(2)
    q1, q2 = _split_maps(q_ref[0])
    width = m_ref.shape[-1]
    fold = lambda a, op: functools.reduce(op, [a[:, c:c + width] for c in range(0, tq, width)])

    m_ref[...] = jnp.full(m_ref.shape, NEG_BIG, jnp.float32)
    l_ref[...] = jnp.zeros_like(l_ref)
    acc_ref[...] = jnp.zeros_like(acc_ref)

    def logits(j, carry):
        kj = k_ref[0, pl.ds(pl.multiple_of(j * tq, tq), tq), :]
        bias = slab_ref[0, jnp.clip(j - qi, -2, 0) + 2]
        for mp, qm in enumerate((q1, q2)):
            a = (lax.dot_general(qm, kj, _NT, preferred_element_type=jnp.float32) + bias) * LOG2_E
            s_ref[mp, j] = a
            m_ref[mp] = jnp.maximum(m_ref[mp], fold(a, jnp.maximum))
        return carry

    lax.fori_loop(0, qi + 1, logits, 0)
    lanes = min(tq, LANES)
    row_max = [jnp.broadcast_to(jnp.max(m_ref[mp], axis=-1, keepdims=True), (tq, lanes)) for mp in range(2)]

    def accumulate(j, carry):
        vj = v_ref[0, pl.ds(pl.multiple_of(j * tq, tq), tq), :]
        for mp in range(2):
            p = jnp.concatenate([jnp.exp2(s_ref[mp, j, :, c * lanes:(c + 1) * lanes] - row_max[mp])
                                 for c in range(tq // lanes)], axis=1)
            l_ref[mp] += fold(p, jnp.add)
            acc_ref[mp] += jnp.dot(p.astype(jnp.bfloat16), vj, preferred_element_type=jnp.float32)
        return carry

    lax.fori_loop(0, qi + 1, accumulate, 0)
    inv = [1.0 / jnp.sum(l_ref[mp], axis=-1, keepdims=True) for mp in range(2)]
    o = acc_ref[0] * inv[0] - lam_ref[0] * (acc_ref[1] * inv[1])
    o_ref[0] = (_rms(o, g_ref[...]) * out_scale).astype(jnp.bfloat16)


def _attn_prompt(q, kb, vb, rel_bias, lam, subln_g, out_scale):
    b, s, c = q.shape
    tq = min(s, 512)
    assert s % tq == 0 and tq % CHUNK == 0 and tq >= MAX_DISTANCE and c == N_HEADS * V_DIM
    n_kb = s // tq
    pos = jnp.arange(tq, dtype=jnp.int32)
    slabs = jnp.stack([
        _bias_table(rel_bias, pos + 2 * tq, pos, False),
        _bias_table(rel_bias, pos + tq, pos, False),
        _bias_table(rel_bias, pos, pos, True),
    ], axis=1)
    width = min(tq, LANES)
    return pl.pallas_call(
        functools.partial(_attn_prompt_kernel, tq=tq, out_scale=out_scale),
        grid=(b, N_HEADS, n_kb),
        in_specs=[
            pl.BlockSpec(memory_space=pltpu.SMEM),
            pl.BlockSpec((1, tq, V_DIM), lambda bi, hi, qi: (bi, qi, hi)),
            pl.BlockSpec((1, s, V_DIM), lambda bi, hi, qi: (bi, 0, hi)),
            pl.BlockSpec((1, s, V_DIM), lambda bi, hi, qi: (bi, 0, hi)),
            pl.BlockSpec((1, 3, tq, tq), lambda bi, hi, qi: (hi, 0, 0, 0)),
            pl.BlockSpec((1, V_DIM), lambda bi, hi, qi: (0, 0)),
        ],
        out_specs=pl.BlockSpec((1, tq, V_DIM), lambda bi, hi, qi: (bi, qi, hi)),
        out_shape=jax.ShapeDtypeStruct((b, s, c), jnp.bfloat16),
        scratch_shapes=[
            pltpu.VMEM((2, n_kb, tq, tq), jnp.float32),
            pltpu.VMEM((2, tq, width), jnp.float32),
            pltpu.VMEM((2, tq, width), jnp.float32),
            pltpu.VMEM((2, tq, V_DIM), jnp.float32),
        ],
        compiler_params=pltpu.CompilerParams(
            dimension_semantics=("arbitrary",) * 3, vmem_limit_bytes=VMEM_LIMIT),
        name="attn_prompt",
    )(lam, q, kb, vb, slabs, subln_g.reshape(1, -1))


def _attn_sample_kernel(lam_ref, q_ref, ck_ref, cv_ref, kn_ref, vn_ref, bc_ref, bn_ref, g_ref, o_ref,
                        *, out_scale):
    q1, q2 = _split_maps(q_ref[0])
    ck = ck_ref[0].astype(jnp.bfloat16)
    kn = kn_ref[0]
    logits = lambda qm: jnp.concatenate([
        lax.dot_general(qm, ck, _NT, preferred_element_type=jnp.float32) + bc_ref[0],
        lax.dot_general(qm, kn, _NT, preferred_element_type=jnp.float32) + bn_ref[0]], axis=1)
    vv = jnp.concatenate([cv_ref[0].astype(jnp.bfloat16), vn_ref[0]], axis=0)
    o_ref[0] = _attn_finish(logits(q1), logits(q2), vv, lam_ref[0], g_ref[...], out_scale)


def _attn_sample(q, cache_k, cache_v, k_new, v_new, bias_c, bias_n, lam, subln_g, out_scale):
    b, sq, c = q.shape
    past, n_new = cache_k.shape[1], k_new.shape[1]
    per_head = lambda rows: pl.BlockSpec((1, rows, V_DIM), lambda bi, hi: (bi, 0, hi))
    return pl.pallas_call(
        functools.partial(_attn_sample_kernel, out_scale=out_scale),
        grid=(b, N_HEADS),
        in_specs=[
            pl.BlockSpec(memory_space=pltpu.SMEM),
            per_head(sq), per_head(past), per_head(past), per_head(n_new), per_head(n_new),
            pl.BlockSpec((1, sq, past), lambda bi, hi: (hi, 0, 0)),
            pl.BlockSpec((1, sq, n_new), lambda bi, hi: (hi, 0, 0)),
            pl.BlockSpec((1, V_DIM), lambda bi, hi: (0, 0)),
        ],
        out_specs=per_head(sq),
        out_shape=jax.ShapeDtypeStruct((b, sq, c), jnp.bfloat16),
        compiler_params=pltpu.CompilerParams(
            dimension_semantics=("arbitrary",) * 2, vmem_limit_bytes=VMEM_LIMIT),
        name="attn_sample",
    )(lam, q, cache_k, cache_v, k_new, v_new, bias_c, bias_n, subln_g.reshape(1, -1))


def _mid_kernel(conv_ref, att_ref, x_ref, wc_ref, wa_ref, g_ref, wq_ref, sk_ref,
                x1_ref, h2_ref, st_ref):
    x1 = (x_ref[...]
          + jnp.dot(conv_ref[...], wc_ref[...], preferred_element_type=jnp.float32)
          + jnp.dot(att_ref[...], wa_ref[...], preferred_element_type=jnp.float32))
    x1_ref[...] = x1
    h2 = _rms(x1, g_ref[...]).astype(jnp.bfloat16)
    h2_ref[...] = h2
    qq = jnp.dot(h2, wq_ref[...], preferred_element_type=jnp.float32).astype(jnp.bfloat16)
    for rp in range(2 * R_HEADS):
        st_ref[rp] = lax.dot_general(sk_ref[rp], qq[:, rp * N_KEYS:(rp + 1) * N_KEYS], _NT,
                                     preferred_element_type=jnp.float32)


def _mid(conv, att, x2d, w_out_bf, g_ffn, w_query_bf, sub_keys_bf):
    t, d = x2d.shape
    c = conv.shape[1]
    tb = min(t, 512)
    assert t % tb == 0
    dq = w_query_bf.shape[1]
    nrp = sub_keys_bf.shape[0]
    tok = lambda i: (i, 0)
    const = lambda i: (0, 0)
    return pl.pallas_call(
        _mid_kernel,
        grid=(t // tb,),
        in_specs=[
            pl.BlockSpec((tb, c), tok),
            pl.BlockSpec((tb, c), tok),
            pl.BlockSpec((tb, d), tok),
            pl.BlockSpec((c, d), const),
            pl.BlockSpec((c, d), lambda i: (1, 0)),
            pl.BlockSpec((1, d), const),
            pl.BlockSpec((d, dq), const),
            pl.BlockSpec(sub_keys_bf.shape, lambda i: (0, 0, 0)),
        ],
        out_specs=[
            pl.BlockSpec((tb, d), tok),
            pl.BlockSpec((tb, d), tok),
            pl.BlockSpec((nrp, N_KEYS, tb), lambda i: (0, 0, i)),
        ],
        out_shape=[
            jax.ShapeDtypeStruct((t, d), jnp.float32),
            jax.ShapeDtypeStruct((t, d), jnp.bfloat16),
            jax.ShapeDtypeStruct((nrp, N_KEYS, t), jnp.float32),
        ],
        compiler_params=pltpu.CompilerParams(
            dimension_semantics=("arbitrary",), vmem_limit_bytes=VMEM_LIMIT),
        name="mid",
    )(conv, att, x2d, w_out_bf, w_out_bf, g_ffn.reshape(1, -1), w_query_bf, sub_keys_bf)


def _ce(a, b):
    if a is None:
        return b, None
    if b is None:
        return a, None
    return jnp.maximum(a, b), jnp.minimum(a, b)


def _sort_desc(xs):
    xs = list(xs)
    n = len(xs)
    p = 1
    while p < n:
        k = p
        while k >= 1:
            for j in range(k % p, n - k, 2 * k):
                for i in range(min(k, n - j - k)):
                    if (i + j) // (2 * p) == (i + j + k) // (2 * p):
                        xs[i + j], xs[i + j + k] = _ce(xs[i + j], xs[i + j + k])
            k //= 2
        p *= 2
    return xs


def _bitonic_top(a, b):
    n = len(a)
    return [_ce(a[i], b[n - 1 - i])[0] for i in range(n)]


def _bitonic_sort_desc(xs):
    xs = list(xs)
    n = len(xs)
    d = n // 2
    while d >= 1:
        for i in range(n):
            if i & d == 0:
                xs[i], xs[i + d] = _ce(xs[i], xs[i + d])
        d //= 2
    return xs


def _fill(xs):
    return [jnp.full((SUBLANES, LANES), NEG_BIG, jnp.float32) if x is None else x for x in xs]


def _sublane_merge_sorted(xs):
    for shift in (4, 2, 1):
        other = [pltpu.roll(x, shift, 0) for x in xs]
        xs = _bitonic_sort_desc(_bitonic_top(xs, other))
    return xs


def _sublane_merge_kth(xs):
    for shift in (4, 2):
        other = [pltpu.roll(x, shift, 0) for x in xs]
        xs = _bitonic_sort_desc(_bitonic_top(xs, other))
    other = [pltpu.roll(x, 1, 0) for x in xs]
    top = _bitonic_top(xs, other)
    return functools.reduce(jnp.minimum, top)


def _top16_rows(s):
    tiles = [s[i * SUBLANES:(i + 1) * SUBLANES, :] for i in range(N_KEYS // SUBLANES)]
    return _sublane_merge_sorted(_sort_desc(tiles))


def _dup_bf16(x):
    b = pltpu.bitcast(x.astype(jnp.bfloat16).astype(jnp.float32), jnp.uint32)
    return b | (b >> 16)


def _route_kernel(st_ref, rank_ref, n_ref, ea_ref, eb_ref, *, tb):
    sub = lax.broadcasted_iota(jnp.int32, (SUBLANES, LANES), 0)
    for g in range(tb // LANES):
        cols = slice(g * LANES, (g + 1) * LANES)
        for r in range(R_HEADS):
            s1 = st_ref[2 * r, :, cols]
            s2 = st_ref[2 * r + 1, :, cols]
            v1 = _top16_rows(s1)
            v2 = _top16_rows(s2)
            pack = lambda v, base: functools.reduce(
                lambda acc, j: jnp.where(sub == j, v[base + j], acc), range(1, SUBLANES), v[base])
            w1a, w1b, w2a, w2b = pack(v1, 0), pack(v1, 8), pack(v2, 0), pack(v2, 8)
            neg = jnp.full((SUBLANES, LANES), NEG_BIG, jnp.float32)
            cands = [
                v1[0] + w2a,
                v1[0] + w2b,
                jnp.where(sub >= 1, v2[0] + w1a, neg),
                v2[0] + w1b,
                jnp.where(sub >= 1, v1[1] + w2a, neg),
                jnp.where(sub >= 2, v2[1] + w1a, neg),
                jnp.where((sub >= 2) & (sub <= 4), v1[2] + w2a, neg),
                jnp.where((sub >= 2) & (sub <= 3), v1[3] + w2a, neg),
                jnp.where(sub == 2, v1[4] + w2a, neg),
            ]
            srt = _fill(_sort_desc(cands + [None] * (TOPK - len(cands))))
            thr = _sublane_merge_kth(srt)
            m1, m2 = v1[0], v2[0]
            top = m1 + m2
            z = functools.reduce(
                lambda acc, cnd: acc + jnp.where(cnd >= thr, jnp.exp(cnd - top), 0.0), cands,
                jnp.zeros((SUBLANES, LANES), jnp.float32))
            for shift in (4, 2, 1):
                z = z + pltpu.roll(z, shift, 0)
            rank2 = jnp.zeros((N_KEYS, LANES), jnp.float32)
            cnt = jnp.zeros((N_KEYS, LANES), jnp.float32)
            thr_row = thr[0:1, :]
            for j in range(TOPK):
                rank2 = jnp.where(v2[j][0:1, :] > s2, j + 1.0, rank2)
            for j in range(TOPK // 2):
                cnt = jnp.where(s1 + v2[j][0:1, :] >= thr_row, j + 1.0, cnt)
            cnt_top = jnp.zeros((SUBLANES, LANES), jnp.float32)
            for j in range(TOPK // 2, TOPK):
                cnt_top = jnp.where(v1[0] + v2[j] >= thr, j + 1.0, cnt_top)
            cnt = jnp.maximum(cnt, jnp.where(s1 == v1[0][0:1, :], cnt_top[0:1, :], 0.0))
            n_ref[r, :, cols] = _dup_bf16(cnt)
            ea_ref[r, :, cols] = _dup_bf16(jnp.exp(s1 - m1[0:1, :]) / z[0:1, :])
            eb_ref[r, :, cols] = jnp.exp(s2 - m2[0:1, :]).astype(jnp.bfloat16)
            rank_ref[r, :, cols] = rank2.astype(jnp.bfloat16)


def _route(st):
    nrp, nk, t = st.shape
    tb = min(t, 256)
    assert t % tb == 0 and tb % LANES == 0 and nk == N_KEYS and nrp == 2 * R_HEADS
    row_spec = pl.BlockSpec((R_HEADS, nk, tb), lambda i: (0, 0, i))
    return pl.pallas_call(
        functools.partial(_route_kernel, tb=tb),
        grid=(t // tb,),
        in_specs=[pl.BlockSpec((nrp, nk, tb), lambda i: (0, 0, i))],
        out_specs=[row_spec] * 4,
        out_shape=[
            jax.ShapeDtypeStruct((R_HEADS, nk, t), jnp.bfloat16),
            jax.ShapeDtypeStruct((R_HEADS, nk, t), jnp.uint32),
            jax.ShapeDtypeStruct((R_HEADS, nk, t), jnp.uint32),
            jax.ShapeDtypeStruct((R_HEADS, nk, t), jnp.bfloat16),
        ],
        compiler_params=pltpu.CompilerParams(
            dimension_semantics=("arbitrary",), vmem_limit_bytes=VMEM_LIMIT),
        name="route",
    )(st)


def _gelu(x):
    hx = 0.5 * x
    return hx + hx * lax.erf(x * (2.0 ** -0.5))


def _packed_row(ref, r, row, cols):
    tile = jnp.broadcast_to(ref[r, row:row + 1, cols], (SUBLANES, LANES))
    return pltpu.bitcast(tile, jnp.bfloat16)


def _peer_kernel(h_ref, u_ref, vt_ref, rank_ref, eb_ref, n_odd_ref, n_even_ref, ea_odd_ref, ea_even_ref,
                 x1_ref, g_ref, y_ref, acc_ref, act_a, act_b, coef_a, coef_b, *, eb_rows, tb):
    j, s = pl.program_id(0), pl.program_id(1)
    n_rows, last = pl.num_programs(0), pl.num_programs(1) - 1

    @pl.when((s == 0) & (j > 0))
    def _():
        acc_ref[...] = jnp.zeros_like(acc_ref)

    n_tg = tb // LANES
    tn = min(tb, MXU_COLS)

    def accumulate(half, coef_ref, piece):
        vt = vt_ref[:, half * eb_rows:(half + 1) * eb_rows]
        cols = slice(piece * tn, (piece + 1) * tn)
        acc_ref[:, cols] += jnp.dot(vt, coef_ref[:, cols], preferred_element_type=jnp.float32)

    def gate(cnt_ref, gain_ref, act_ref, coef_ref, tg):
        zero = jnp.zeros((PACKED_ROWS, LANES), jnp.bfloat16)
        cols = slice(tg * LANES, (tg + 1) * LANES)
        for ci in range(eb_rows // N_KEYS):
            cnt = [_packed_row(cnt_ref, r, ci, cols) for r in range(R_HEADS)]
            ea = [_packed_row(gain_ref, r, ci, cols) for r in range(R_HEADS)]
            for ch in range(N_KEYS // PACKED_ROWS):
                keys = slice(ch * PACKED_ROWS, (ch + 1) * PACKED_ROWS)
                g = zero
                for r in range(R_HEADS):
                    sel = jnp.minimum(jnp.maximum(cnt[r] - rank_ref[r, keys, cols], 0), 1)
                    g = g + (ea[r] * sel) * eb_ref[r, keys, cols]
                rows = slice(ci * N_KEYS + ch * PACKED_ROWS, ci * N_KEYS + (ch + 1) * PACKED_ROWS)
                coef_ref[rows, cols] = g * _gelu(act_ref[rows, cols].astype(jnp.bfloat16))

    def activate(half, act_ref, piece):
        u = u_ref[half * eb_rows:(half + 1) * eb_rows, :]
        cols = slice(piece * tn, (piece + 1) * tn)
        act_ref[:, cols] = lax.dot_general(u, h_ref[cols, :], _NT, preferred_element_type=jnp.float32)

    n_piece = tb // tn

    per = n_tg // n_piece

    halves = ((n_even_ref, ea_even_ref, act_a, coef_a), (n_odd_ref, ea_odd_ref, act_b, coef_b))

    def step(drain, fill, finish):
        for half, (cnt_ref, gain_ref, act_ref, coef_ref) in enumerate(halves):
            for piece in range(n_piece):
                if drain:
                    for tg in range(piece * per, (piece + 1) * per):
                        gate(cnt_ref, gain_ref, act_ref, coef_ref, tg)
                    accumulate(half, coef_ref, piece)
                if fill:
                    activate(half, act_ref, piece)
        if finish:
            y_ref[...] = _rms(x1_ref[...] + acc_ref[...].T, g_ref[...])

    pl.when((j > 0) & (s < last))(functools.partial(step, True, True, False))
    pl.when((j > 0) & (j < n_rows - 1) & (s == last))(functools.partial(step, True, True, True))
    pl.when((j == n_rows - 1) & (s == last))(functools.partial(step, True, False, True))
    pl.when((j == 0) & (s == last))(functools.partial(step, False, True, False))


def _peer(h2, u_bf, vt_bf, rank2, cnt, ea, eb, x1, g_final):
    t, d = h2.shape
    n_exp = u_bf.shape[0]
    tb = min(t, 512)
    eb_rows = SUBLANES * N_KEYS
    assert t % tb == 0 and n_exp % (2 * eb_rows) == 0 and n_exp == N_KEYS * N_KEYS
    n_tb = t // tb
    n_steps = n_exp // (2 * eb_rows)
    drained = lambda j: jnp.clip(j - 1, 0, n_tb - 1)
    filled = lambda j, s: jnp.where(s == n_steps - 1, jnp.minimum(j, n_tb - 1), drained(j))
    tok = lambda j, s: (drained(j), 0)
    tile_spec = pl.BlockSpec((R_HEADS, N_KEYS, tb), lambda j, s: (0, 0, drained(j)))
    even_spec = pl.BlockSpec((R_HEADS, SUBLANES, tb), lambda j, s: (0, 2 * s, drained(j)))
    odd_spec = pl.BlockSpec((R_HEADS, SUBLANES, tb), lambda j, s: (0, 2 * s + 1, drained(j)))
    return pl.pallas_call(
        functools.partial(_peer_kernel, eb_rows=eb_rows, tb=tb),
        grid=(n_tb + 1, n_steps),
        in_specs=[
            pl.BlockSpec((tb, d), lambda j, s: (filled(j, s), 0)),
            pl.BlockSpec((2 * eb_rows, d), lambda j, s: ((s + 1) % n_steps, 0)),
            pl.BlockSpec((d, 2 * eb_rows), lambda j, s: (0, s)),
            tile_spec, tile_spec, odd_spec, even_spec, odd_spec, even_spec,
            pl.BlockSpec((tb, d), tok),
            pl.BlockSpec((1, d), lambda j, s: (0, 0)),
        ],
        out_specs=pl.BlockSpec((tb, d), tok),
        out_shape=jax.ShapeDtypeStruct((t, d), jnp.float32),
        scratch_shapes=[
            pltpu.VMEM((d, tb), jnp.float32),
            pltpu.VMEM((eb_rows, tb), jnp.float32),
            pltpu.VMEM((eb_rows, tb), jnp.float32),
            pltpu.VMEM((eb_rows, tb), jnp.bfloat16),
            pltpu.VMEM((eb_rows, tb), jnp.bfloat16),
        ],
        compiler_params=pltpu.CompilerParams(
            dimension_semantics=("arbitrary", "arbitrary"), vmem_limit_bytes=VMEM_LIMIT),
        name="peer",
    )(h2, u_bf, vt_bf, rank2, eb, cnt, cnt, ea, ea, x1, g_final.reshape(1, -1))


def _stream(x, left, attend, p):
    b, s, d = x.shape
    q, k, v, kb, vb, conv, tail = _in_proj(x, p["g_mix"], p["w_in"], left, p["conv_w"], p["conv_b"],
                                           p["ln_g"], p["ln_b"])
    att = attend(q, kb, vb)
    c = conv.shape[-1]
    x1, h2, st = _mid(conv.reshape(b * s, c), att.reshape(b * s, c), x.reshape(b * s, d),
                      p["w_out"], p["g_ffn"], p["w_query"], p["sub_keys"])
    rank2, cnt, ea, eb = _route(st)
    y = _peer(h2, p["peer_u"], p["peer_vt"], rank2, cnt, ea, eb, x1, p["g_final"])
    k = k.reshape(b, s, N_HEADS, 2, HEAD_DIM)
    v = v.reshape(b, s, N_HEADS, V_DIM)
    return y.reshape(b, s, d), k, v, tail[:, CONV_PAD - (CONV_WIDTH - 1):]


def kernel(x_prompt, x_sample, cache_k, cache_v, state_conv, g_mix, w_in, conv_w, conv_b, conv_ln_g, conv_ln_b, lambda_q1, lambda_k1, lambda_q2, lambda_k2, subln_g, rel_bias, w_out, g_ffn, w_query, sub_keys, peer_u, peer_v, g_final):
    depth = w_in.shape[0]
    assert depth == 1, "single-layer step"
    l = 0
    bf16 = jnp.bfloat16
    b, s, d = x_prompt.shape
    bd, sd, _ = x_sample.shape
    past = cache_k.shape[2]
    c = conv_w.shape[-1]

    lam_init = _lambda_init(l)
    lam = (jnp.exp(jnp.sum(lambda_q1[l].astype(jnp.float32) * lambda_k1[l].astype(jnp.float32)))
           - jnp.exp(jnp.sum(lambda_q2[l].astype(jnp.float32) * lambda_k2[l].astype(jnp.float32)))
           + lam_init).reshape(1)
    out_scale = 1.0 - lam_init

    p = {
        "g_mix": g_mix[l], "w_in": w_in[l].astype(bf16), "conv_w": conv_w[l], "conv_b": conv_b[l],
        "ln_g": conv_ln_g[l], "ln_b": conv_ln_b[l], "w_out": w_out[l].astype(bf16), "g_ffn": g_ffn[l],
        "w_query": w_query[l].astype(bf16),
        "sub_keys": sub_keys[l].reshape(2 * R_HEADS, N_KEYS, -1).astype(bf16),
        "peer_u": peer_u[l].astype(bf16), "peer_vt": peer_v[l].astype(bf16).T, "g_final": g_final,
    }

    attend_p = lambda q, kb, vb: _attn_prompt(q, kb, vb, rel_bias, lam, subln_g[l], out_scale)
    y_p, k_p, v_p, tail_p = _stream(x_prompt, jnp.zeros((b, CONV_PAD, c), jnp.float32), attend_p, p)

    n_new = -(-sd // LANES) * LANES
    pos_s = past + jnp.arange(sd, dtype=jnp.int32)
    bias_c = _bias_table(rel_bias, pos_s, jnp.arange(past, dtype=jnp.int32), True)
    bias_n = _bias_table(rel_bias, pos_s, past + jnp.arange(n_new, dtype=jnp.int32), True)
    bias_n = jnp.where(jnp.arange(n_new) < sd, bias_n, NEG)
    ck = cache_k[l].reshape(bd, past, c)
    cv = cache_v[l].reshape(bd, past, c)

    def attend_s(q, kb, vb):
        pad = ((0, 0), (0, n_new - sd), (0, 0))
        return _attn_sample(q, ck, cv, jnp.pad(kb, pad), jnp.pad(vb, pad), bias_c, bias_n,
                            lam, subln_g[l], out_scale)

    left_s = jnp.pad(state_conv[l], ((0, 0), (CONV_PAD - (CONV_WIDTH - 1), 0), (0, 0)))
    y_s, k_s, v_s, tail_s = _stream(x_sample, left_s, attend_s, p)

    return (y_p, y_s, k_p[None], v_p[None], tail_p[None], k_s[None], v_s[None], tail_s[None])
```

```python
import functools
import math

import jax
import jax.numpy as jnp
from jax import lax
from jax.experimental import pallas as pl
from jax.experimental.pallas import tpu as pltpu

CHUNK = 64
CONV_WIDTH = 31
CONV_PAD = 32
N_HEADS = 4
HEAD_DIM = 64
V_DIM = 2 * HEAD_DIM
N_BUCKETS = 32
MAX_DISTANCE = 128
N_KEYS = 128
R_HEADS = 8
TOPK = 16
EPS = 1e-6
NEG = -1e30
NEG_BIG = -3.0e38
LOG2_E = 1.4426950408889634
LANES = 128
SUBLANES = 8
PACKED_ROWS = 2 * SUBLANES
MXU_COLS = 256
VMEM_LIMIT = 48 * 1024 * 1024

_NT = (((1,), (1,)), ((), ()))


def _lambda_init(layer):
    return 0.8 - 0.6 * math.exp(-0.3 * layer)


def _rms(xf, g):
    return xf * lax.rsqrt(jnp.mean(xf * xf, axis=-1, keepdims=True) + EPS) * g


def _sigmoid(x):
    return 1.0 / (1.0 + jnp.exp(-x))


def _in_proj_kernel(x_ref, g_ref, w_ref, left_ref, cw_ref, cb_ref, lg_ref, lb_ref,
                    q_ref, k_ref, v_ref, kb_ref, vb_ref, conv_ref, tail_ref, abuf, *, ts, c):
    s = pl.program_id(1)
    h = _rms(x_ref[0], g_ref[...]).astype(jnp.bfloat16)

    @pl.when(s == 0)
    def _():
        abuf[0, 0:CONV_PAD, :] = left_ref[0]

    glu_in = jnp.dot(h, w_ref[:, 0:2 * c], preferred_element_type=jnp.float32)
    abuf[0, CONV_PAD:CONV_PAD + ts, :] = glu_in[:, :c] * _sigmoid(glu_in[:, c:])
    q = jnp.dot(h, w_ref[:, 2 * c:3 * c], preferred_element_type=jnp.float32)
    q_ref[0] = (q * (HEAD_DIM ** -0.5)).astype(jnp.bfloat16)
    k = jnp.dot(h, w_ref[:, 3 * c:4 * c], preferred_element_type=jnp.float32)
    k_ref[0] = k
    kb_ref[0] = k.astype(jnp.bfloat16)
    v = jnp.dot(h, w_ref[:, 4 * c:5 * c], preferred_element_type=jnp.float32)
    v_ref[0] = v
    vb_ref[0] = v.astype(jnp.bfloat16)

    n_sh = ts + CONV_PAD - SUBLANES
    for sh in range(1, SUBLANES):
        abuf[sh, 0:n_sh, :] = abuf[0, sh:sh + n_sh, :]
    rc = min(ts, 64)
    for r0 in range(0, ts, rc):
        acc = jnp.zeros((rc, c), jnp.float32)
        for w in range(CONV_WIDTH):
            off = r0 + CONV_PAD - (CONV_WIDTH - 1) + w
            sh = off % SUBLANES
            acc = acc + abuf[sh, off - sh:off - sh + rc, :] * cw_ref[w:w + 1, :]
        y = acc + cb_ref[...]
        mu = jnp.mean(y, axis=-1, keepdims=True)
        d = y - mu
        var = jnp.mean(d * d, axis=-1, keepdims=True)
        yn = d * lax.rsqrt(var + EPS) * lg_ref[...] + lb_ref[...]
        conv_ref[0, r0:r0 + rc, :] = (yn * _sigmoid(yn)).astype(jnp.bfloat16)

    tail = abuf[0, ts:ts + CONV_PAD, :]
    tail_ref[0] = tail
    abuf[0, 0:CONV_PAD, :] = tail


def _in_proj(x, g_mix, w_in_bf, left, conv_w, conv_b, ln_g, ln_b):
    b, s, d = x.shape
    c = conv_w.shape[1]
    ts = min(s, 512)
    assert s % ts == 0 and ts >= CONV_PAD and ts % SUBLANES == 0
    cw = jnp.pad(conv_w, ((0, CONV_PAD - CONV_WIDTH), (0, 0)))
    row = lambda a: a.reshape(1, -1)
    tok = lambda bi, si: (bi, si, 0)
    const2 = lambda bi, si: (0, 0)
    f32, bf16 = jnp.float32, jnp.bfloat16
    outs = pl.pallas_call(
        functools.partial(_in_proj_kernel, ts=ts, c=c),
        grid=(b, s // ts),
        in_specs=[
            pl.BlockSpec((1, ts, d), tok),
            pl.BlockSpec((1, d), const2),
            pl.BlockSpec(w_in_bf.shape, const2),
            pl.BlockSpec((1, CONV_PAD, c), lambda bi, si: (bi, 0, 0)),
            pl.BlockSpec((CONV_PAD, c), const2),
            pl.BlockSpec((1, c), const2),
            pl.BlockSpec((1, c), const2),
            pl.BlockSpec((1, c), const2),
        ],
        out_specs=[pl.BlockSpec((1, ts, c), tok)] * 6
        + [pl.BlockSpec((1, CONV_PAD, c), lambda bi, si: (bi, 0, 0))],
        out_shape=[
            jax.ShapeDtypeStruct((b, s, c), bf16),
            jax.ShapeDtypeStruct((b, s, c), f32),
            jax.ShapeDtypeStruct((b, s, c), f32),
            jax.ShapeDtypeStruct((b, s, c), bf16),
            jax.ShapeDtypeStruct((b, s, c), bf16),
            jax.ShapeDtypeStruct((b, s, c), bf16),
            jax.ShapeDtypeStruct((b, CONV_PAD, c), f32),
        ],
        scratch_shapes=[pltpu.VMEM((SUBLANES, ts + CONV_PAD, c), f32)],
        compiler_params=pltpu.CompilerParams(
            dimension_semantics=("arbitrary", "arbitrary"), vmem_limit_bytes=VMEM_LIMIT),
        name="in_proj",
    )(x, row(g_mix), w_in_bf, left, cw, row(conv_b), row(ln_g), row(ln_b))
    return outs


def _rel_bucket(rel):
    nb = N_BUCKETS // 2
    max_exact = nb // 2
    ret = jnp.where(rel > 0, nb, 0)
    n = jnp.abs(rel)
    nf = jnp.maximum(n, 1).astype(jnp.float32)
    large = max_exact + (jnp.log(nf / max_exact) / math.log(MAX_DISTANCE / max_exact)
                         * (nb - max_exact)).astype(jnp.int32)
    large = jnp.minimum(large, nb - 1)
    return ret + jnp.where(n < max_exact, n, large)


def _bias_table(rel_bias, q_pos, k_pos, masked):
    nq, nk = q_pos.shape[0], k_pos.shape[0]
    period = nq + nk
    m = jnp.arange(period, dtype=jnp.int32)
    bucket = _rel_bucket(k_pos[0] - q_pos[0] + jnp.where(m < nk, m, m - period))
    table = rel_bias.astype(jnp.float32).T[:, None, :]
    hit = bucket[None, :, None] == jnp.arange(N_BUCKETS, dtype=jnp.int32)
    line = jnp.sum(jnp.where(hit, table, 0.0), axis=-1)
    bias = jnp.tile(line, (1, nq))[:, :nq * (period - 1)].reshape(-1, nq, period - 1)[:, :, :nk]
    if masked:
        mask = (k_pos[None, :] // CHUNK) <= (q_pos[:, None] // CHUNK)
        bias = jnp.where(mask[None], bias, NEG)
    return bias


def _split_maps(q):
    lane = lax.broadcasted_iota(jnp.int32, q.shape, 1)
    zero = jnp.zeros_like(q)
    return jnp.where(lane < HEAD_DIM, q, zero), jnp.where(lane >= HEAD_DIM, q, zero)


def _softmax_rows(s):
    p = jnp.exp(s - jnp.max(s, axis=-1, keepdims=True))
    return p / jnp.sum(p, axis=-1, keepdims=True)


def _attn_finish(s1, s2, vv, lam, g, out_scale):
    attn = (_softmax_rows(s1) - lam * _softmax_rows(s2)).astype(jnp.bfloat16)
    o = jnp.dot(attn, vv, preferred_element_type=jnp.float32)
    return (_rms(o, g) * out_scale).astype(jnp.bfloat16)


def _attn_prompt_kernel(lam_ref, q_ref, k_ref, v_ref, slab_ref, g_ref, o_ref,
                        s_ref, m_ref, l_ref, acc_ref, *, tq, out_scale):
    qi = pl.program_id(2)
    q1, q2 = _split_maps(q_ref[0])
    width = m_ref.shape[-1]
    fold = lambda a, op: functools.reduce(op, [a[:, c:c + width] for c in range(0, tq, width)])

    m_ref[...] = jnp.full(m_ref.shape, NEG_BIG, jnp.float32)
    l_ref[...] = jnp.zeros_like(l_ref)
    acc_ref[...] = jnp.zeros_like(acc_ref)

    def logits(j, carry):
        kj = k_ref[0, pl.ds(pl.multiple_of(j * tq, tq), tq), :]
        bias = slab_ref[0, jnp.clip(j - qi, -2, 0) + 2]
        for mp, qm in enumerate((q1, q2)):
            a = (lax.dot_general(qm, kj, _NT, preferred_element_type=jnp.float32) + bias) * LOG2_E
            s_ref[mp, j] = a
            m_ref[mp] = jnp.maximum(m_ref[mp], fold(a, jnp.maximum))
        return carry

    lax.fori_loop(0, qi + 1, logits, 0)
    lanes = min(tq, LANES)
    row_max = [jnp.broadcast_to(jnp.max(m_ref[mp], axis=-1, keepdims=True), (tq, lanes)) for mp in range(2)]

    def accumulate(j, carry):
        vj = v_ref[0, pl.ds(pl.multiple_of(j * tq, tq), tq), :]
        for mp in range(2):
            p = jnp.concatenate([jnp.exp2(s_ref[mp, j, :, c * lanes:(c + 1) * lanes] - row_max[mp])
                                 for c in range(tq // lanes)], axis=1)
            l_ref[mp] += fold(p, jnp.add)
            acc_ref[mp] += jnp.dot(p.astype(jnp.bfloat16), vj, preferred_element_type=jnp.float32)
        return carry

    lax.fori_loop(0, qi + 1, accumulate, 0)
    inv = [1.0 / jnp.sum(l_ref[mp], axis=-1, keepdims=True) for mp in range(2)]
    o = acc_ref[0] * inv[0] - lam_ref[0] * (acc_ref[1] * inv[1])
    o_ref[0] = (_rms(o, g_ref[...]) * out_scale).astype(jnp.bfloat16)


def _attn_prompt(q, kb, vb, rel_bias, lam, subln_g, out_scale):
    b, s, c = q.shape
    tq = min(s, 512)
    assert s % tq == 0 and tq % CHUNK == 0 and tq >= MAX_DISTANCE and c == N_HEADS * V_DIM
    n_kb = s // tq
    pos = jnp.arange(tq, dtype=jnp.int32)
    slabs = jnp.stack([
        _bias_table(rel_bias, pos + 2 * tq, pos, False),
        _bias_table(rel_bias, pos + tq, pos, False),
        _bias_table(rel_bias, pos, pos, True),
    ], axis=1)
    width = min(tq, LANES)
    return pl.pallas_call(
        functools.partial(_attn_prompt_kernel, tq=tq, out_scale=out_scale),
        grid=(b, N_HEADS, n_kb),
        in_specs=[
            pl.BlockSpec(memory_space=pltpu.SMEM),
            pl.BlockSpec((1, tq, V_DIM), lambda bi, hi, qi: (bi, qi, hi)),
            pl.BlockSpec((1, s, V_DIM), lambda bi, hi, qi: (bi, 0, hi)),
            pl.BlockSpec((1, s, V_DIM), lambda bi, hi, qi: (bi, 0, hi)),
            pl.BlockSpec((1, 3, tq, tq), lambda bi, hi, qi: (hi, 0, 0, 0)),
            pl.BlockSpec((1, V_DIM), lambda bi, hi, qi: (0, 0)),
        ],
        out_specs=pl.BlockSpec((1, tq, V_DIM), lambda bi, hi, qi: (bi, qi, hi)),
        out_shape=jax.ShapeDtypeStruct((b, s, c), jnp.bfloat16),
        scratch_shapes=[
            pltpu.VMEM((2, n_kb, tq, tq), jnp.float32),
            pltpu.VMEM((2, tq, width), jnp.float32),
            pltpu.VMEM((2, tq, width), jnp.float32),
            pltpu.VMEM((2, tq, V_DIM), jnp.float32),
        ],
        compiler_params=pltpu.CompilerParams(
            dimension_semantics=("arbitrary",) * 3, vmem_limit_bytes=VMEM_LIMIT),
        name="attn_prompt",
    )(lam, q, kb, vb, slabs, subln_g.reshape(1, -1))


def _attn_sample_kernel(lam_ref, q_ref, ck_ref, cv_ref, kn_ref, vn_ref, bc_ref, bn_ref, g_ref, o_ref,
                        *, out_scale):
    q1, q2 = _split_maps(q_ref[0])
    ck = ck_ref[0].astype(jnp.bfloat16)
    kn = kn_ref[0]
    logits = lambda qm: jnp.concatenate([
        lax.dot_general(qm, ck, _NT, preferred_element_type=jnp.float32) + bc_ref[0],
        lax.dot_general(qm, kn, _NT, preferred_element_type=jnp.float32) + bn_ref[0]], axis=1)
    vv = jnp.concatenate([cv_ref[0].astype(jnp.bfloat16), vn_ref[0]], axis=0)
    o_ref[0] = _attn_finish(logits(q1), logits(q2), vv, lam_ref[0], g_ref[...], out_scale)


def _attn_sample(q, cache_k, cache_v, k_new, v_new, bias_c, bias_n, lam, subln_g, out_scale):
    b, sq, c = q.shape
    past, n_new = cache_k.shape[1], k_new.shape[1]
    per_head = lambda rows: pl.BlockSpec((1, rows, V_DIM), lambda bi, hi: (bi, 0, hi))
    return pl.pallas_call(
        functools.partial(_attn_sample_kernel, out_scale=out_scale),
        grid=(b, N_HEADS),
        in_specs=[
            pl.BlockSpec(memory_space=pltpu.SMEM),
            per_head(sq), per_head(past), per_head(past), per_head(n_new), per_head(n_new),
            pl.BlockSpec((1, sq, past), lambda bi, hi: (hi, 0, 0)),
            pl.BlockSpec((1, sq, n_new), lambda bi, hi: (hi, 0, 0)),
            pl.BlockSpec((1, V_DIM), lambda bi, hi: (0, 0)),
        ],
        out_specs=per_head(sq),
        out_shape=jax.ShapeDtypeStruct((b, sq, c), jnp.bfloat16),
        compiler_params=pltpu.CompilerParams(
            dimension_semantics=("arbitrary",) * 2, vmem_limit_bytes=VMEM_LIMIT),
        name="attn_sample",
    )(lam, q, cache_k, cache_v, k_new, v_new, bias_c, bias_n, subln_g.reshape(1, -1))


def _mid_kernel(conv_ref, att_ref, x_ref, wc_ref, wa_ref, g_ref, wq_ref, sk_ref,
                x1_ref, h2_ref, st_ref):
    x1 = (x_ref[...]
          + jnp.dot(conv_ref[...], wc_ref[...], preferred_element_type=jnp.float32)
          + jnp.dot(att_ref[...], wa_ref[...], preferred_element_type=jnp.float32))
    x1_ref[...] = x1
    h2 = _rms(x1, g_ref[...]).astype(jnp.bfloat16)
    h2_ref[...] = h2.T
    qq = jnp.dot(h2, wq_ref[...], preferred_element_type=jnp.float32).astype(jnp.bfloat16)
    for rp in range(2 * R_HEADS):
        st_ref[rp] = lax.dot_general(sk_ref[rp], qq[:, rp * N_KEYS:(rp + 1) * N_KEYS], _NT,
                                     preferred_element_type=jnp.float32)


def _mid(conv, att, x2d, w_out_bf, g_ffn, w_query_bf, sub_keys_bf):
    t, d = x2d.shape
    c = conv.shape[1]
    tb = min(t, 512)
    assert t % tb == 0
    dq = w_query_bf.shape[1]
    nrp = sub_keys_bf.shape[0]
    tok = lambda i: (i, 0)
    const = lambda i: (0, 0)
    return pl.pallas_call(
        _mid_kernel,
        grid=(t // tb,),
        in_specs=[
            pl.BlockSpec((tb, c), tok),
            pl.BlockSpec((tb, c), tok),
            pl.BlockSpec((tb, d), tok),
            pl.BlockSpec((c, d), const),
            pl.BlockSpec((c, d), lambda i: (1, 0)),
            pl.BlockSpec((1, d), const),
            pl.BlockSpec((d, dq), const),
            pl.BlockSpec(sub_keys_bf.shape, lambda i: (0, 0, 0)),
        ],
        out_specs=[
            pl.BlockSpec((tb, d), tok),
            pl.BlockSpec((d, tb), lambda i: (0, i)),
            pl.BlockSpec((nrp, N_KEYS, tb), lambda i: (0, 0, i)),
        ],
        out_shape=[
            jax.ShapeDtypeStruct((t, d), jnp.float32),
            jax.ShapeDtypeStruct((d, t), jnp.bfloat16),
            jax.ShapeDtypeStruct((nrp, N_KEYS, t), jnp.float32),
        ],
        compiler_params=pltpu.CompilerParams(
            dimension_semantics=("arbitrary",), vmem_limit_bytes=VMEM_LIMIT),
        name="mid",
    )(conv, att, x2d, w_out_bf, w_out_bf, g_ffn.reshape(1, -1), w_query_bf, sub_keys_bf)


def _ce(a, b):
    if a is None:
        return b, None
    if b is None:
        return a, None
    return jnp.maximum(a, b), jnp.minimum(a, b)


def _sort_desc(xs):
    xs = list(xs)
    n = len(xs)
    p = 1
    while p < n:
        k = p
        while k >= 1:
            for j in range(k % p, n - k, 2 * k):
                for i in range(min(k, n - j - k)):
                    if (i + j) // (2 * p) == (i + j + k) // (2 * p):
                        xs[i + j], xs[i + j + k] = _ce(xs[i + j], xs[i + j + k])
            k //= 2
        p *= 2
    return xs


def _bitonic_top(a, b):
    n = len(a)
    return [_ce(a[i], b[n - 1 - i])[0] for i in range(n)]


def _bitonic_sort_desc(xs):
    xs = list(xs)
    n = len(xs)
    d = n // 2
    while d >= 1:
        for i in range(n):
            if i & d == 0:
                xs[i], xs[i + d] = _ce(xs[i], xs[i + d])
        d //= 2
    return xs


def _fill(xs):
    return [jnp.full((SUBLANES, LANES), NEG_BIG, jnp.float32) if x is None else x for x in xs]


def _sublane_merge_sorted(xs):
    for shift in (4, 2, 1):
        other = [pltpu.roll(x, shift, 0) for x in xs]
        xs = _bitonic_sort_desc(_bitonic_top(xs, other))
    return xs


def _sublane_merge_kth(xs):
    for shift in (4, 2):
        other = [pltpu.roll(x, shift, 0) for x in xs]
        xs = _bitonic_sort_desc(_bitonic_top(xs, other))
    other = [pltpu.roll(x, 1, 0) for x in xs]
    top = _bitonic_top(xs, other)
    return functools.reduce(jnp.minimum, top)


def _top16_rows(s):
    tiles = [s[i * SUBLANES:(i + 1) * SUBLANES, :] for i in range(N_KEYS // SUBLANES)]
    return _sublane_merge_sorted(_sort_desc(tiles))


def _dup_bf16(x):
    b = pltpu.bitcast(x.astype(jnp.bfloat16).astype(jnp.float32), jnp.uint32)
    return b | (b >> 16)


def _route_kernel(st_ref, rank_ref, n_ref, ea_ref, eb_ref, *, tb):
    sub = lax.broadcasted_iota(jnp.int32, (SUBLANES, LANES), 0)
    for g in range(tb // LANES):
        cols = slice(g * LANES, (g + 1) * LANES)
        for r in range(R_HEADS):
            s1 = st_ref[2 * r, :, cols]
            s2 = st_ref[2 * r + 1, :, cols]
            v1 = _top16_rows(s1)
            v2 = _top16_rows(s2)
            pack = lambda v, base: functools.reduce(
                lambda acc, j: jnp.where(sub == j, v[base + j], acc), range(1, SUBLANES), v[base])
            w1a, w1b, w2a, w2b = pack(v1, 0), pack(v1, 8), pack(v2, 0), pack(v2, 8)
            neg = jnp.full((SUBLANES, LANES), NEG_BIG, jnp.float32)
            cands = [
                v1[0] + w2a,
                v1[0] + w2b,
                jnp.where(sub >= 1, v2[0] + w1a, neg),
                v2[0] + w1b,
                jnp.where(sub >= 1, v1[1] + w2a, neg),
                jnp.where(sub >= 2, v2[1] + w1a, neg),
                jnp.where((sub >= 2) & (sub <= 4), v1[2] + w2a, neg),
                jnp.where((sub >= 2) & (sub <= 3), v1[3] + w2a, neg),
                jnp.where(sub == 2, v1[4] + w2a, neg),
            ]
            srt = _fill(_sort_desc(cands + [None] * (TOPK - len(cands))))
            thr = _sublane_merge_kth(srt)
            m1, m2 = v1[0], v2[0]
            top = m1 + m2
            z = functools.reduce(
                lambda acc, cnd: acc + jnp.where(cnd >= thr, jnp.exp(cnd - top), 0.0), cands,
                jnp.zeros((SUBLANES, LANES), jnp.float32))
            for shift in (4, 2, 1):
                z = z + pltpu.roll(z, shift, 0)
            rank2 = jnp.zeros((N_KEYS, LANES), jnp.float32)
            cnt = jnp.zeros((N_KEYS, LANES), jnp.float32)
            thr_row = thr[0:1, :]
            for j in range(TOPK):
                rank2 = jnp.where(v2[j][0:1, :] > s2, j + 1.0, rank2)
            for j in range(TOPK // 2):
                cnt = jnp.where(s1 + v2[j][0:1, :] >= thr_row, j + 1.0, cnt)
            cnt_top = jnp.zeros((SUBLANES, LANES), jnp.float32)
            for j in range(TOPK // 2, TOPK):
                cnt_top = jnp.where(v1[0] + v2[j] >= thr, j + 1.0, cnt_top)
            cnt = jnp.maximum(cnt, jnp.where(s1 == v1[0][0:1, :], cnt_top[0:1, :], 0.0))
            n_ref[r, :, cols] = _dup_bf16(cnt)
            ea_ref[r, :, cols] = _dup_bf16(jnp.exp(s1 - m1[0:1, :]) / z[0:1, :])
            eb_ref[r, :, cols] = jnp.exp(s2 - m2[0:1, :]).astype(jnp.bfloat16)
            rank_ref[r, :, cols] = rank2.astype(jnp.bfloat16)


def _route(st):
    nrp, nk, t = st.shape
    tb = min(t, 256)
    assert t % tb == 0 and tb % LANES == 0 and nk == N_KEYS and nrp == 2 * R_HEADS
    row_spec = pl.BlockSpec((R_HEADS, nk, tb), lambda i: (0, 0, i))
    return pl.pallas_call(
        functools.partial(_route_kernel, tb=tb),
        grid=(t // tb,),
        in_specs=[pl.BlockSpec((nrp, nk, tb), lambda i: (0, 0, i))],
        out_specs=[row_spec] * 4,
        out_shape=[
            jax.ShapeDtypeStruct((R_HEADS, nk, t), jnp.bfloat16),
            jax.ShapeDtypeStruct((R_HEADS, nk, t), jnp.uint32),
            jax.ShapeDtypeStruct((R_HEADS, nk, t), jnp.uint32),
            jax.ShapeDtypeStruct((R_HEADS, nk, t), jnp.bfloat16),
        ],
        compiler_params=pltpu.CompilerParams(
            dimension_semantics=("arbitrary",), vmem_limit_bytes=VMEM_LIMIT),
        name="route",
    )(st)


def _gelu(x):
    hx = 0.5 * x
    return hx + hx * lax.erf(x * (2.0 ** -0.5))


def _packed_row(ref, r, row, cols):
    tile = jnp.broadcast_to(ref[r, row:row + 1, cols], (SUBLANES, LANES))
    return pltpu.bitcast(tile, jnp.bfloat16)


def _peer_kernel(h_ref, u_ref, vt_ref, rank_ref, eb_ref, n_odd_ref, n_even_ref, ea_odd_ref, ea_even_ref,
                 x1_ref, g_ref, y_ref, acc_ref, act_a, act_b, coef_a, coef_b, *, eb_rows, tb):
    j, s = pl.program_id(0), pl.program_id(1)
    n_rows, last = pl.num_programs(0), pl.num_programs(1) - 1

    @pl.when((s == 0) & (j > 0))
    def _():
        acc_ref[...] = jnp.zeros_like(acc_ref)

    n_tg = tb // LANES
    tn = min(tb, MXU_COLS)

    def accumulate(half, coef_ref, piece):
        vt = vt_ref[:, half * eb_rows:(half + 1) * eb_rows]
        cols = slice(piece * tn, (piece + 1) * tn)
        acc_ref[:, cols] += jnp.dot(vt, coef_ref[:, cols], preferred_element_type=jnp.float32)

    def gate(cnt_ref, gain_ref, act_ref, coef_ref, tg):
        zero = jnp.zeros((PACKED_ROWS, LANES), jnp.bfloat16)
        cols = slice(tg * LANES, (tg + 1) * LANES)
        for ci in range(eb_rows // N_KEYS):
            cnt = [_packed_row(cnt_ref, r, ci, cols) for r in range(R_HEADS)]
            ea = [_packed_row(gain_ref, r, ci, cols) for r in range(R_HEADS)]
            for ch in range(N_KEYS // PACKED_ROWS):
                keys = slice(ch * PACKED_ROWS, (ch + 1) * PACKED_ROWS)
                g = zero
                for r in range(R_HEADS):
                    sel = jnp.minimum(jnp.maximum(cnt[r] - rank_ref[r, keys, cols], 0), 1)
                    g = g + (ea[r] * sel) * eb_ref[r, keys, cols]
                rows = slice(ci * N_KEYS + ch * PACKED_ROWS, ci * N_KEYS + (ch + 1) * PACKED_ROWS)
                coef_ref[rows, cols] = g * _gelu(act_ref[rows, cols].astype(jnp.bfloat16))

    def activate(half, act_ref, piece):
        u = u_ref[half * eb_rows:(half + 1) * eb_rows, :]
        cols = slice(piece * tn, (piece + 1) * tn)
        act_ref[:, cols] = jnp.dot(u, h_ref[:, cols], preferred_element_type=jnp.float32)

    n_piece = tb // tn

    per = n_tg // n_piece

    halves = ((n_even_ref, ea_even_ref, act_a, coef_a), (n_odd_ref, ea_odd_ref, act_b, coef_b))

    def step(drain, fill, finish):
        for half, (cnt_ref, gain_ref, act_ref, coef_ref) in enumerate(halves):
            for piece in range(n_piece):
                if drain:
                    for tg in range(piece * per, (piece + 1) * per):
                        gate(cnt_ref, gain_ref, act_ref, coef_ref, tg)
                    accumulate(half, coef_ref, piece)
                if fill:
                    activate(half, act_ref, piece)
        if finish:
            y_ref[...] = _rms(x1_ref[...] + acc_ref[...].T, g_ref[...])

    pl.when((j > 0) & (s < last))(functools.partial(step, True, True, False))
    pl.when((j > 0) & (j < n_rows - 1) & (s == last))(functools.partial(step, True, True, True))
    pl.when((j == n_rows - 1) & (s == last))(functools.partial(step, True, False, True))
    pl.when((j == 0) & (s == last))(functools.partial(step, False, True, False))


def _peer(h2, u_bf, vt_bf, rank2, cnt, ea, eb, x1, g_final):
    d, t = h2.shape
    n_exp = u_bf.shape[0]
    tb = min(t, 512)
    eb_rows = SUBLANES * N_KEYS
    assert t % tb == 0 and n_exp % (2 * eb_rows) == 0 and n_exp == N_KEYS * N_KEYS
    n_tb = t // tb
    n_steps = n_exp // (2 * eb_rows)
    drained = lambda j: jnp.clip(j - 1, 0, n_tb - 1)
    filled = lambda j, s: jnp.where(s == n_steps - 1, jnp.minimum(j, n_tb - 1), drained(j))
    tok = lambda j, s: (drained(j), 0)
    tile_spec = pl.BlockSpec((R_HEADS, N_KEYS, tb), lambda j, s: (0, 0, drained(j)))
    even_spec = pl.BlockSpec((R_HEADS, SUBLANES, tb), lambda j, s: (0, 2 * s, drained(j)))
    odd_spec = pl.BlockSpec((R_HEADS, SUBLANES, tb), lambda j, s: (0, 2 * s + 1, drained(j)))
    return pl.pallas_call(
        functools.partial(_peer_kernel, eb_rows=eb_rows, tb=tb),
        grid=(n_tb + 1, n_steps),
        in_specs=[
            pl.BlockSpec((d, tb), lambda j, s: (0, filled(j, s))),
            pl.BlockSpec((2 * eb_rows, d), lambda j, s: ((s + 1) % n_steps, 0)),
            pl.BlockSpec((d, 2 * eb_rows), lambda j, s: (0, s)),
            tile_spec, tile_spec, odd_spec, even_spec, odd_spec, even_spec,
            pl.BlockSpec((tb, d), tok),
            pl.BlockSpec((1, d), lambda j, s: (0, 0)),
        ],
        out_specs=pl.BlockSpec((tb, d), tok),
        out_shape=jax.ShapeDtypeStruct((t, d), jnp.float32),
        scratch_shapes=[
            pltpu.VMEM((d, tb), jnp.float32),
            pltpu.VMEM((eb_rows, tb), jnp.float32),
            pltpu.VMEM((eb_rows, tb), jnp.float32),
            pltpu.VMEM((eb_rows, tb), jnp.bfloat16),
            pltpu.VMEM((eb_rows, tb), jnp.bfloat16),
        ],
        compiler_params=pltpu.CompilerParams(
            dimension_semantics=("arbitrary", "arbitrary"), vmem_limit_bytes=VMEM_LIMIT),
        name="peer",
    )(h2, u_bf, vt_bf, rank2, eb, cnt, cnt, ea, ea, x1, g_final.reshape(1, -1))


def _stream(x, left, attend, p):
    b, s, d = x.shape
    q, k, v, kb, vb, conv, tail = _in_proj(x, p["g_mix"], p["w_in"], left, p["conv_w"], p["conv_b"],
                                           p["ln_g"], p["ln_b"])
    att = attend(q, kb, vb)
    c = conv.shape[-1]
    x1, h2, st = _mid(conv.reshape(b * s, c), att.reshape(b * s, c), x.reshape(b * s, d),
                      p["w_out"], p["g_ffn"], p["w_query"], p["sub_keys"])
    rank2, cnt, ea, eb = _route(st)
    y = _peer(h2, p["peer_u"], p["peer_vt"], rank2, cnt, ea, eb, x1, p["g_final"])
    k = k.reshape(b, s, N_HEADS, 2, HEAD_DIM)
    v = v.reshape(b, s, N_HEADS, V_DIM)
    return y.reshape(b, s, d), k, v, tail[:, CONV_PAD - (CONV_WIDTH - 1):]


def kernel(x_prompt, x_sample, cache_k, cache_v, state_conv, g_mix, w_in, conv_w, conv_b, conv_ln_g, conv_ln_b, lambda_q1, lambda_k1, lambda_q2, lambda_k2, subln_g, rel_bias, w_out, g_ffn, w_query, sub_keys, peer_u, peer_v, g_final):
    depth = w_in.shape[0]
    assert depth == 1, "single-layer step"
    l = 0
    bf16 = jnp.bfloat16
    b, s, d = x_prompt.shape
    bd, sd, _ = x_sample.shape
    past = cache_k.shape[2]
    c = conv_w.shape[-1]

    lam_init = _lambda_init(l)
    lam = (jnp.exp(jnp.sum(lambda_q1[l].astype(jnp.float32) * lambda_k1[l].astype(jnp.float32)))
           - jnp.exp(jnp.sum(lambda_q2[l].astype(jnp.float32) * lambda_k2[l].astype(jnp.float32)))
           + lam_init).reshape(1)
    out_scale = 1.0 - lam_init

    p = {
        "g_mix": g_mix[l], "w_in": w_in[l].astype(bf16), "conv_w": conv_w[l], "conv_b": conv_b[l],
        "ln_g": conv_ln_g[l], "ln_b": conv_ln_b[l], "w_out": w_out[l].astype(bf16), "g_ffn": g_ffn[l],
        "w_query": w_query[l].astype(bf16),
        "sub_keys": sub_keys[l].reshape(2 * R_HEADS, N_KEYS, -1).astype(bf16),
        "peer_u": peer_u[l].astype(bf16), "peer_vt": peer_v[l].astype(bf16).T, "g_final": g_final,
    }

    attend_p = lambda q, kb, vb: _attn_prompt(q, kb, vb, rel_bias, lam, subln_g[l], out_scale)
    y_p, k_p, v_p, tail_p = _stream(x_prompt, jnp.zeros((b, CONV_PAD, c), jnp.float32), attend_p, p)

    n_new = -(-sd // LANES) * LANES
    pos_s = past + jnp.arange(sd, dtype=jnp.int32)
    bias_c = _bias_table(rel_bias, pos_s, jnp.arange(past, dtype=jnp.int32), True)
    bias_n = _bias_table(rel_bias, pos_s, past + jnp.arange(n_new, dtype=jnp.int32), True)
    bias_n = jnp.where(jnp.arange(n_new) < sd, bias_n, NEG)
    ck = cache_k[l].reshape(bd, past, c)
    cv = cache_v[l].reshape(bd, past, c)

    def attend_s(q, kb, vb):
        pad = ((0, 0), (0, n_new - sd), (0, 0))
        return _attn_sample(q, ck, cv, jnp.pad(kb, pad), jnp.pad(vb, pad), bias_c, bias_n,
                            lam, subln_g[l], out_scale)

    left_s = jnp.pad(state_conv[l], ((0, 0), (CONV_PAD - (CONV_WIDTH - 1), 0), (0, 0)))
    y_s, k_s, v_s, tail_s = _stream(x_sample, left_s, attend_s, p)

    return (y_p, y_s, k_p[None], v_p[None], tail_p[None], k_s[None], v_s[None], tail_s[None])
```

```python
import functools
import math

import jax
import jax.numpy as jnp
from jax import lax
from jax.experimental import pallas as pl
from jax.experimental.pallas import tpu as pltpu

CHUNK = 64
CONV_WIDTH = 31
CONV_PAD = 32
N_HEADS = 4
HEAD_DIM = 64
V_DIM = 2 * HEAD_DIM
N_BUCKETS = 32
MAX_DISTANCE = 128
N_KEYS = 128
R_HEADS = 8
TOPK = 16
EPS = 1e-6
NEG = -1e30
NEG_BIG = -3.0e38
LOG2_E = 1.4426950408889634
LANES = 128
SUBLANES = 8
PACKED_ROWS = 2 * SUBLANES
MXU_COLS = 256
VMEM_LIMIT = 48 * 1024 * 1024

_NT = (((1,), (1,)), ((), ()))


def _lambda_init(layer):
    return 0.8 - 0.6 * math.exp(-0.3 * layer)


def _rms(xf, g):
    return xf * lax.rsqrt(jnp.mean(xf * xf, axis=-1, keepdims=True) + EPS) * g


def _sigmoid(x):
    return 1.0 / (1.0 + jnp.exp(-x))


def _in_proj_kernel(x_ref, g_ref, w_ref, left_ref, cw_ref, cb_ref, lg_ref, lb_ref,
                    q_ref, k_ref, v_ref, kb_ref, vb_ref, conv_ref, tail_ref, abuf, *, ts, c):
    s = pl.program_id(1)
    h = _rms(x_ref[0], g_ref[...]).astype(jnp.bfloat16)

    @pl.when(s == 0)
    def _():
        abuf[0, 0:CONV_PAD, :] = left_ref[0]

    glu_in = jnp.dot(h, w_ref[:, 0:2 * c], preferred_element_type=jnp.float32)
    abuf[0, CONV_PAD:CONV_PAD + ts, :] = glu_in[:, :c] * _sigmoid(glu_in[:, c:])
    q = jnp.dot(h, w_ref[:, 2 * c:3 * c], preferred_element_type=jnp.float32)
    q_ref[0] = (q * (HEAD_DIM ** -0.5)).astype(jnp.bfloat16)
    k = jnp.dot(h, w_ref[:, 3 * c:4 * c], preferred_element_type=jnp.float32)
    k_ref[0] = k
    kb_ref[0] = k.astype(jnp.bfloat16)
    v = jnp.dot(h, w_ref[:, 4 * c:5 * c], preferred_element_type=jnp.float32)
    v_ref[0] = v
    vb_ref[0] = v.astype(jnp.bfloat16)

    n_sh = ts + CONV_PAD - SUBLANES
    for sh in range(1, SUBLANES):
        abuf[sh, 0:n_sh, :] = abuf[0, sh:sh + n_sh, :]
    rc = min(ts, 64)
    for r0 in range(0, ts, rc):
        acc = jnp.zeros((rc, c), jnp.float32)
        for w in range(CONV_WIDTH):
            off = r0 + CONV_PAD - (CONV_WIDTH - 1) + w
            sh = off % SUBLANES
            acc = acc + abuf[sh, off - sh:off - sh + rc, :] * cw_ref[w:w + 1, :]
        y = acc + cb_ref[...]
        mu = jnp.mean(y, axis=-1, keepdims=True)
        d = y - mu
        var = jnp.mean(d * d, axis=-1, keepdims=True)
        yn = d * lax.rsqrt(var + EPS) * lg_ref[...] + lb_ref[...]
        conv_ref[0, r0:r0 + rc, :] = (yn * _sigmoid(yn)).astype(jnp.bfloat16)

    tail = abuf[0, ts:ts + CONV_PAD, :]
    tail_ref[0] = tail
    abuf[0, 0:CONV_PAD, :] = tail


def _in_proj(x, g_mix, w_in_bf, left, conv_w, conv_b, ln_g, ln_b):
    b, s, d = x.shape
    c = conv_w.shape[1]
    ts = min(s, 512)
    assert s % ts == 0 and ts >= CONV_PAD and ts % SUBLANES == 0
    cw = jnp.pad(conv_w, ((0, CONV_PAD - CONV_WIDTH), (0, 0)))
    row = lambda a: a.reshape(1, -1)
    tok = lambda bi, si: (bi, si, 0)
    const2 = lambda bi, si: (0, 0)
    f32, bf16 = jnp.float32, jnp.bfloat16
    outs = pl.pallas_call(
        functools.partial(_in_proj_kernel, ts=ts, c=c),
        grid=(b, s // ts),
        in_specs=[
            pl.BlockSpec((1, ts, d), tok),
            pl.BlockSpec((1, d), const2),
            pl.BlockSpec(w_in_bf.shape, const2),
            pl.BlockSpec((1, CONV_PAD, c), lambda bi, si: (bi, 0, 0)),
            pl.BlockSpec((CONV_PAD, c), const2),
            pl.BlockSpec((1, c), const2),
            pl.BlockSpec((1, c), const2),
            pl.BlockSpec((1, c), const2),
        ],
        out_specs=[pl.BlockSpec((1, ts, c), tok)] * 6
        + [pl.BlockSpec((1, CONV_PAD, c), lambda bi, si: (bi, 0, 0))],
        out_shape=[
            jax.ShapeDtypeStruct((b, s, c), bf16),
            jax.ShapeDtypeStruct((b, s, c), f32),
            jax.ShapeDtypeStruct((b, s, c), f32),
            jax.ShapeDtypeStruct((b, s, c), bf16),
            jax.ShapeDtypeStruct((b, s, c), bf16),
            jax.ShapeDtypeStruct((b, s, c), bf16),
            jax.ShapeDtypeStruct((b, CONV_PAD, c), f32),
        ],
        scratch_shapes=[pltpu.VMEM((SUBLANES, ts + CONV_PAD, c), f32)],
        compiler_params=pltpu.CompilerParams(
            dimension_semantics=("arbitrary", "arbitrary"), vmem_limit_bytes=VMEM_LIMIT),
        name="in_proj",
    )(x, row(g_mix), w_in_bf, left, cw, row(conv_b), row(ln_g), row(ln_b))
    return outs


def _rel_bucket(rel):
    nb = N_BUCKETS // 2
    max_exact = nb // 2
    ret = jnp.where(rel > 0, nb, 0)
    n = jnp.abs(rel)
    nf = jnp.maximum(n, 1).astype(jnp.float32)
    large = max_exact + (jnp.log(nf / max_exact) / math.log(MAX_DISTANCE / max_exact)
                         * (nb - max_exact)).astype(jnp.int32)
    large = jnp.minimum(large, nb - 1)
    return ret + jnp.where(n < max_exact, n, large)


def _bias_table(rel_bias, q_pos, k_pos, masked):
    nq, nk = q_pos.shape[0], k_pos.shape[0]
    period = nq + nk
    m = jnp.arange(period, dtype=jnp.int32)
    bucket = _rel_bucket(k_pos[0] - q_pos[0] + jnp.where(m < nk, m, m - period))
    table = rel_bias.astype(jnp.float32).T[:, None, :]
    hit = bucket[None, :, None] == jnp.arange(N_BUCKETS, dtype=jnp.int32)
    line = jnp.sum(jnp.where(hit, table, 0.0), axis=-1)
    bias = jnp.tile(line, (1, nq))[:, :nq * (period - 1)].reshape(-1, nq, period - 1)[:, :, :nk]
    if masked:
        mask = (k_pos[None, :] // CHUNK) <= (q_pos[:, None] // CHUNK)
        bias = jnp.where(mask[None], bias, NEG)
    return bias


def _split_maps(q):
    lane = lax.broadcasted_iota(jnp.int32, q.shape, 1)
    zero = jnp.zeros_like(q)
    return jnp.where(lane < HEAD_DIM, q, zero), jnp.where(lane >= HEAD_DIM, q, zero)


def _softmax_rows(s):
    p = jnp.exp(s - jnp.max(s, axis=-1, keepdims=True))
    return p / jnp.sum(p, axis=-1, keepdims=True)


def _attn_finish(s1, s2, vv, lam, g, out_scale):
    attn = (_softmax_rows(s1) - lam * _softmax_rows(s2)).astype(jnp.bfloat16)
    o = jnp.dot(attn, vv, preferred_element_type=jnp.float32)
    return (_rms(o, g) * out_scale).astype(jnp.bfloat16)


def _attn_prompt_kernel(lam_ref, q_ref, k_ref, v_ref, slab_ref, g_ref, o_ref,
                        s_ref, m_ref, l_ref, acc_ref, *, tq, n_hd, out_scale):
    qi = pl.program_id(2)
    head = lambda hh: slice(hh * V_DIM, (hh + 1) * V_DIM)
    qs = [qm for hh in range(n_hd) for qm in _split_maps(q_ref[0, :, head(hh)])]
    width = m_ref.shape[-1]
    fold = lambda a, op: functools.reduce(op, [a[:, c:c + width] for c in range(0, tq, width)])

    m_ref[...] = jnp.full(m_ref.shape, NEG_BIG, jnp.float32)
    l_ref[...] = jnp.zeros_like(l_ref)
    acc_ref[...] = jnp.zeros_like(acc_ref)

    def logits(j, carry):
        rows = pl.ds(pl.multiple_of(j * tq, tq), tq)
        kind = jnp.clip(j - qi, -2, 0) + 2
        for hh in range(n_hd):
            kj = k_ref[0, rows, head(hh)]
            bias = slab_ref[hh, kind]
            for mp in range(2):
                a = (lax.dot_general(qs[2 * hh + mp], kj, _NT, preferred_element_type=jnp.float32) + bias) * LOG2_E
                s_ref[2 * hh + mp, j] = a
                m_ref[2 * hh + mp] = jnp.maximum(m_ref[2 * hh + mp], fold(a, jnp.maximum))
        return carry

    lax.fori_loop(0, qi + 1, logits, 0)
    lanes = min(tq, LANES)
    row_max = [jnp.broadcast_to(jnp.max(m_ref[i], axis=-1, keepdims=True), (tq, lanes)) for i in range(2 * n_hd)]

    def accumulate(j, carry):
        rows = pl.ds(pl.multiple_of(j * tq, tq), tq)
        for hh in range(n_hd):
            vj = v_ref[0, rows, head(hh)]
            for i in (2 * hh, 2 * hh + 1):
                p = jnp.concatenate([jnp.exp2(s_ref[i, j, :, c * lanes:(c + 1) * lanes] - row_max[i])
                                     for c in range(tq // lanes)], axis=1)
                l_ref[i] += fold(p, jnp.add)
                acc_ref[i] += jnp.dot(p.astype(jnp.bfloat16), vj, preferred_element_type=jnp.float32)
        return carry

    lax.fori_loop(0, qi + 1, accumulate, 0)
    for hh in range(n_hd):
        inv = [1.0 / jnp.sum(l_ref[2 * hh + mp], axis=-1, keepdims=True) for mp in range(2)]
        o = acc_ref[2 * hh] * inv[0] - lam_ref[0] * (acc_ref[2 * hh + 1] * inv[1])
        o_ref[0, :, head(hh)] = (_rms(o, g_ref[...]) * out_scale).astype(jnp.bfloat16)


def _attn_prompt(q, kb, vb, rel_bias, lam, subln_g, out_scale):
    b, s, c = q.shape
    tq = min(s, 512)
    assert s % tq == 0 and tq % CHUNK == 0 and tq >= MAX_DISTANCE and c == N_HEADS * V_DIM
    n_kb = s // tq
    pos = jnp.arange(tq, dtype=jnp.int32)
    slabs = jnp.stack([
        _bias_table(rel_bias, pos + 2 * tq, pos, False),
        _bias_table(rel_bias, pos + tq, pos, False),
        _bias_table(rel_bias, pos, pos, True),
    ], axis=1)
    width = min(tq, LANES)
    n_hd = 2
    assert N_HEADS % n_hd == 0
    return pl.pallas_call(
        functools.partial(_attn_prompt_kernel, tq=tq, n_hd=n_hd, out_scale=out_scale),
        grid=(b, N_HEADS // n_hd, n_kb),
        in_specs=[
            pl.BlockSpec(memory_space=pltpu.SMEM),
            pl.BlockSpec((1, tq, n_hd * V_DIM), lambda bi, hi, qi: (bi, qi, hi)),
            pl.BlockSpec((1, s, n_hd * V_DIM), lambda bi, hi, qi: (bi, 0, hi)),
            pl.BlockSpec((1, s, n_hd * V_DIM), lambda bi, hi, qi: (bi, 0, hi)),
            pl.BlockSpec((n_hd, 3, tq, tq), lambda bi, hi, qi: (hi, 0, 0, 0)),
            pl.BlockSpec((1, V_DIM), lambda bi, hi, qi: (0, 0)),
        ],
        out_specs=pl.BlockSpec((1, tq, n_hd * V_DIM), lambda bi, hi, qi: (bi, qi, hi)),
        out_shape=jax.ShapeDtypeStruct((b, s, c), jnp.bfloat16),
        scratch_shapes=[
            pltpu.VMEM((2 * n_hd, n_kb, tq, tq), jnp.float32),
            pltpu.VMEM((2 * n_hd, tq, width), jnp.float32),
            pltpu.VMEM((2 * n_hd, tq, width), jnp.float32),
            pltpu.VMEM((2 * n_hd, tq, V_DIM), jnp.float32),
        ],
        compiler_params=pltpu.CompilerParams(
            dimension_semantics=("arbitrary",) * 3, vmem_limit_bytes=VMEM_LIMIT),
        name="attn_prompt",
    )(lam, q, kb, vb, slabs, subln_g.reshape(1, -1))


def _attn_sample_kernel(lam_ref, q_ref, ck_ref, cv_ref, kn_ref, vn_ref, bc_ref, bn_ref, g_ref, o_ref,
                        *, out_scale):
    q1, q2 = _split_maps(q_ref[0])
    ck = ck_ref[0].astype(jnp.bfloat16)
    kn = kn_ref[0]
    logits = lambda qm: jnp.concatenate([
        lax.dot_general(qm, ck, _NT, preferred_element_type=jnp.float32) + bc_ref[0],
        lax.dot_general(qm, kn, _NT, preferred_element_type=jnp.float32) + bn_ref[0]], axis=1)
    vv = jnp.concatenate([cv_ref[0].astype(jnp.bfloat16), vn_ref[0]], axis=0)
    o_ref[0] = _attn_finish(logits(q1), logits(q2), vv, lam_ref[0], g_ref[...], out_scale)


def _attn_sample(q, cache_k, cache_v, k_new, v_new, bias_c, bias_n, lam, subln_g, out_scale):
    b, sq, c = q.shape
    past, n_new = cache_k.shape[1], k_new.shape[1]
    per_head = lambda rows: pl.BlockSpec((1, rows, V_DIM), lambda bi, hi: (bi, 0, hi))
    return pl.pallas_call(
        functools.partial(_attn_sample_kernel, out_scale=out_scale),
        grid=(b, N_HEADS),
        in_specs=[
            pl.BlockSpec(memory_space=pltpu.SMEM),
            per_head(sq), per_head(past), per_head(past), per_head(n_new), per_head(n_new),
            pl.BlockSpec((1, sq, past), lambda bi, hi: (hi, 0, 0)),
            pl.BlockSpec((1, sq, n_new), lambda bi, hi: (hi, 0, 0)),
            pl.BlockSpec((1, V_DIM), lambda bi, hi: (0, 0)),
        ],
        out_specs=per_head(sq),
        out_shape=jax.ShapeDtypeStruct((b, sq, c), jnp.bfloat16),
        compiler_params=pltpu.CompilerParams(
            dimension_semantics=("arbitrary",) * 2, vmem_limit_bytes=VMEM_LIMIT),
        name="attn_sample",
    )(lam, q, cache_k, cache_v, k_new, v_new, bias_c, bias_n, subln_g.reshape(1, -1))


def _mid_kernel(conv_ref, att_ref, x_ref, wc_ref, wa_ref, g_ref, wq_ref, sk_ref,
                x1_ref, h2_ref, st_ref):
    x1 = (x_ref[...]
          + jnp.dot(conv_ref[...], wc_ref[...], preferred_element_type=jnp.float32)
          + jnp.dot(att_ref[...], wa_ref[...], preferred_element_type=jnp.float32))
    x1_ref[...] = x1
    h2 = _rms(x1, g_ref[...]).astype(jnp.bfloat16)
    h2_ref[...] = h2
    qq = jnp.dot(h2, wq_ref[...], preferred_element_type=jnp.float32).astype(jnp.bfloat16)
    for rp in range(2 * R_HEADS):
        st_ref[rp] = lax.dot_general(sk_ref[rp], qq[:, rp * N_KEYS:(rp + 1) * N_KEYS], _NT,
                                     preferred_element_type=jnp.float32)


def _mid(conv, att, x2d, w_out_bf, g_ffn, w_query_bf, sub_keys_bf):
    t, d = x2d.shape
    c = conv.shape[1]
    tb = min(t, 512)
    assert t % tb == 0
    dq = w_query_bf.shape[1]
    nrp = sub_keys_bf.shape[0]
    tok = lambda i: (i, 0)
    const = lambda i: (0, 0)
    return pl.pallas_call(
        _mid_kernel,
        grid=(t // tb,),
        in_specs=[
            pl.BlockSpec((tb, c), tok),
            pl.BlockSpec((tb, c), tok),
            pl.BlockSpec((tb, d), tok),
            pl.BlockSpec((c, d), const),
            pl.BlockSpec((c, d), lambda i: (1, 0)),
            pl.BlockSpec((1, d), const),
            pl.BlockSpec((d, dq), const),
            pl.BlockSpec(sub_keys_bf.shape, lambda i: (0, 0, 0)),
        ],
        out_specs=[
            pl.BlockSpec((tb, d), tok),
            pl.BlockSpec((tb, d), tok),
            pl.BlockSpec((nrp, N_KEYS, tb), lambda i: (0, 0, i)),
        ],
        out_shape=[
            jax.ShapeDtypeStruct((t, d), jnp.float32),
            jax.ShapeDtypeStruct((t, d), jnp.bfloat16),
            jax.ShapeDtypeStruct((nrp, N_KEYS, t), jnp.float32),
        ],
        compiler_params=pltpu.CompilerParams(
            dimension_semantics=("arbitrary",), vmem_limit_bytes=VMEM_LIMIT),
        name="mid",
    )(conv, att, x2d, w_out_bf, w_out_bf, g_ffn.reshape(1, -1), w_query_bf, sub_keys_bf)


def _ce(a, b):
    if a is None:
        return b, None
    if b is None:
        return a, None
    return jnp.maximum(a, b), jnp.minimum(a, b)


def _sort_desc(xs):
    xs = list(xs)
    n = len(xs)
    p = 1
    while p < n:
        k = p
        while k >= 1:
            for j in range(k % p, n - k, 2 * k):
                for i in range(min(k, n - j - k)):
                    if (i + j) // (2 * p) == (i + j + k) // (2 * p):
                        xs[i + j], xs[i + j + k] = _ce(xs[i + j], xs[i + j + k])
            k //= 2
        p *= 2
    return xs


def _bitonic_top(a, b):
    n = len(a)
    return [_ce(a[i], b[n - 1 - i])[0] for i in range(n)]


def _bitonic_sort_desc(xs):
    xs = list(xs)
    n = len(xs)
    d = n // 2
    while d >= 1:
        for i in range(n):
            if i & d == 0:
                xs[i], xs[i + d] = _ce(xs[i], xs[i + d])
        d //= 2
    return xs


def _fill(xs):
    return [jnp.full((SUBLANES, LANES), NEG_BIG, jnp.float32) if x is None else x for x in xs]


def _sublane_merge_sorted(xs):
    for shift in (4, 2, 1):
        other = [pltpu.roll(x, shift, 0) for x in xs]
        xs = _bitonic_sort_desc(_bitonic_top(xs, other))
    return xs


def _sublane_merge_kth(xs):
    for shift in (4, 2):
        other = [pltpu.roll(x, shift, 0) for x in xs]
        xs = _bitonic_sort_desc(_bitonic_top(xs, other))
    other = [pltpu.roll(x, 1, 0) for x in xs]
    top = _bitonic_top(xs, other)
    return functools.reduce(jnp.minimum, top)


def _top16_rows(s):
    tiles = [s[i * SUBLANES:(i + 1) * SUBLANES, :] for i in range(N_KEYS // SUBLANES)]
    return _sublane_merge_sorted(_sort_desc(tiles))


def _dup_bf16(x):
    b = pltpu.bitcast(x.astype(jnp.bfloat16).astype(jnp.float32), jnp.uint32)
    return b | (b >> 16)


def _route_kernel(st_ref, rank_ref, n_ref, ea_ref, eb_ref, *, tb):
    sub = lax.broadcasted_iota(jnp.int32, (SUBLANES, LANES), 0)
    for g in range(tb // LANES):
        cols = slice(g * LANES, (g + 1) * LANES)
        for r in range(R_HEADS):
            s1 = st_ref[2 * r, :, cols]
            s2 = st_ref[2 * r + 1, :, cols]
            v1 = _top16_rows(s1)
            v2 = _top16_rows(s2)
            pack = lambda v, base: functools.reduce(
                lambda acc, j: jnp.where(sub == j, v[base + j], acc), range(1, SUBLANES), v[base])
            w1a, w1b, w2a, w2b = pack(v1, 0), pack(v1, 8), pack(v2, 0), pack(v2, 8)
            neg = jnp.full((SUBLANES, LANES), NEG_BIG, jnp.float32)
            cands = [
                v1[0] + w2a,
                v1[0] + w2b,
                jnp.where(sub >= 1, v2[0] + w1a, neg),
                v2[0] + w1b,
                jnp.where(sub >= 1, v1[1] + w2a, neg),
                jnp.where(sub >= 2, v2[1] + w1a, neg),
                jnp.where((sub >= 2) & (sub <= 4), v1[2] + w2a, neg),
                jnp.where((sub >= 2) & (sub <= 3), v1[3] + w2a, neg),
                jnp.where(sub == 2, v1[4] + w2a, neg),
            ]
            srt = _fill(_sort_desc(cands + [None] * (TOPK - len(cands))))
            thr = _sublane_merge_kth(srt)
            m1, m2 = v1[0], v2[0]
            top = m1 + m2
            z = functools.reduce(
                lambda acc, cnd: acc + jnp.where(cnd >= thr, jnp.exp(cnd - top), 0.0), cands,
                jnp.zeros((SUBLANES, LANES), jnp.float32))
            for shift in (4, 2, 1):
                z = z + pltpu.roll(z, shift, 0)
            rank2 = jnp.zeros((N_KEYS, LANES), jnp.float32)
            cnt = jnp.zeros((N_KEYS, LANES), jnp.float32)
            thr_row = thr[0:1, :]
            for j in range(TOPK):
                rank2 = jnp.where(v2[j][0:1, :] > s2, j + 1.0, rank2)
            for j in range(TOPK // 2):
                cnt = jnp.where(s1 + v2[j][0:1, :] >= thr_row, j + 1.0, cnt)
            cnt_top = jnp.zeros((SUBLANES, LANES), jnp.float32)
            for j in range(TOPK // 2, TOPK):
                cnt_top = jnp.where(v1[0] + v2[j] >= thr, j + 1.0, cnt_top)
            cnt = jnp.maximum(cnt, jnp.where(s1 == v1[0][0:1, :], cnt_top[0:1, :], 0.0))
            n_ref[r, :, cols] = _dup_bf16(cnt)
            ea_ref[r, :, cols] = _dup_bf16(jnp.exp(s1 - m1[0:1, :]) / z[0:1, :])
            eb_ref[r, :, cols] = jnp.exp(s2 - m2[0:1, :]).astype(jnp.bfloat16)
            rank_ref[r, :, cols] = rank2.astype(jnp.bfloat16)


def _route(st):
    nrp, nk, t = st.shape
    tb = min(t, 256)
    assert t % tb == 0 and tb % LANES == 0 and nk == N_KEYS and nrp == 2 * R_HEADS
    row_spec = pl.BlockSpec((R_HEADS, nk, tb), lambda i: (0, 0, i))
    return pl.pallas_call(
        functools.partial(_route_kernel, tb=tb),
        grid=(t // tb,),
        in_specs=[pl.BlockSpec((nrp, nk, tb), lambda i: (0, 0, i))],
        out_specs=[row_spec] * 4,
        out_shape=[
            jax.ShapeDtypeStruct((R_HEADS, nk, t), jnp.bfloat16),
            jax.ShapeDtypeStruct((R_HEADS, nk, t), jnp.uint32),
            jax.ShapeDtypeStruct((R_HEADS, nk, t), jnp.uint32),
            jax.ShapeDtypeStruct((R_HEADS, nk, t), jnp.bfloat16),
        ],
        compiler_params=pltpu.CompilerParams(
            dimension_semantics=("arbitrary",), vmem_limit_bytes=VMEM_LIMIT),
        name="route",
    )(st)


def _gelu(x):
    hx = 0.5 * x
    return hx + hx * lax.erf(x * (2.0 ** -0.5))


def _packed_row(ref, r, row, cols):
    tile = jnp.broadcast_to(ref[r, row:row + 1, cols], (SUBLANES, LANES))
    return pltpu.bitcast(tile, jnp.bfloat16)


def _peer_kernel(h_ref, u_ref, vt_ref, rank_ref, eb_ref, n_odd_ref, n_even_ref, ea_odd_ref, ea_even_ref,
                 x1_ref, g_ref, y_ref, acc_ref, act_a, act_b, coef_a, coef_b, *, eb_rows, tb):
    j, s = pl.program_id(0), pl.program_id(1)
    n_rows, last = pl.num_programs(0), pl.num_programs(1) - 1

    @pl.when((s == 0) & (j > 0))
    def _():
        acc_ref[...] = jnp.zeros_like(acc_ref)

    n_tg = tb // LANES
    tn = min(tb, MXU_COLS)

    def accumulate(half, coef_ref, piece):
        vt = vt_ref[:, half * eb_rows:(half + 1) * eb_rows]
        cols = slice(piece * tn, (piece + 1) * tn)
        acc_ref[:, cols] += jnp.dot(vt, coef_ref[:, cols], preferred_element_type=jnp.float32)

    def gate(cnt_ref, gain_ref, act_ref, coef_ref, tg):
        zero = jnp.zeros((PACKED_ROWS, LANES), jnp.bfloat16)
        cols = slice(tg * LANES, (tg + 1) * LANES)
        for ci in range(eb_rows // N_KEYS):
            cnt = [_packed_row(cnt_ref, r, ci, cols) for r in range(R_HEADS)]
            ea = [_packed_row(gain_ref, r, ci, cols) for r in range(R_HEADS)]
            for ch in range(N_KEYS // PACKED_ROWS):
                keys = slice(ch * PACKED_ROWS, (ch + 1) * PACKED_ROWS)
                g = zero
                for r in range(R_HEADS):
                    sel = jnp.minimum(jnp.maximum(cnt[r] - rank_ref[r, keys, cols], 0), 1)
                    g = g + (ea[r] * sel) * eb_ref[r, keys, cols]
                rows = slice(ci * N_KEYS + ch * PACKED_ROWS, ci * N_KEYS + (ch + 1) * PACKED_ROWS)
                coef_ref[rows, cols] = g * _gelu(act_ref[rows, cols].astype(jnp.bfloat16))

    def activate(half, act_ref, piece):
        u = u_ref[half * eb_rows:(half + 1) * eb_rows, :]
        cols = slice(piece * tn, (piece + 1) * tn)
        act_ref[:, cols] = lax.dot_general(u, h_ref[cols, :], _NT, preferred_element_type=jnp.float32)

    n_piece = tb // tn

    per = n_tg // n_piece

    halves = ((n_even_ref, ea_even_ref, act_a, coef_a), (n_odd_ref, ea_odd_ref, act_b, coef_b))

    def step(drain, fill, finish):
        for half, (cnt_ref, gain_ref, act_ref, coef_ref) in enumerate(halves):
            for piece in range(n_piece):
                if drain:
                    for tg in range(piece * per, (piece + 1) * per):
                        gate(cnt_ref, gain_ref, act_ref, coef_ref, tg)
                    accumulate(half, coef_ref, piece)
                if fill:
                    activate(half, act_ref, piece)
        if finish:
            y_ref[...] = _rms(x1_ref[...] + acc_ref[...].T, g_ref[...])

    pl.when((j > 0) & (s < last))(functools.partial(step, True, True, False))
    pl.when((j > 0) & (j < n_rows - 1) & (s == last))(functools.partial(step, True, True, True))
    pl.when((j == n_rows - 1) & (s == last))(functools.partial(step, True, False, True))
    pl.when((j == 0) & (s == last))(functools.partial(step, False, True, False))


def _peer(h2, u_bf, vt_bf, rank2, cnt, ea, eb, x1, g_final):
    t, d = h2.shape
    n_exp = u_bf.shape[0]
    tb = min(t, 512)
    eb_rows = SUBLANES * N_KEYS
    assert t % tb == 0 and n_exp % (2 * eb_rows) == 0 and n_exp == N_KEYS * N_KEYS
    n_tb = t // tb
    n_steps = n_exp // (2 * eb_rows)
    drained = lambda j: jnp.clip(j - 1, 0, n_tb - 1)
    filled = lambda j, s: jnp.where(s == n_steps - 1, jnp.minimum(j, n_tb - 1), drained(j))
    tok = lambda j, s: (drained(j), 0)
    tile_spec = pl.BlockSpec((R_HEADS, N_KEYS, tb), lambda j, s: (0, 0, drained(j)))
    even_spec = pl.BlockSpec((R_HEADS, SUBLANES, tb), lambda j, s: (0, 2 * s, drained(j)))
    odd_spec = pl.BlockSpec((R_HEADS, SUBLANES, tb), lambda j, s: (0, 2 * s + 1, drained(j)))
    return pl.pallas_call(
        functools.partial(_peer_kernel, eb_rows=eb_rows, tb=tb),
        grid=(n_tb + 1, n_steps),
        in_specs=[
            pl.BlockSpec((tb, d), lambda j, s: (filled(j, s), 0)),
            pl.BlockSpec((2 * eb_rows, d), lambda j, s: ((s + 1) % n_steps, 0)),
            pl.BlockSpec((d, 2 * eb_rows), lambda j, s: (0, s)),
            tile_spec, tile_spec, odd_spec, even_spec, odd_spec, even_spec,
            pl.BlockSpec((tb, d), tok),
            pl.BlockSpec((1, d), lambda j, s: (0, 0)),
        ],
        out_specs=pl.BlockSpec((tb, d), tok),
        out_shape=jax.ShapeDtypeStruct((t, d), jnp.float32),
        scratch_shapes=[
            pltpu.VMEM((d, tb), jnp.float32),
            pltpu.VMEM((eb_rows, tb), jnp.float32),
            pltpu.VMEM((eb_rows, tb), jnp.float32),
            pltpu.VMEM((eb_rows, tb), jnp.bfloat16),
            pltpu.VMEM((eb_rows, tb), jnp.bfloat16),
        ],
        compiler_params=pltpu.CompilerParams(
            dimension_semantics=("arbitrary", "arbitrary"), vmem_limit_bytes=VMEM_LIMIT),
        name="peer",
    )(h2, u_bf, vt_bf, rank2, eb, cnt, cnt, ea, ea, x1, g_final.reshape(1, -1))


def _stream(x, left, attend, p):
    b, s, d = x.shape
    q, k, v, kb, vb, conv, tail = _in_proj(x, p["g_mix"], p["w_in"], left, p["conv_w"], p["conv_b"],
                                           p["ln_g"], p["ln_b"])
    att = attend(q, kb, vb)
    c = conv.shape[-1]
    x1, h2, st = _mid(conv.reshape(b * s, c), att.reshape(b * s, c), x.reshape(b * s, d),
                      p["w_out"], p["g_ffn"], p["w_query"], p["sub_keys"])
    rank2, cnt, ea, eb = _route(st)
    y = _peer(h2, p["peer_u"], p["peer_vt"], rank2, cnt, ea, eb, x1, p["g_final"])
    k = k.reshape(b, s, N_HEADS, 2, HEAD_DIM)
    v = v.reshape(b, s, N_HEADS, V_DIM)
    return y.reshape(b, s, d), k, v, tail[:, CONV_PAD - (CONV_WIDTH - 1):]


def kernel(x_prompt, x_sample, cache_k, cache_v, state_conv, g_mix, w_in, conv_w, conv_b, conv_ln_g, conv_ln_b, lambda_q1, lambda_k1, lambda_q2, lambda_k2, subln_g, rel_bias, w_out, g_ffn, w_query, sub_keys, peer_u, peer_v, g_final):
    depth = w_in.shape[0]
    assert depth == 1, "single-layer step"
    l = 0
    bf16 = jnp.bfloat16
    b, s, d = x_prompt.shape
    bd, sd, _ = x_sample.shape
    past = cache_k.shape[2]
    c = conv_w.shape[-1]

    lam_init = _lambda_init(l)
    lam = (jnp.exp(jnp.sum(lambda_q1[l].astype(jnp.float32) * lambda_k1[l].astype(jnp.float32)))
           - jnp.exp(jnp.sum(lambda_q2[l].astype(jnp.float32) * lambda_k2[l].astype(jnp.float32)))
           + lam_init).reshape(1)
    out_scale = 1.0 - lam_init

    p = {
        "g_mix": g_mix[l], "w_in": w_in[l].astype(bf16), "conv_w": conv_w[l], "conv_b": conv_b[l],
        "ln_g": conv_ln_g[l], "ln_b": conv_ln_b[l], "w_out": w_out[l].astype(bf16), "g_ffn": g_ffn[l],
        "w_query": w_query[l].astype(bf16),
        "sub_keys": sub_keys[l].reshape(2 * R_HEADS, N_KEYS, -1).astype(bf16),
        "peer_u": peer_u[l].astype(bf16), "peer_vt": peer_v[l].astype(bf16).T, "g_final": g_final,
    }

    attend_p = lambda q, kb, vb: _attn_prompt(q, kb, vb, rel_bias, lam, subln_g[l], out_scale)
    y_p, k_p, v_p, tail_p = _stream(x_prompt, jnp.zeros((b, CONV_PAD, c), jnp.float32), attend_p, p)

    n_new = -(-sd // LANES) * LANES
    pos_s = past + jnp.arange(sd, dtype=jnp.int32)
    bias_c = _bias_table(rel_bias, pos_s, jnp.arange(past, dtype=jnp.int32), True)
    bias_n = _bias_table(rel_bias, pos_s, past + jnp.arange(n_new, dtype=jnp.int32), True)
    bias_n = jnp.where(jnp.arange(n_new) < sd, bias_n, NEG)
    ck = cache_k[l].reshape(bd, past, c)
    cv = cache_v[l].reshape(bd, past, c)

    def attend_s(q, kb, vb):
        pad = ((0, 0), (0, n_new - sd), (0, 0))
        return _attn_sample(q, ck, cv, jnp.pad(kb, pad), jnp.pad(vb, pad), bias_c, bias_n,
                            lam, subln_g[l], out_scale)

    left_s = jnp.pad(state_conv[l], ((0, 0), (CONV_PAD - (CONV_WIDTH - 1), 0), (0, 0)))
    y_s, k_s, v_s, tail_s = _stream(x_sample, left_s, attend_s, p)

    return (y_p, y_s, k_p[None], v_p[None], tail_p[None], k_s[None], v_s[None], tail_s[None])
```

```python
import functools
import math

import jax
import jax.numpy as jnp
from jax import lax
from jax.experimental import pallas as pl
from jax.experimental.pallas import tpu as pltpu

CHUNK = 64
CONV_WIDTH = 31
CONV_PAD = 32
N_HEADS = 4
HEAD_DIM = 64
V_DIM = 2 * HEAD_DIM
N_BUCKETS = 32
MAX_DISTANCE = 128
N_KEYS = 128
R_HEADS = 8
TOPK = 16
EPS = 1e-6
NEG = -1e30
NEG_BIG = -3.0e38
LOG2_E = 1.4426950408889634
LANES = 128
SUBLANES = 8
PACKED_ROWS = 2 * SUBLANES
MXU_COLS = 256
VMEM_LIMIT = 48 * 1024 * 1024

_NT = (((1,), (1,)), ((), ()))


def _lambda_init(layer):
    return 0.8 - 0.6 * math.exp(-0.3 * layer)


def _rms(xf, g):
    return xf * lax.rsqrt(jnp.mean(xf * xf, axis=-1, keepdims=True) + EPS) * g


def _sigmoid(x):
    return 1.0 / (1.0 + jnp.exp(-x))


def _in_proj_kernel(x_ref, g_ref, w_ref, left_ref, cw_ref, cb_ref, lg_ref, lb_ref,
                    q_ref, k_ref, v_ref, kb_ref, vb_ref, conv_ref, tail_ref, abuf, *, ts, c):
    s = pl.program_id(1)
    h = _rms(x_ref[0], g_ref[...]).astype(jnp.bfloat16)

    @pl.when(s == 0)
    def _():
        abuf[0, 0:CONV_PAD, :] = left_ref[0]

    glu_in = jnp.dot(h, w_ref[:, 0:2 * c], preferred_element_type=jnp.float32)
    abuf[0, CONV_PAD:CONV_PAD + ts, :] = glu_in[:, :c] * _sigmoid(glu_in[:, c:])
    q = jnp.dot(h, w_ref[:, 2 * c:3 * c], preferred_element_type=jnp.float32)
    q_ref[0] = (q * (HEAD_DIM ** -0.5)).astype(jnp.bfloat16)
    k = jnp.dot(h, w_ref[:, 3 * c:4 * c], preferred_element_type=jnp.float32)
    k_ref[0] = k
    kb_ref[0] = k.astype(jnp.bfloat16)
    v = jnp.dot(h, w_ref[:, 4 * c:5 * c], preferred_element_type=jnp.float32)
    v_ref[0] = v
    vb_ref[0] = v.astype(jnp.bfloat16)

    n_sh = ts + CONV_PAD - SUBLANES
    for sh in range(1, SUBLANES):
        abuf[sh, 0:n_sh, :] = abuf[0, sh:sh + n_sh, :]
    rc = min(ts, 64)
    for r0 in range(0, ts, rc):
        acc = jnp.zeros((rc, c), jnp.float32)
        for w in range(CONV_WIDTH):
            off = r0 + CONV_PAD - (CONV_WIDTH - 1) + w
            sh = off % SUBLANES
            acc = acc + abuf[sh, off - sh:off - sh + rc, :] * cw_ref[w:w + 1, :]
        y = acc + cb_ref[...]
        mu = jnp.mean(y, axis=-1, keepdims=True)
        d = y - mu
        var = jnp.mean(d * d, axis=-1, keepdims=True)
        yn = d * lax.rsqrt(var + EPS) * lg_ref[...] + lb_ref[...]
        conv_ref[0, r0:r0 + rc, :] = (yn * _sigmoid(yn)).astype(jnp.bfloat16)

    tail = abuf[0, ts:ts + CONV_PAD, :]
    tail_ref[0] = tail
    abuf[0, 0:CONV_PAD, :] = tail


def _in_proj(x, g_mix, w_in_bf, left, conv_w, conv_b, ln_g, ln_b):
    b, s, d = x.shape
    c = conv_w.shape[1]
    ts = min(s, 512)
    assert s % ts == 0 and ts >= CONV_PAD and ts % SUBLANES == 0
    cw = jnp.pad(conv_w, ((0, CONV_PAD - CONV_WIDTH), (0, 0)))
    row = lambda a: a.reshape(1, -1)
    tok = lambda bi, si: (bi, si, 0)
    const2 = lambda bi, si: (0, 0)
    f32, bf16 = jnp.float32, jnp.bfloat16
    outs = pl.pallas_call(
        functools.partial(_in_proj_kernel, ts=ts, c=c),
        grid=(b, s // ts),
        in_specs=[
            pl.BlockSpec((1, ts, d), tok),
            pl.BlockSpec((1, d), const2),
            pl.BlockSpec(w_in_bf.shape, const2),
            pl.BlockSpec((1, CONV_PAD, c), lambda bi, si: (bi, 0, 0)),
            pl.BlockSpec((CONV_PAD, c), const2),
            pl.BlockSpec((1, c), const2),
            pl.BlockSpec((1, c), const2),
            pl.BlockSpec((1, c), const2),
        ],
        out_specs=[pl.BlockSpec((1, ts, c), tok)] * 6
        + [pl.BlockSpec((1, CONV_PAD, c), lambda bi, si: (bi, 0, 0))],
        out_shape=[
            jax.ShapeDtypeStruct((b, s, c), bf16),
            jax.ShapeDtypeStruct((b, s, c), f32),
            jax.ShapeDtypeStruct((b, s, c), f32),
            jax.ShapeDtypeStruct((b, s, c), bf16),
            jax.ShapeDtypeStruct((b, s, c), bf16),
            jax.ShapeDtypeStruct((b, s, c), bf16),
            jax.ShapeDtypeStruct((b, CONV_PAD, c), f32),
        ],
        scratch_shapes=[pltpu.VMEM((SUBLANES, ts + CONV_PAD, c), f32)],
        compiler_params=pltpu.CompilerParams(
            dimension_semantics=("arbitrary", "arbitrary"), vmem_limit_bytes=VMEM_LIMIT),
        name="in_proj",
    )(x, row(g_mix), w_in_bf, left, cw, row(conv_b), row(ln_g), row(ln_b))
    return outs


def _rel_bucket(rel):
    nb = N_BUCKETS // 2
    max_exact = nb // 2
    ret = jnp.where(rel > 0, nb, 0)
    n = jnp.abs(rel)
    nf = jnp.maximum(n, 1).astype(jnp.float32)
    large = max_exact + (jnp.log(nf / max_exact) / math.log(MAX_DISTANCE / max_exact)
                         * (nb - max_exact)).astype(jnp.int32)
    large = jnp.minimum(large, nb - 1)
    return ret + jnp.where(n < max_exact, n, large)


def _bias_table(rel_bias, q_pos, k_pos, masked):
    nq, nk = q_pos.shape[0], k_pos.shape[0]
    period = nq + nk
    m = jnp.arange(period, dtype=jnp.int32)
    bucket = _rel_bucket(k_pos[0] - q_pos[0] + jnp.where(m < nk, m, m - period))
    table = rel_bias.astype(jnp.float32).T[:, None, :]
    hit = bucket[None, :, None] == jnp.arange(N_BUCKETS, dtype=jnp.int32)
    line = jnp.sum(jnp.where(hit, table, 0.0), axis=-1)
    bias = jnp.tile(line, (1, nq))[:, :nq * (period - 1)].reshape(-1, nq, period - 1)[:, :, :nk]
    if masked:
        mask = (k_pos[None, :] // CHUNK) <= (q_pos[:, None] // CHUNK)
        bias = jnp.where(mask[None], bias, NEG)
    return bias


def _split_maps(q):
    lane = lax.broadcasted_iota(jnp.int32, q.shape, 1)
    zero = jnp.zeros_like(q)
    return jnp.where(lane < HEAD_DIM, q, zero), jnp.where(lane >= HEAD_DIM, q, zero)


def _softmax_rows(s):
    p = jnp.exp(s - jnp.max(s, axis=-1, keepdims=True))
    return p / jnp.sum(p, axis=-1, keepdims=True)


def _attn_finish(s1, s2, vv, lam, g, out_scale):
    attn = (_softmax_rows(s1) - lam * _softmax_rows(s2)).astype(jnp.bfloat16)
    o = jnp.dot(attn, vv, preferred_element_type=jnp.float32)
    return (_rms(o, g) * out_scale).astype(jnp.bfloat16)


def _attn_prompt_kernel(lam_ref, q_ref, k_ref, v_ref, slab_ref, g_ref, o_ref,
                        s_ref, m_ref, l_ref, acc_ref, *, tq, n_hd, out_scale):
    qi = pl.program_id(2)
    head = lambda hh: slice(hh * V_DIM, (hh + 1) * V_DIM)
    qs = [qm for hh in range(n_hd) for qm in _split_maps(q_ref[0, :, head(hh)])]
    width = m_ref.shape[-1]
    fold = lambda a, op: functools.reduce(op, [a[:, c:c + width] for c in range(0, tq, width)])

    m_ref[...] = jnp.full(m_ref.shape, NEG_BIG, jnp.float32)
    l_ref[...] = jnp.zeros_like(l_ref)
    acc_ref[...] = jnp.zeros_like(acc_ref)

    def logits(j, carry):
        rows = pl.ds(pl.multiple_of(j * tq, tq), tq)
        kind = jnp.clip(j - qi, -2, 0) + 2
        for hh in range(n_hd):
            kj = k_ref[0, rows, head(hh)]
            bias = slab_ref[hh, kind]
            for mp in range(2):
                a = (lax.dot_general(qs[2 * hh + mp], kj, _NT, preferred_element_type=jnp.float32) + bias) * LOG2_E
                s_ref[2 * hh + mp, j] = a
                m_ref[2 * hh + mp] = jnp.maximum(m_ref[2 * hh + mp], fold(a, jnp.maximum))
        return carry

    lax.fori_loop(0, qi + 1, logits, 0)
    lanes = min(tq, LANES)
    row_max = [jnp.broadcast_to(jnp.max(m_ref[i], axis=-1, keepdims=True), (tq, lanes)) for i in range(2 * n_hd)]

    def accumulate(j, carry):
        rows = pl.ds(pl.multiple_of(j * tq, tq), tq)
        for hh in range(n_hd):
            vj = v_ref[0, rows, head(hh)]
            for i in (2 * hh, 2 * hh + 1):
                p = jnp.concatenate([jnp.exp2(s_ref[i, j, :, c * lanes:(c + 1) * lanes] - row_max[i])
                                     for c in range(tq // lanes)], axis=1)
                l_ref[i] += fold(p, jnp.add)
                acc_ref[i] += jnp.dot(p.astype(jnp.bfloat16), vj, preferred_element_type=jnp.float32)
        return carry

    lax.fori_loop(0, qi + 1, accumulate, 0)
    for hh in range(n_hd):
        inv = [1.0 / jnp.sum(l_ref[2 * hh + mp], axis=-1, keepdims=True) for mp in range(2)]
        o = acc_ref[2 * hh] * inv[0] - lam_ref[0] * (acc_ref[2 * hh + 1] * inv[1])
        o_ref[0, :, head(hh)] = (_rms(o, g_ref[...]) * out_scale).astype(jnp.bfloat16)


def _attn_prompt(q, kb, vb, rel_bias, lam, subln_g, out_scale):
    b, s, c = q.shape
    tq = min(s, 512)
    assert s % tq == 0 and tq % CHUNK == 0 and tq >= MAX_DISTANCE and c == N_HEADS * V_DIM
    n_kb = s // tq
    pos = jnp.arange(tq, dtype=jnp.int32)
    slabs = jnp.stack([
        _bias_table(rel_bias, pos + 2 * tq, pos, False),
        _bias_table(rel_bias, pos + tq, pos, False),
        _bias_table(rel_bias, pos, pos, True),
    ], axis=1)
    width = min(tq, LANES)
    n_hd = 2
    assert N_HEADS % n_hd == 0
    return pl.pallas_call(
        functools.partial(_attn_prompt_kernel, tq=tq, n_hd=n_hd, out_scale=out_scale),
        grid=(b, N_HEADS // n_hd, n_kb),
        in_specs=[
            pl.BlockSpec(memory_space=pltpu.SMEM),
            pl.BlockSpec((1, tq, n_hd * V_DIM), lambda bi, hi, qi: (bi, qi, hi)),
            pl.BlockSpec((1, s, n_hd * V_DIM), lambda bi, hi, qi: (bi, 0, hi)),
            pl.BlockSpec((1, s, n_hd * V_DIM), lambda bi, hi, qi: (bi, 0, hi)),
            pl.BlockSpec((n_hd, 3, tq, tq), lambda bi, hi, qi: (hi, 0, 0, 0)),
            pl.BlockSpec((1, V_DIM), lambda bi, hi, qi: (0, 0)),
        ],
        out_specs=pl.BlockSpec((1, tq, n_hd * V_DIM), lambda bi, hi, qi: (bi, qi, hi)),
        out_shape=jax.ShapeDtypeStruct((b, s, c), jnp.bfloat16),
        scratch_shapes=[
            pltpu.VMEM((2 * n_hd, n_kb, tq, tq), jnp.float32),
            pltpu.VMEM((2 * n_hd, tq, width), jnp.float32),
            pltpu.VMEM((2 * n_hd, tq, width), jnp.float32),
            pltpu.VMEM((2 * n_hd, tq, V_DIM), jnp.float32),
        ],
        compiler_params=pltpu.CompilerParams(
            dimension_semantics=("arbitrary",) * 3, vmem_limit_bytes=VMEM_LIMIT),
        name="attn_prompt",
    )(lam, q, kb, vb, slabs, subln_g.reshape(1, -1))


def _attn_sample_kernel(lam_ref, q_ref, ck_ref, cv_ref, kn_ref, vn_ref, bc_ref, bn_ref, g_ref, o_ref,
                        *, out_scale):
    q1, q2 = _split_maps(q_ref[0])
    ck = ck_ref[0].astype(jnp.bfloat16)
    kn = kn_ref[0]
    logits = lambda qm: jnp.concatenate([
        lax.dot_general(qm, ck, _NT, preferred_element_type=jnp.float32) + bc_ref[0],
        lax.dot_general(qm, kn, _NT, preferred_element_type=jnp.float32) + bn_ref[0]], axis=1)
    vv = jnp.concatenate([cv_ref[0].astype(jnp.bfloat16), vn_ref[0]], axis=0)
    o_ref[0] = _attn_finish(logits(q1), logits(q2), vv, lam_ref[0], g_ref[...], out_scale)


def _attn_sample(q, cache_k, cache_v, k_new, v_new, bias_c, bias_n, lam, subln_g, out_scale):
    b, sq, c = q.shape
    past, n_new = cache_k.shape[1], k_new.shape[1]
    per_head = lambda rows: pl.BlockSpec((1, rows, V_DIM), lambda bi, hi: (bi, 0, hi))
    return pl.pallas_call(
        functools.partial(_attn_sample_kernel, out_scale=out_scale),
        grid=(b, N_HEADS),
        in_specs=[
            pl.BlockSpec(memory_space=pltpu.SMEM),
            per_head(sq), per_head(past), per_head(past), per_head(n_new), per_head(n_new),
            pl.BlockSpec((1, sq, past), lambda bi, hi: (hi, 0, 0)),
            pl.BlockSpec((1, sq, n_new), lambda bi, hi: (hi, 0, 0)),
            pl.BlockSpec((1, V_DIM), lambda bi, hi: (0, 0)),
        ],
        out_specs=per_head(sq),
        out_shape=jax.ShapeDtypeStruct((b, sq, c), jnp.bfloat16),
        compiler_params=pltpu.CompilerParams(
            dimension_semantics=("arbitrary",) * 2, vmem_limit_bytes=VMEM_LIMIT),
        name="attn_sample",
    )(lam, q, cache_k, cache_v, k_new, v_new, bias_c, bias_n, subln_g.reshape(1, -1))


def _mid_kernel(conv_ref, att_ref, x_ref, wc_ref, wa_ref, g_ref, wq_ref, sk_ref,
                x1_ref, h2_ref, st_ref):
    x1 = (x_ref[...]
          + jnp.dot(conv_ref[...], wc_ref[...], preferred_element_type=jnp.float32)
          + jnp.dot(att_ref[...], wa_ref[...], preferred_element_type=jnp.float32))
    x1_ref[...] = x1
    h2 = _rms(x1, g_ref[...]).astype(jnp.bfloat16)
    h2_ref[...] = h2
    qq = jnp.dot(h2, wq_ref[...], preferred_element_type=jnp.float32).astype(jnp.bfloat16)
    for rp in range(2 * R_HEADS):
        st_ref[rp] = lax.dot_general(sk_ref[rp], qq[:, rp * N_KEYS:(rp + 1) * N_KEYS], _NT,
                                     preferred_element_type=jnp.float32)


def _mid(conv, att, x2d, w_out_bf, g_ffn, w_query_bf, sub_keys_bf):
    t, d = x2d.shape
    c = conv.shape[1]
    tb = min(t, 512)
    assert t % tb == 0
    dq = w_query_bf.shape[1]
    nrp = sub_keys_bf.shape[0]
    tok = lambda i: (i, 0)
    const = lambda i: (0, 0)
    return pl.pallas_call(
        _mid_kernel,
        grid=(t // tb,),
        in_specs=[
            pl.BlockSpec((tb, c), tok),
            pl.BlockSpec((tb, c), tok),
            pl.BlockSpec((tb, d), tok),
            pl.BlockSpec((c, d), const),
            pl.BlockSpec((c, d), lambda i: (1, 0)),
            pl.BlockSpec((1, d), const),
            pl.BlockSpec((d, dq), const),
            pl.BlockSpec(sub_keys_bf.shape, lambda i: (0, 0, 0)),
        ],
        out_specs=[
            pl.BlockSpec((tb, d), tok),
            pl.BlockSpec((tb, d), tok),
            pl.BlockSpec((nrp, N_KEYS, tb), lambda i: (0, 0, i)),
        ],
        out_shape=[
            jax.ShapeDtypeStruct((t, d), jnp.float32),
            jax.ShapeDtypeStruct((t, d), jnp.bfloat16),
            jax.ShapeDtypeStruct((nrp, N_KEYS, t), jnp.float32),
        ],
        compiler_params=pltpu.CompilerParams(
            dimension_semantics=("arbitrary",), vmem_limit_bytes=VMEM_LIMIT),
        name="mid",
    )(conv, att, x2d, w_out_bf, w_out_bf, g_ffn.reshape(1, -1), w_query_bf, sub_keys_bf)


def _ce(a, b):
    if a is None:
        return b, None
    if b is None:
        return a, None
    return jnp.maximum(a, b), jnp.minimum(a, b)


def _sort_desc(xs):
    xs = list(xs)
    n = len(xs)
    p = 1
    while p < n:
        k = p
        while k >= 1:
            for j in range(k % p, n - k, 2 * k):
                for i in range(min(k, n - j - k)):
                    if (i + j) // (2 * p) == (i + j + k) // (2 * p):
                        xs[i + j], xs[i + j + k] = _ce(xs[i + j], xs[i + j + k])
            k //= 2
        p *= 2
    return xs


def _bitonic_top(a, b):
    n = len(a)
    return [_ce(a[i], b[n - 1 - i])[0] for i in range(n)]


def _bitonic_sort_desc(xs):
    xs = list(xs)
    n = len(xs)
    d = n // 2
    while d >= 1:
        for i in range(n):
            if i & d == 0:
                xs[i], xs[i + d] = _ce(xs[i], xs[i + d])
        d //= 2
    return xs


def _fill(xs):
    return [jnp.full((SUBLANES, LANES), NEG_BIG, jnp.float32) if x is None else x for x in xs]


def _sublane_merge_sorted(xs):
    for shift in (4, 2, 1):
        other = [pltpu.roll(x, shift, 0) for x in xs]
        xs = _bitonic_sort_desc(_bitonic_top(xs, other))
    return xs


def _sublane_merge_kth(xs):
    for shift in (4, 2):
        other = [pltpu.roll(x, shift, 0) for x in xs]
        xs = _bitonic_sort_desc(_bitonic_top(xs, other))
    other = [pltpu.roll(x, 1, 0) for x in xs]
    top = _bitonic_top(xs, other)
    return functools.reduce(jnp.minimum, top)


def _top16_rows(s):
    tiles = [s[i * SUBLANES:(i + 1) * SUBLANES, :] for i in range(N_KEYS // SUBLANES)]
    return _sublane_merge_sorted(_sort_desc(tiles))


def _dup_bf16(x):
    b = pltpu.bitcast(x.astype(jnp.bfloat16).astype(jnp.float32), jnp.uint32)
    return b | (b >> 16)


def _route_kernel(st_ref, rank_ref, n_ref, ea_ref, eb_ref, *, tb):
    sub = lax.broadcasted_iota(jnp.int32, (SUBLANES, LANES), 0)
    for g in range(tb // LANES):
        cols = slice(g * LANES, (g + 1) * LANES)
        for r in range(R_HEADS):
            s1 = st_ref[2 * r, :, cols]
            s2 = st_ref[2 * r + 1, :, cols]
            v1 = _top16_rows(s1)
            v2 = _top16_rows(s2)
            pack = lambda v, base: functools.reduce(
                lambda acc, j: jnp.where(sub == j, v[base + j], acc), range(1, SUBLANES), v[base])
            w1a, w1b, w2a, w2b = pack(v1, 0), pack(v1, 8), pack(v2, 0), pack(v2, 8)
            neg = jnp.full((SUBLANES, LANES), NEG_BIG, jnp.float32)
            cands = [
                v1[0] + w2a,
                v1[0] + w2b,
                jnp.where(sub >= 1, v2[0] + w1a, neg),
                v2[0] + w1b,
                jnp.where(sub >= 1, v1[1] + w2a, neg),
                jnp.where(sub >= 2, v2[1] + w1a, neg),
                jnp.where((sub >= 2) & (sub <= 4), v1[2] + w2a, neg),
                jnp.where((sub >= 2) & (sub <= 3), v1[3] + w2a, neg),
                jnp.where(sub == 2, v1[4] + w2a, neg),
            ]
            srt = _fill(_sort_desc(cands + [None] * (TOPK - len(cands))))
            thr = _sublane_merge_kth(srt)
            m1, m2 = v1[0], v2[0]
            top = m1 + m2
            z = functools.reduce(
                lambda acc, cnd: acc + jnp.where(cnd >= thr, jnp.exp(cnd - top), 0.0), cands,
                jnp.zeros((SUBLANES, LANES), jnp.float32))
            for shift in (4, 2, 1):
                z = z + pltpu.roll(z, shift, 0)
            rank2 = jnp.zeros((N_KEYS, LANES), jnp.float32)
            cnt = jnp.zeros((N_KEYS, LANES), jnp.float32)
            thr_row = thr[0:1, :]
            for j in range(TOPK):
                rank2 = jnp.where(v2[j][0:1, :] > s2, j + 1.0, rank2)
            for j in range(TOPK // 2):
                cnt = jnp.where(s1 + v2[j][0:1, :] >= thr_row, j + 1.0, cnt)
            cnt_top = jnp.zeros((SUBLANES, LANES), jnp.float32)
            for j in range(TOPK // 2, TOPK):
                cnt_top = jnp.where(v1[0] + v2[j] >= thr, j + 1.0, cnt_top)
            cnt = jnp.maximum(cnt, jnp.where(s1 == v1[0][0:1, :], cnt_top[0:1, :], 0.0))
            n_ref[r, :, cols] = _dup_bf16(cnt)
            ea_ref[r, :, cols] = _dup_bf16(jnp.exp(s1 - m1[0:1, :]) / z[0:1, :])
            eb_ref[r, :, cols] = jnp.exp(s2 - m2[0:1, :]).astype(jnp.bfloat16)
            rank_ref[r, :, cols] = rank2.astype(jnp.bfloat16)


def _route(st):
    nrp, nk, t = st.shape
    tb = min(t, 512)
    assert t % tb == 0 and tb % LANES == 0 and nk == N_KEYS and nrp == 2 * R_HEADS
    row_spec = pl.BlockSpec((R_HEADS, nk, tb), lambda i: (0, 0, i))
    return pl.pallas_call(
        functools.partial(_route_kernel, tb=tb),
        grid=(t // tb,),
        in_specs=[pl.BlockSpec((nrp, nk, tb), lambda i: (0, 0, i))],
        out_specs=[row_spec] * 4,
        out_shape=[
            jax.ShapeDtypeStruct((R_HEADS, nk, t), jnp.bfloat16),
            jax.ShapeDtypeStruct((R_HEADS, nk, t), jnp.uint32),
            jax.ShapeDtypeStruct((R_HEADS, nk, t), jnp.uint32),
            jax.ShapeDtypeStruct((R_HEADS, nk, t), jnp.bfloat16),
        ],
        compiler_params=pltpu.CompilerParams(
            dimension_semantics=("arbitrary",), vmem_limit_bytes=VMEM_LIMIT),
        name="route",
    )(st)


def _gelu(x):
    hx = 0.5 * x
    return hx + hx * lax.erf(x * (2.0 ** -0.5))


def _packed_row(ref, r, row, cols):
    tile = jnp.broadcast_to(ref[r, row:row + 1, cols], (SUBLANES, LANES))
    return pltpu.bitcast(tile, jnp.bfloat16)


def _peer_kernel(h_ref, u_ref, vt_ref, rank_ref, eb_ref, n_odd_ref, n_even_ref, ea_odd_ref, ea_even_ref,
                 x1_ref, g_ref, y_ref, acc_ref, act_a, act_b, coef_a, coef_b, *, eb_rows, tb):
    j, s = pl.program_id(0), pl.program_id(1)
    n_rows, last = pl.num_programs(0), pl.num_programs(1) - 1

    @pl.when((s == 0) & (j > 0))
    def _():
        acc_ref[...] = jnp.zeros_like(acc_ref)

    n_tg = tb // LANES
    tn = min(tb, MXU_COLS)

    def accumulate(half, coef_ref, piece):
        vt = vt_ref[:, half * eb_rows:(half + 1) * eb_rows]
        cols = slice(piece * tn, (piece + 1) * tn)
        acc_ref[:, cols] += jnp.dot(vt, coef_ref[:, cols], preferred_element_type=jnp.float32)

    def gate(cnt_ref, gain_ref, act_ref, coef_ref, tg):
        zero = jnp.zeros((PACKED_ROWS, LANES), jnp.bfloat16)
        cols = slice(tg * LANES, (tg + 1) * LANES)
        for ci in range(eb_rows // N_KEYS):
            cnt = [_packed_row(cnt_ref, r, ci, cols) for r in range(R_HEADS)]
            ea = [_packed_row(gain_ref, r, ci, cols) for r in range(R_HEADS)]
            for ch in range(N_KEYS // PACKED_ROWS):
                keys = slice(ch * PACKED_ROWS, (ch + 1) * PACKED_ROWS)
                g = zero
                for r in range(R_HEADS):
                    sel = jnp.minimum(jnp.maximum(cnt[r] - rank_ref[r, keys, cols], 0), 1)
                    g = g + (ea[r] * sel) * eb_ref[r, keys, cols]
                rows = slice(ci * N_KEYS + ch * PACKED_ROWS, ci * N_KEYS + (ch + 1) * PACKED_ROWS)
                coef_ref[rows, cols] = g * _gelu(act_ref[rows, cols].astype(jnp.bfloat16))

    def activate(half, act_ref, piece):
        u = u_ref[half * eb_rows:(half + 1) * eb_rows, :]
        cols = slice(piece * tn, (piece + 1) * tn)
        act_ref[:, cols] = lax.dot_general(u, h_ref[cols, :], _NT, preferred_element_type=jnp.float32)

    n_piece = tb // tn

    per = n_tg // n_piece

    halves = ((n_even_ref, ea_even_ref, act_a, coef_a), (n_odd_ref, ea_odd_ref, act_b, coef_b))

    def step(drain, fill, finish):
        for half, (cnt_ref, gain_ref, act_ref, coef_ref) in enumerate(halves):
            for piece in range(n_piece):
                if drain:
                    for tg in range(piece * per, (piece + 1) * per):
                        gate(cnt_ref, gain_ref, act_ref, coef_ref, tg)
                    accumulate(half, coef_ref, piece)
                if fill:
                    activate(half, act_ref, piece)
        if finish:
            y_ref[...] = _rms(x1_ref[...] + acc_ref[...].T, g_ref[...])

    pl.when((j > 0) & (s < last))(functools.partial(step, True, True, False))
    pl.when((j > 0) & (j < n_rows - 1) & (s == last))(functools.partial(step, True, True, True))
    pl.when((j == n_rows - 1) & (s == last))(functools.partial(step, True, False, True))
    pl.when((j == 0) & (s == last))(functools.partial(step, False, True, False))


def _peer(h2, u_bf, vt_bf, rank2, cnt, ea, eb, x1, g_final):
    t, d = h2.shape
    n_exp = u_bf.shape[0]
    tb = min(t, 512)
    eb_rows = SUBLANES * N_KEYS
    assert t % tb == 0 and n_exp % (2 * eb_rows) == 0 and n_exp == N_KEYS * N_KEYS
    n_tb = t // tb
    n_steps = n_exp // (2 * eb_rows)
    drained = lambda j: jnp.clip(j - 1, 0, n_tb - 1)
    filled = lambda j, s: jnp.where(s == n_steps - 1, jnp.minimum(j, n_tb - 1), drained(j))
    tok = lambda j, s: (drained(j), 0)
    tile_spec = pl.BlockSpec((R_HEADS, N_KEYS, tb), lambda j, s: (0, 0, drained(j)))
    even_spec = pl.BlockSpec((R_HEADS, SUBLANES, tb), lambda j, s: (0, 2 * s, drained(j)))
    odd_spec = pl.BlockSpec((R_HEADS, SUBLANES, tb), lambda j, s: (0, 2 * s + 1, drained(j)))
    return pl.pallas_call(
        functools.partial(_peer_kernel, eb_rows=eb_rows, tb=tb),
        grid=(n_tb + 1, n_steps),
        in_specs=[
            pl.BlockSpec((tb, d), lambda j, s: (filled(j, s), 0)),
            pl.BlockSpec((2 * eb_rows, d), lambda j, s: ((s + 1) % n_steps, 0)),
            pl.BlockSpec((d, 2 * eb_rows), lambda j, s: (0, s)),
            tile_spec, tile_spec, odd_spec, even_spec, odd_spec, even_spec,
            pl.BlockSpec((tb, d), tok),
            pl.BlockSpec((1, d), lambda j, s: (0, 0)),
        ],
        out_specs=pl.BlockSpec((tb, d), tok),
        out_shape=jax.ShapeDtypeStruct((t, d), jnp.float32),
        scratch_shapes=[
            pltpu.VMEM((d, tb), jnp.float32),
            pltpu.VMEM((eb_rows, tb), jnp.float32),
            pltpu.VMEM((eb_rows, tb), jnp.float32),
            pltpu.VMEM((eb_rows, tb), jnp.bfloat16),
            pltpu.VMEM((eb_rows, tb), jnp.bfloat16),
        ],
        compiler_params=pltpu.CompilerParams(
            dimension_semantics=("arbitrary", "arbitrary"), vmem_limit_bytes=VMEM_LIMIT),
        name="peer",
    )(h2, u_bf, vt_bf, rank2, eb, cnt, cnt, ea, ea, x1, g_final.reshape(1, -1))


def _stream(x, left, attend, p):
    b, s, d = x.shape
    q, k, v, kb, vb, conv, tail = _in_proj(x, p["g_mix"], p["w_in"], left, p["conv_w"], p["conv_b"],
                                           p["ln_g"], p["ln_b"])
    att = attend(q, kb, vb)
    c = conv.shape[-1]
    x1, h2, st = _mid(conv.reshape(b * s, c), att.reshape(b * s, c), x.reshape(b * s, d),
                      p["w_out"], p["g_ffn"], p["w_query"], p["sub_keys"])
    rank2, cnt, ea, eb = _route(st)
    y = _peer(h2, p["peer_u"], p["peer_vt"], rank2, cnt, ea, eb, x1, p["g_final"])
    k = k.reshape(b, s, N_HEADS, 2, HEAD_DIM)
    v = v.reshape(b, s, N_HEADS, V_DIM)
    return y.reshape(b, s, d), k, v, tail[:, CONV_PAD - (CONV_WIDTH - 1):]


def kernel(x_prompt, x_sample, cache_k, cache_v, state_conv, g_mix, w_in, conv_w, conv_b, conv_ln_g, conv_ln_b, lambda_q1, lambda_k1, lambda_q2, lambda_k2, subln_g, rel_bias, w_out, g_ffn, w_query, sub_keys, peer_u, peer_v, g_final):
    depth = w_in.shape[0]
    assert depth == 1, "single-layer step"
    l = 0
    bf16 = jnp.bfloat16
    b, s, d = x_prompt.shape
    bd, sd, _ = x_sample.shape
    past = cache_k.shape[2]
    c = conv_w.shape[-1]

    lam_init = _lambda_init(l)
    lam = (jnp.exp(jnp.sum(lambda_q1[l].astype(jnp.float32) * lambda_k1[l].astype(jnp.float32)))
           - jnp.exp(jnp.sum(lambda_q2[l].astype(jnp.float32) * lambda_k2[l].astype(jnp.float32)))
           + lam_init).reshape(1)
    out_scale = 1.0 - lam_init

    p = {
        "g_mix": g_mix[l], "w_in": w_in[l].astype(bf16), "conv_w": conv_w[l], "conv_b": conv_b[l],
        "ln_g": conv_ln_g[l], "ln_b": conv_ln_b[l], "w_out": w_out[l].astype(bf16), "g_ffn": g_ffn[l],
        "w_query": w_query[l].astype(bf16),
        "sub_keys": sub_keys[l].reshape(2 * R_HEADS, N_KEYS, -1).astype(bf16),
        "peer_u": peer_u[l].astype(bf16), "peer_vt": peer_v[l].astype(bf16).T, "g_final": g_final,
    }

    attend_p = lambda q, kb, vb: _attn_prompt(q, kb, vb, rel_bias, lam, subln_g[l], out_scale)
    y_p, k_p, v_p, tail_p = _stream(x_prompt, jnp.zeros((b, CONV_PAD, c), jnp.float32), attend_p, p)

    n_new = -(-sd // LANES) * LANES
    pos_s = past + jnp.arange(sd, dtype=jnp.int32)
    bias_c = _bias_table(rel_bias, pos_s, jnp.arange(past, dtype=jnp.int32), True)
    bias_n = _bias_table(rel_bias, pos_s, past + jnp.arange(n_new, dtype=jnp.int32), True)
    bias_n = jnp.where(jnp.arange(n_new) < sd, bias_n, NEG)
    ck = cache_k[l].reshape(bd, past, c)
    cv = cache_v[l].reshape(bd, past, c)

    def attend_s(q, kb, vb):
        pad = ((0, 0), (0, n_new - sd), (0, 0))
        return _attn_sample(q, ck, cv, jnp.pad(kb, pad), jnp.pad(vb, pad), bias_c, bias_n,
                            lam, subln_g[l], out_scale)

    left_s = jnp.pad(state_conv[l], ((0, 0), (CONV_PAD - (CONV_WIDTH - 1), 0), (0, 0)))
    y_s, k_s, v_s, tail_s = _stream(x_sample, left_s, attend_s, p)

    return (y_p, y_s, k_p[None], v_p[None], tail_p[None], k_s[None], v_s[None], tail_s[None])
```
